```python
import math
import jax
import jax.numpy as jnp
from jax import lax
import numpy as np

D_MODEL = 1024
BATCH = 1
SEQ = 16384
DEPTH = 2
DEC_BATCH = 16
DEC_SEQ = 16
PAST_LEN = 1024

CHUNK = 64
QBLOCK = 128
N_MIXERS = 2
N_MLA_LAYERS = (DEPTH + 1) // 2
N_SSM_LAYERS = DEPTH // 2

MLA_HEADS = 16
Q_LORA = 256
KV_LORA = 128
QK_NOPE = 64
QK_ROPE = 32
V_DIM = 64
MLA_IN_DIM = Q_LORA + KV_LORA + QK_ROPE
ROPE_BASE = 10000.0
ATTN_SCALE = (QK_NOPE + QK_ROPE) ** -0.5
NEG_INF = -1e30

SSM_WIDTH = D_MODEL
SSM_GROUP_CH = 16
SSM_GROUPS = SSM_WIDTH // SSM_GROUP_CH
SSM_STATE = 64

MOE_GROUPS = 4
MOE_EXPERTS_PER_GROUP = 8
MOE_TOP_K = 2
MOE_FF = 256

PLE_DIM = 256

ALPHA = (2.0 * DEPTH) ** 0.25
BETA = (8.0 * DEPTH) ** -0.25
LN_EPS = 1e-5
RMS_EPS = 1e-6

kernel_name = 'hybrid_mla_s5_hmoe_stream_step'


def _layernorm(x, g, b):
    xf = x.astype(jnp.float32)
    mu = jnp.mean(xf, axis=-1, keepdims=True)
    var = jnp.mean(jnp.square(xf - mu), axis=-1, keepdims=True)
    y = (xf - mu) * lax.rsqrt(var + LN_EPS) * g.astype(jnp.float32) + b.astype(jnp.float32)
    return y.astype(x.dtype)


def _rmsnorm(x, g):
    xf = x.astype(jnp.float32)
    y = xf * lax.rsqrt(jnp.mean(jnp.square(xf), axis=-1, keepdims=True) + RMS_EPS)
    return (y * g.astype(jnp.float32)).astype(x.dtype)


def _rope(x, pos):
    half = x.shape[-1] // 2
    inv = ROPE_BASE ** (-jnp.arange(half, dtype=jnp.float32) / half)
    ang = pos.astype(jnp.float32)[:, None] * inv[None, :]
    shape = (1, x.shape[1]) + (1,) * (x.ndim - 3) + (half,)
    cos = jnp.cos(ang).reshape(shape)
    sin = jnp.sin(ang).reshape(shape)
    xf = x.astype(jnp.float32)
    x1, x2 = xf[..., :half], xf[..., half:]
    return jnp.concatenate([x1 * cos - x2 * sin, x1 * sin + x2 * cos], axis=-1).astype(x.dtype)


def _chunk_causal_attention(q_nope, q_pe, k_nope, k_pe, v, q_pos, k_pos):
    k_chunk = k_pos // CHUNK

    def attend(args):
        qn, qp, qpos = args
        s = (jnp.einsum('bqhn,bkhn->bhqk', qn, k_nope)
             + jnp.einsum('bqhr,bkr->bhqk', qp, k_pe)).astype(jnp.float32) * ATTN_SCALE
        mask = k_chunk[None, :] <= (qpos // CHUNK)[:, None]
        s = jnp.where(mask[None, None], s, NEG_INF)
        pr = jax.nn.softmax(s, axis=-1).astype(v.dtype)
        return jnp.einsum('bhqk,bkhv->bqhv', pr, v)

    b, lq = q_nope.shape[0], q_nope.shape[1]
    if lq > QBLOCK:
        nb = lq // QBLOCK

        def split(t):
            return jnp.moveaxis(t.reshape((b, nb, QBLOCK) + t.shape[2:]), 1, 0)

        out = lax.map(attend, (split(q_nope), split(q_pe), q_pos.reshape(nb, QBLOCK)))
        return jnp.moveaxis(out, 0, 1).reshape((b, lq) + out.shape[3:])
    return attend((q_nope, q_pe, q_pos))


def _mla_mixer(x, cache_ckv, cache_kpe, w_in, q_norm, kv_norm, w_uq, w_uk, w_uv, w_o):
    b, l, _ = x.shape
    past = 0 if cache_ckv is None else cache_ckv.shape[1]
    q_pos = past + jnp.arange(l, dtype=jnp.int32)
    z = x @ w_in
    cq = _rmsnorm(z[..., :Q_LORA], q_norm)
    ckv = _rmsnorm(z[..., Q_LORA:Q_LORA + KV_LORA], kv_norm)
    kpe = _rope(z[..., Q_LORA + KV_LORA:], q_pos)
    q = (cq @ w_uq).reshape(b, l, MLA_HEADS, QK_NOPE + QK_ROPE)
    q_nope = q[..., :QK_NOPE]
    q_pe = _rope(q[..., QK_NOPE:], q_pos)
    if cache_ckv is None:
        ckv_all, kpe_all = ckv, kpe
    else:
        ckv_all = jnp.concatenate([cache_ckv, ckv], axis=1)
        kpe_all = jnp.concatenate([cache_kpe, kpe], axis=1)
    k_pos = jnp.arange(past + l, dtype=jnp.int32)
    k_nope = jnp.einsum('bkc,chn->bkhn', ckv_all, w_uk)
    v = jnp.einsum('bkc,chv->bkhv', ckv_all, w_uv)
    o = _chunk_causal_attention(q_nope, q_pe, k_nope, kpe_all, v, q_pos, k_pos)
    y = o.reshape(b, l, MLA_HEADS * V_DIM) @ w_o
    return y, ckv, kpe


def _ssm_combine(left, right):
    a_l, b_l = left
    a_r, b_r = right
    return a_r * a_l, a_r * b_l + b_r


def _s5_scan(u, lam_bar, b_bar, c_mat, h0):
    b, l = u.shape[0], u.shape[1]
    blk = CHUNK if l % CHUNK == 0 else l
    nb = l // blk
    u_blocks = jnp.moveaxis(u.reshape(b, nb, blk, SSM_GROUPS, SSM_GROUP_CH), 1, 0)

    def body(h, u_blk):
        bu = jnp.einsum('btgc,gpc->btgp', u_blk.astype(jnp.complex64), b_bar)
        a = jnp.broadcast_to(lam_bar, bu.shape)
        a_cum, h_loc = lax.associative_scan(_ssm_combine, (a, bu), axis=1)
        h_all = a_cum * h[:, None] + h_loc
        y = jnp.real(jnp.einsum('gcp,btgp->btgc', c_mat, h_all))
        return h_all[:, -1], y

    h_last, ys = lax.scan(body, h0, u_blocks)
    return jnp.moveaxis(ys, 0, 1).reshape(b, l, SSM_GROUPS, SSM_GROUP_CH), h_last


def _s5_mixer(x, st_re, st_im, w_in, a_re, a_im, log_dt, b_re, b_im, c_re, c_im, d_skip, w_glu):
    b, l, _ = x.shape
    f32 = jnp.float32
    u = (x @ w_in).astype(f32).reshape(b, l, SSM_GROUPS, SSM_GROUP_CH)
    lam = lax.complex(a_re.astype(f32), a_im.astype(f32))
    dt = jnp.exp(log_dt.astype(f32))[:, None]
    lam_bar = jnp.exp(lam * dt)
    b_bar = ((lam_bar - 1.0) / lam)[:, :, None] * lax.complex(b_re.astype(f32), b_im.astype(f32))
    c_mat = lax.complex(c_re.astype(f32), c_im.astype(f32))
    if st_re is None:
        h0 = jnp.zeros((b, SSM_GROUPS, SSM_STATE), jnp.complex64)
    else:
        h0 = lax.complex(st_re.astype(f32), st_im.astype(f32))
    y, h_last = _s5_scan(u, lam_bar, b_bar, c_mat, h0)
    y = y + d_skip.astype(f32).reshape(SSM_GROUPS, SSM_GROUP_CH) * u
    g = jax.nn.gelu(y).reshape(b, l, SSM_WIDTH).astype(x.dtype)
    z = g @ w_glu
    out = z[..., :D_MODEL] * jax.nn.sigmoid(z[..., D_MODEL:])
    return out, jnp.real(h_last).astype(x.dtype), jnp.imag(h_last).astype(x.dtype)


def _hier_moe(x, w_rg, b_rg, w_rexp, b_rexp, w_gate, w_up, w_down):
    shp = x.shape
    f32 = jnp.float32
    xt = x.reshape(-1, shp[-1])
    lg = (xt @ w_rg).astype(f32) + b_rg.astype(f32)
    pg = jax.nn.softmax(lg, axis=-1)
    g_sel = jnp.argmax(lg, axis=-1)
    gate_g = jnp.max(pg, axis=-1)
    le_all = jnp.einsum('td,gde->tge', xt, w_rexp).astype(f32) + b_rexp.astype(f32)
    le = jnp.take_along_axis(le_all, g_sel[:, None, None], axis=1)[:, 0]
    pe = jax.nn.softmax(le, axis=-1)
    w_top, e_top = lax.top_k(pe, MOE_TOP_K)
    w_top = w_top / jnp.sum(w_top, axis=-1, keepdims=True)
    ew = jnp.sum(jax.nn.one_hot(e_top, MOE_EXPERTS_PER_GROUP, dtype=f32) * w_top[..., None], axis=1)
    gates = (gate_g[:, None, None] * jax.nn.one_hot(g_sel, MOE_GROUPS, dtype=f32)[:, :, None]
             * ew[:, None, :]).astype(x.dtype)
    y = jnp.zeros_like(xt)
    for g in range(MOE_GROUPS):
        hdn = jax.nn.silu(jnp.einsum('td,edf->tef', xt, w_gate[g])) * jnp.einsum('td,edf->tef', xt, w_up[g])
        y = y + jnp.einsum('tef,efd->td', hdn * gates[:, g, :, None], w_down[g])
    return y.reshape(shp)


def _trunk(x, p, mla_cache, ssm_state, mla_w, ssm_w, layer_w):
    ckv_rows, kpe_rows, h_re, h_im = [], [], [], []
    for i in range(DEPTH):
        j = i // N_MIXERS
        if i % N_MIXERS == 0:
            c_ckv = None if mla_cache is None else mla_cache[0][j]
            c_kpe = None if mla_cache is None else mla_cache[1][j]
            mix, ckv, kpe = _mla_mixer(x, c_ckv, c_kpe, *[w[j] for w in mla_w])
            ckv_rows.append(ckv)
            kpe_rows.append(kpe)
        else:
            s_re = None if ssm_state is None else ssm_state[0][j]
            s_im = None if ssm_state is None else ssm_state[1][j]
            mix, hr, hi = _s5_mixer(x, s_re, s_im, *[w[j] for w in ssm_w])
            h_re.append(hr)
            h_im.append(hi)
        (ln1_g, ln1_b, ln2_g, ln2_b, w_rg, b_rg, w_rexp, b_rexp,
         w_gate, w_up, w_down, w_proj, w_pg) = [w[i] for w in layer_w]
        h = _layernorm(ALPHA * x + mix, ln1_g, ln1_b)
        h = _layernorm(ALPHA * h + _hier_moe(h, w_rg, b_rg, w_rexp, b_rexp, w_gate, w_up, w_down), ln2_g, ln2_b)
        x = h + (p[i] @ w_proj) * jax.nn.sigmoid(h @ w_pg)
    return x, jnp.stack(ckv_rows), jnp.stack(kpe_rows), jnp.stack(h_re), jnp.stack(h_im)


def setup_inputs(seed: int = 0) -> dict:
    key = jax.random.key(seed)
    ks = jax.random.split(key, 40)
    f32 = jnp.float32

    def nrm(i, shape, scale=1.0):
        return scale * jax.random.normal(ks[i], shape, f32)

    glu_val = nrm(23, (N_SSM_LAYERS, SSM_WIDTH, D_MODEL), BETA * SSM_WIDTH ** -0.5)
    glu_gate = nrm(24, (N_SSM_LAYERS, SSM_WIDTH, D_MODEL), SSM_WIDTH ** -0.5)
    return {
        'x_prompt': nrm(0, (BATCH, SEQ, D_MODEL)),
        'x_sample': nrm(1, (DEC_BATCH, DEC_SEQ, D_MODEL)),
        'p_prompt': nrm(2, (DEPTH, BATCH, SEQ, PLE_DIM)),
        'p_sample': nrm(3, (DEPTH, DEC_BATCH, DEC_SEQ, PLE_DIM)),
        'cache_mla_ckv': nrm(4, (N_MLA_LAYERS, DEC_BATCH, PAST_LEN, KV_LORA)),
        'cache_mla_kpe': nrm(5, (N_MLA_LAYERS, DEC_BATCH, PAST_LEN, QK_ROPE)),
        'state_ssm_re': nrm(6, (N_SSM_LAYERS, DEC_BATCH, SSM_GROUPS, SSM_STATE), 0.1),
        'state_ssm_im': nrm(7, (N_SSM_LAYERS, DEC_BATCH, SSM_GROUPS, SSM_STATE), 0.1),
        'mla_w_in': nrm(8, (N_MLA_LAYERS, D_MODEL, MLA_IN_DIM), D_MODEL ** -0.5),
        'mla_q_norm': 1.0 + nrm(9, (N_MLA_LAYERS, Q_LORA), 0.01),
        'mla_kv_norm': 1.0 + nrm(10, (N_MLA_LAYERS, KV_LORA), 0.01),
        'mla_w_uq': nrm(11, (N_MLA_LAYERS, Q_LORA, MLA_HEADS * (QK_NOPE + QK_ROPE)), Q_LORA ** -0.5),
        'mla_w_uk': nrm(12, (N_MLA_LAYERS, KV_LORA, MLA_HEADS, QK_NOPE), KV_LORA ** -0.5),
        'mla_w_uv': nrm(13, (N_MLA_LAYERS, KV_LORA, MLA_HEADS, V_DIM), KV_LORA ** -0.5),
        'mla_w_o': nrm(14, (N_MLA_LAYERS, MLA_HEADS * V_DIM, D_MODEL), BETA * (MLA_HEADS * V_DIM) ** -0.5),
        'ssm_w_in': nrm(15, (N_SSM_LAYERS, D_MODEL, SSM_WIDTH), D_MODEL ** -0.5),
        'ssm_a_re': -0.5 + nrm(16, (N_SSM_LAYERS, SSM_GROUPS, SSM_STATE), 0.01),
        'ssm_a_im': jnp.pi * jnp.arange(SSM_STATE, dtype=f32) + nrm(17, (N_SSM_LAYERS, SSM_GROUPS, SSM_STATE), 0.01),
        'ssm_log_dt': jax.random.uniform(ks[18], (N_SSM_LAYERS, SSM_GROUPS), f32,
                                         minval=math.log(1e-3), maxval=math.log(1e-1)),
        'ssm_b_re': nrm(19, (N_SSM_LAYERS, SSM_GROUPS, SSM_STATE, SSM_GROUP_CH), (2.0 * SSM_GROUP_CH) ** -0.5),
        'ssm_b_im': nrm(20, (N_SSM_LAYERS, SSM_GROUPS, SSM_STATE, SSM_GROUP_CH), (2.0 * SSM_GROUP_CH) ** -0.5),
        'ssm_c_re': nrm(21, (N_SSM_LAYERS, SSM_GROUPS, SSM_GROUP_CH, SSM_STATE), SSM_STATE ** -0.5),
        'ssm_c_im': nrm(22, (N_SSM_LAYERS, SSM_GROUPS, SSM_GROUP_CH, SSM_STATE), SSM_STATE ** -0.5),
        'ssm_d': nrm(25, (N_SSM_LAYERS, SSM_WIDTH)),
        'ssm_w_glu': jnp.concatenate([glu_val, glu_gate], axis=-1),
        'ln1_g': 1.0 + nrm(26, (DEPTH, D_MODEL), 0.01),
        'ln1_b': nrm(27, (DEPTH, D_MODEL), 0.01),
        'ln2_g': 1.0 + nrm(28, (DEPTH, D_MODEL), 0.01),
        'ln2_b': nrm(29, (DEPTH, D_MODEL), 0.01),
        'moe_w_rg': nrm(30, (DEPTH, D_MODEL, MOE_GROUPS), D_MODEL ** -0.5),
        'moe_b_rg': nrm(31, (DEPTH, MOE_GROUPS), 0.01),
        'moe_w_re': nrm(32, (DEPTH, MOE_GROUPS, D_MODEL, MOE_EXPERTS_PER_GROUP), D_MODEL ** -0.5),
        'moe_b_re': nrm(33, (DEPTH, MOE_GROUPS, MOE_EXPERTS_PER_GROUP), 0.01),
        'moe_w_gate': nrm(34, (DEPTH, MOE_GROUPS, MOE_EXPERTS_PER_GROUP, D_MODEL, MOE_FF), D_MODEL ** -0.5),
        'moe_w_up': nrm(35, (DEPTH, MOE_GROUPS, MOE_EXPERTS_PER_GROUP, D_MODEL, MOE_FF), D_MODEL ** -0.5),
        'moe_w_down': nrm(36, (DEPTH, MOE_GROUPS, MOE_EXPERTS_PER_GROUP, MOE_FF, D_MODEL), BETA * MOE_FF ** -0.5),
        'ple_w_proj': nrm(37, (DEPTH, PLE_DIM, D_MODEL), PLE_DIM ** -0.5),
        'ple_w_gate': nrm(38, (DEPTH, D_MODEL, D_MODEL), D_MODEL ** -0.5),
    }


def reference(x_prompt, x_sample, p_prompt, p_sample, cache_mla_ckv, cache_mla_kpe, state_ssm_re, state_ssm_im,
              mla_w_in, mla_q_norm, mla_kv_norm, mla_w_uq, mla_w_uk, mla_w_uv, mla_w_o,
              ssm_w_in, ssm_a_re, ssm_a_im, ssm_log_dt, ssm_b_re, ssm_b_im, ssm_c_re, ssm_c_im, ssm_d, ssm_w_glu,
              ln1_g, ln1_b, ln2_g, ln2_b, moe_w_rg, moe_b_rg, moe_w_re, moe_b_re,
              moe_w_gate, moe_w_up, moe_w_down, ple_w_proj, ple_w_gate):
    mla_w = (mla_w_in, mla_q_norm, mla_kv_norm, mla_w_uq, mla_w_uk, mla_w_uv, mla_w_o)
    ssm_w = (ssm_w_in, ssm_a_re, ssm_a_im, ssm_log_dt, ssm_b_re, ssm_b_im, ssm_c_re, ssm_c_im, ssm_d, ssm_w_glu)
    layer_w = (ln1_g, ln1_b, ln2_g, ln2_b, moe_w_rg, moe_b_rg, moe_w_re, moe_b_re,
               moe_w_gate, moe_w_up, moe_w_down, ple_w_proj, ple_w_gate)
    y_prompt, ckv_p, kpe_p, re_p, im_p = _trunk(x_prompt, p_prompt, None, None, mla_w, ssm_w, layer_w)
    y_sample, ckv_s, kpe_s, re_s, im_s = _trunk(x_sample, p_sample, (cache_mla_ckv, cache_mla_kpe),
                                                 (state_ssm_re, state_ssm_im), mla_w, ssm_w, layer_w)
    return (y_prompt, y_sample, ckv_p, kpe_p, re_p, im_p, ckv_s, kpe_s, re_s, im_s)
```

```python
import functools
import math

import jax
import jax.numpy as jnp
from jax import lax
from jax.experimental import pallas as pl
from jax.experimental.pallas import tpu as pltpu

F32 = jnp.float32
BF16 = jnp.bfloat16

N_HEADS = 16
Q_LORA = 256
KV_LORA = 128
QK_NOPE = 64
QK_ROPE = 32
V_DIM = 64
ROPE_BASE = 10000.0
ATTN_SCALE = (QK_NOPE + QK_ROPE) ** -0.5
CHUNK = 64
SSM_GROUP_CH = 16
SSM_STATE = 64
MOE_GROUPS = 4
MOE_EPG = 8
N_EXPERTS = MOE_GROUPS * MOE_EPG
DEPTH = 2
ALPHA = (2.0 * DEPTH) ** 0.25
LN_EPS = 1e-5
RMS_EPS = 1e-6
NEG = -1e30

LANES = 128
SLOT = LANES
ROW_TILE = 256
SSM_T = 16
VMEM_LIMIT = 56 * 1024 * 1024
ROUTER_LANES = LANES
EXPERT_LANE0 = MOE_GROUPS


def _cparams(sem):
    return pltpu.CompilerParams(dimension_semantics=sem, vmem_limit_bytes=VMEM_LIMIT)


def _dot(a, b):
    return jnp.dot(a, b, preferred_element_type=F32)


def _dot_nt(a, b):
    return lax.dot_general(a, b, (((1,), (1,)), ((), ())), preferred_element_type=F32)


def _full(shape):
    nd = len(shape)
    return pl.BlockSpec(shape, lambda *_: (0,) * nd)


def _rows(width, rows=ROW_TILE):
    return pl.BlockSpec((rows, width), lambda i: (i, 0))


def _layernorm(t, g, b):
    mu = jnp.mean(t, axis=-1, keepdims=True)
    d = t - mu
    var = jnp.mean(d * d, axis=-1, keepdims=True)
    return d * lax.rsqrt(var + LN_EPS) * g + b


def _rmsnorm(t, g):
    return t * lax.rsqrt(jnp.mean(t * t, axis=-1, keepdims=True) + RMS_EPS) * g


def _mla_proj_kernel(x_ref, ct_ref, st_ref, w_in_ref, qn_ref, kvn_ref, wqa_ref, wqb_ref, wk_ref, wv_ref,
                     vone_ref, q_ref, k_ref, v_ref, ckv_ref, kpe_ref):
    x = x_ref[...].astype(BF16)
    z = _dot(x, w_in_ref[...])
    cq = _rmsnorm(z[:, :Q_LORA], qn_ref[...])
    ckv = _rmsnorm(z[:, Q_LORA:Q_LORA + KV_LORA], kvn_ref[...])
    ct = ct_ref[...]
    st = st_ref[...]
    o = Q_LORA + KV_LORA
    kpe = z[:, o:o + SLOT] * ct + z[:, o + SLOT:o + 2 * SLOT] * st
    ckv_ref[...] = ckv
    kpe_ref[...] = kpe
    cqb = cq.astype(BF16)
    ckvb = ckv.astype(BF16)
    qa = _dot(cqb, wqa_ref[...])
    qb = _dot(cqb, wqb_ref[...])
    kn = _dot(ckvb, wk_ref[...])
    vv = _dot(ckvb, wv_ref[...]) + vone_ref[...]
    scale = ATTN_SCALE * math.log2(math.e)
    for h in range(N_HEADS):
        sl = slice(h * SLOT, (h + 1) * SLOT)
        q_ref[:, sl] = ((qa[:, sl] * ct + qb[:, sl] * st) * scale).astype(BF16)
        k_ref[:, sl] = (kn[:, sl] + kpe).astype(BF16)
    v_ref[...] = vv.astype(BF16)


def _mla_proj(x, ct, st, w):
    n = x.shape[0]
    wide = N_HEADS * SLOT
    return pl.pallas_call(
        _mla_proj_kernel,
        grid=(n // ROW_TILE,),
        in_specs=[_rows(x.shape[1]), _rows(SLOT), _rows(SLOT), _full(w["w_in"].shape), _full((1, Q_LORA)),
                  _full((1, KV_LORA)), _full(w["wqa"].shape), _full(w["wqb"].shape), _full(w["wk"].shape),
                  _full(w["wv"].shape), _full((1, wide))],
        out_specs=[_rows(wide), _rows(wide), _rows(wide), _rows(KV_LORA), _rows(SLOT)],
        out_shape=[jax.ShapeDtypeStruct((n, wide), BF16)] * 3
        + [jax.ShapeDtypeStruct((n, KV_LORA), F32), jax.ShapeDtypeStruct((n, SLOT), F32)],
        compiler_params=_cparams(("parallel",)),
        name="mla_proj",
    )(x, ct, st, w["w_in"], w["qn"], w["kvn"], w["wqa"], w["wqb"], w["wk"], w["wv"], w["vone"])


ATT_TQ = 256
ATT_TK = 256


def _attn_kernel(q_ref, k_ref, v_ref, o_ref):
    qi = pl.program_id(1)
    q = q_ref[...]

    def step(kblk, vblk, m, acc, mask):
        s = _dot_nt(q, kblk)
        if mask is not None:
            s = jnp.where(mask, s, NEG)
        m_new = jnp.maximum(m, jnp.max(s, axis=-1, keepdims=True))
        alpha = jnp.exp2(m - m_new)
        p = jnp.exp2(s - m_new)
        acc = acc * alpha + _dot(p.astype(BF16), vblk)
        return m_new, acc

    def body(j, carry):
        m, acc = carry
        off = pl.multiple_of(j * ATT_TK, ATT_TK)
        return step(k_ref[pl.ds(off, ATT_TK), :], v_ref[pl.ds(off, ATT_TK), :], m, acc, None)

    m0 = jnp.full((ATT_TQ, 1), NEG, F32)
    acc0 = jnp.zeros((ATT_TQ, SLOT), F32)
    m, acc = lax.fori_loop(0, qi * (ATT_TQ // ATT_TK), body, (m0, acc0))
    off = pl.multiple_of(qi * ATT_TQ, ATT_TQ)
    rc = lax.broadcasted_iota(jnp.int32, (ATT_TQ, ATT_TQ), 0) // CHUNK
    cc = lax.broadcasted_iota(jnp.int32, (ATT_TQ, ATT_TQ), 1) // CHUNK
    m, acc = step(k_ref[pl.ds(off, ATT_TQ), :], v_ref[pl.ds(off, ATT_TQ), :], m, acc, cc <= rc)
    lane = lax.broadcasted_iota(jnp.int32, acc.shape, 1)
    l = jnp.sum(jnp.where(lane == V_DIM, acc, 0.0), axis=-1, keepdims=True)
    o_ref[...] = (acc / l).astype(BF16)


def _prompt_attention(q, k, v, n_prompt):
    n = q.shape[0]
    return pl.pallas_call(
        _attn_kernel,
        grid=(N_HEADS, n_prompt // ATT_TQ),
        in_specs=[pl.BlockSpec((ATT_TQ, SLOT), lambda h, i: (i, h)),
                  pl.BlockSpec((n_prompt, SLOT), lambda h, i: (0, h)),
                  pl.BlockSpec((n_prompt, SLOT), lambda h, i: (0, h))],
        out_specs=pl.BlockSpec((ATT_TQ, SLOT), lambda h, i: (i, h)),
        out_shape=jax.ShapeDtypeStruct((n, N_HEADS * SLOT), BF16),
        compiler_params=_cparams(("parallel", "arbitrary")),
        name="prompt_attention",
    )(q, k, v)


def _sample_attn_kernel(o_in_ref, q_ref, ckvc_ref, kpec_ref, ckvn_ref, kpen_ref, wabs_ref, wuv_ref, o_ref):
    del o_in_ref
    q = q_ref[...]
    seq = q.shape[0]
    qa, qp = [], []
    for h in range(N_HEADS):
        qh = q[:, h * SLOT:(h + 1) * SLOT]
        qa.append(_dot(qh, wabs_ref[h]))
        qp.append(qh[:, QK_NOPE:QK_NOPE + QK_ROPE])
    qa = jnp.concatenate(qa, axis=0).astype(BF16)
    qp = jnp.concatenate(qp, axis=0)
    ckvc = ckvc_ref[0].astype(BF16)
    kpec = kpec_ref[0].astype(BF16)
    ckvn = ckvn_ref[...].astype(BF16)
    kpen = kpen_ref[...][:, QK_NOPE:QK_NOPE + QK_ROPE].astype(BF16)
    s_c = _dot_nt(qa, ckvc) + _dot_nt(qp, kpec)
    s_n = _dot_nt(qa, ckvn) + _dot_nt(qp, kpen)
    m = jnp.maximum(jnp.max(s_c, axis=-1, keepdims=True), jnp.max(s_n, axis=-1, keepdims=True))
    p_c = jnp.exp2(s_c - m)
    p_n = jnp.exp2(s_n - m)
    l = jnp.sum(p_c, axis=-1, keepdims=True) + jnp.sum(p_n, axis=-1, keepdims=True)
    ol = (_dot(p_c.astype(BF16), ckvc) + _dot(p_n.astype(BF16), ckvn)) / l
    olb = ol.astype(BF16)
    for h in range(N_HEADS):
        o_ref[:, h * SLOT:(h + 1) * SLOT] = _dot(olb[h * seq:(h + 1) * seq], wuv_ref[h]).astype(BF16)


def _sample_attention(o_all, q, cache_ckv, cache_kpe, ckv_new, kpe_new, wabs, wuv, n_prompt):
    nb, past, _ = cache_ckv.shape
    seq = (q.shape[0] - n_prompt) // nb
    base = n_prompt // seq
    wide = N_HEADS * SLOT
    return pl.pallas_call(
        _sample_attn_kernel,
        grid=(nb,),
        in_specs=[pl.BlockSpec(memory_space=pl.ANY),
                  pl.BlockSpec((seq, wide), lambda b: (base + b, 0)),
                  pl.BlockSpec((1, past, KV_LORA), lambda b: (b, 0, 0)),
                  pl.BlockSpec((1, past, QK_ROPE), lambda b: (b, 0, 0)),
                  pl.BlockSpec((seq, KV_LORA), lambda b: (base + b, 0)),
                  pl.BlockSpec((seq, SLOT), lambda b: (base + b, 0)),
                  _full(wabs.shape), _full(wuv.shape)],
        out_specs=pl.BlockSpec((seq, wide), lambda b: (base + b, 0)),
        out_shape=jax.ShapeDtypeStruct(o_all.shape, o_all.dtype),
        input_output_aliases={0: 0},
        compiler_params=_cparams(("parallel",)),
        name="sample_attention",
    )(o_all, q, cache_ckv, cache_kpe, ckv_new, kpe_new, wabs, wuv)


def _router_gates(lg):
    lane = lax.broadcasted_iota(jnp.int32, lg.shape, 1)
    lanef = lane.astype(F32)
    big = float(ROUTER_LANES)
    is_g = lane < MOE_GROUPS
    gl = jnp.where(is_g, lg, NEG)
    gmax = jnp.max(gl, axis=-1, keepdims=True)
    gsel = jnp.min(jnp.where(gl == gmax, lanef, big), axis=-1, keepdims=True)
    gate_g = 1.0 / jnp.sum(jnp.where(is_g, jnp.exp(gl - gmax), 0.0), axis=-1, keepdims=True)
    lo = EXPERT_LANE0 + MOE_EPG * gsel
    el = jnp.where(jnp.logical_and(lanef >= lo, lanef < lo + MOE_EPG), lg, NEG)
    m1 = jnp.max(el, axis=-1, keepdims=True)
    i1 = jnp.min(jnp.where(el == m1, lanef, big), axis=-1, keepdims=True)
    el2 = jnp.where(lanef == i1, NEG, el)
    m2 = jnp.max(el2, axis=-1, keepdims=True)
    i2 = jnp.min(jnp.where(el2 == m2, lanef, big), axis=-1, keepdims=True)
    r = jnp.exp(m2 - m1)
    w1 = 1.0 / (1.0 + r)
    w2 = r / (1.0 + r)
    return gate_g * (jnp.where(lanef == i1, w1, 0.0) + jnp.where(lanef == i2, w2, 0.0))


def _post_mix(t, x, g, b, wr_hi, wr_lo, br, h_ref, hb_ref, gates_ref):
    h = _layernorm(ALPHA * x + t, g, b)
    h_ref[...] = h
    hb = h.astype(BF16)
    hb_ref[...] = hb
    h_lo = (h - hb.astype(F32)).astype(BF16)
    lg = _dot(hb, wr_hi) + (_dot(hb, wr_lo) + _dot(h_lo, wr_hi)) + br
    gates_ref[...] = _router_gates(lg)


def _attn_out_kernel(o_ref, x_ref, wo_ref, g_ref, b_ref, wrh_ref, wrl_ref, br_ref, h_ref, hb_ref, gates_ref):
    t = _dot(o_ref[...], wo_ref[...])
    _post_mix(t, x_ref[...], g_ref[...], b_ref[...], wrh_ref[...], wrl_ref[...], br_ref[...],
              h_ref, hb_ref, gates_ref)


def _post_out(n, d):
    specs = [_rows(d), _rows(d), _rows(ROUTER_LANES)]
    shapes = [jax.ShapeDtypeStruct((n, d), F32), jax.ShapeDtypeStruct((n, d), BF16),
              jax.ShapeDtypeStruct((n, ROUTER_LANES), F32)]
    return specs, shapes


def _attn_out(o_all, x, wo, lw):
    n, d = x.shape
    specs, shapes = _post_out(n, d)
    return pl.pallas_call(
        _attn_out_kernel,
        grid=(n // ROW_TILE,),
        in_specs=[_rows(o_all.shape[1]), _rows(d), _full(wo.shape), _full((1, d)), _full((1, d)),
                  _full(lw["wr_hi"].shape), _full(lw["wr_lo"].shape), _full((1, ROUTER_LANES))],
        out_specs=specs, out_shape=shapes,
        compiler_params=_cparams(("parallel",)),
        name="attn_out_ln_router",
    )(o_all, x, wo, lw["ln1_g"], lw["ln1_b"], lw["wr_hi"], lw["wr_lo"], lw["br"])


def _moe_dense_kernel(hb_ref, gates_ref, wg_ref, wu_ref, wd_ref, y_ref):
    e = pl.program_id(1)
    x = hb_ref[...]
    g = _dot(x, wg_ref[0])
    u = _dot(x, wu_ref[0])
    gates = gates_ref[...]
    lane = lax.broadcasted_iota(jnp.int32, gates.shape, 1)
    ge = jnp.sum(jnp.where(lane == e + EXPERT_LANE0, gates, 0.0), axis=-1, keepdims=True)
    hdn = (g * jax.nn.sigmoid(g)) * u * ge
    y = _dot(hdn.astype(BF16), wd_ref[0])

    @pl.when(e == 0)
    def _():
        y_ref[...] = y

    @pl.when(e > 0)
    def _():
        y_ref[...] += y


def _moe_dense(hb, gates, wg, wu, wd, tile):
    n, d = hb.shape
    ff = wg.shape[-1]
    return pl.pallas_call(
        _moe_dense_kernel,
        grid=(n // tile, N_EXPERTS),
        in_specs=[pl.BlockSpec((tile, d), lambda i, e: (i, 0)),
                  pl.BlockSpec((tile, ROUTER_LANES), lambda i, e: (i, 0)),
                  pl.BlockSpec((1, d, ff), lambda i, e: (e, 0, 0)),
                  pl.BlockSpec((1, d, ff), lambda i, e: (e, 0, 0)),
                  pl.BlockSpec((1, ff, d), lambda i, e: (e, 0, 0))],
        out_specs=pl.BlockSpec((tile, d), lambda i, e: (i, 0)),
        out_shape=jax.ShapeDtypeStruct((n, d), F32),
        compiler_params=_cparams(("parallel", "arbitrary")),
        name="moe_dense",
    )(hb, gates, wg, wu, wd)


def _ln2_ple_kernel(h_ref, y_ref, p_ref, g_ref, b_ref, wp_ref, wpg_ref, x_ref):
    h2 = _layernorm(ALPHA * h_ref[...] + y_ref[...], g_ref[...], b_ref[...])
    proj = _dot(p_ref[...].astype(BF16), wp_ref[...])
    gate = jax.nn.sigmoid(_dot(h2.astype(BF16), wpg_ref[...]))
    x_ref[...] = h2 + proj * gate


def _ln2_ple(h, y, p, lw):
    n, d = h.shape
    return pl.pallas_call(
        _ln2_ple_kernel,
        grid=(n // ROW_TILE,),
        in_specs=[_rows(d), _rows(d), _rows(p.shape[1]), _full((1, d)), _full((1, d)),
                  _full(lw["w_proj"].shape), _full(lw["w_pg"].shape)],
        out_specs=_rows(d),
        out_shape=jax.ShapeDtypeStruct((n, d), F32),
        compiler_params=_cparams(("parallel",)),
        name="ln2_ple",
    )(h, y, p, lw["ln2_g"], lw["ln2_b"], lw["w_proj"], lw["w_pg"])


def _ssm_in_kernel(x_ref, w_ref, u_ref):
    u_ref[...] = _dot(x_ref[...].astype(BF16), w_ref[...])


def _ssm_in(x, w):
    n, d = x.shape
    return pl.pallas_call(
        _ssm_in_kernel,
        grid=(n // ROW_TILE,),
        in_specs=[_rows(d), _full(w.shape)],
        out_specs=_rows(w.shape[1]),
        out_shape=jax.ShapeDtypeStruct((n, w.shape[1]), F32),
        compiler_params=_cparams(("parallel",)),
        name="ssm_in",
    )(x, w)


def _ssm_core_kernel(u_ref, m_ref, bp_ref, cp_ref, apow_ref, h0_ref, y_ref, hp_ref, hs_ref, sre_ref, sim_ref,
                     *, n_chunks, levels):
    pad = sre_ref.shape[0] - n_chunks
    u0 = u_ref[0]
    u1 = u_ref[1]
    bb = _dot(u0, bp_ref[0]) + _dot(u1, bp_ref[1])
    re = bb[:n_chunks, :LANES]
    im = bb[:n_chunks, LANES:]
    zeros = jnp.zeros((pad, LANES), F32)
    sre_ref[:pad, :] = zeros
    sim_ref[:pad, :] = zeros

    def shifted(d):
        return sre_ref[pl.ds(pad - d, n_chunks), :], sim_ref[pl.ds(pad - d, n_chunks), :]

    for lv in range(levels):
        d = 1 << lv
        sre_ref[pad:, :] = re
        sim_ref[pad:, :] = im
        pr, pi = shifted(d)
        ar = apow_ref[0, lv:lv + 1, :LANES]
        ai = apow_ref[0, lv:lv + 1, LANES:]
        re, im = re + ar * pr - ai * pi, im + ar * pi + ai * pr
    sre_ref[pad:, :] = re
    sim_ref[pad:, :] = im
    pr, pi = shifted(1)
    h0 = h0_ref[0]
    hprev = jnp.concatenate([jnp.concatenate([pr, pi], axis=1), h0], axis=0).astype(BF16)
    y_ref[0] = _dot(u0, m_ref[0]) + _dot(hprev, cp_ref[0])
    y_ref[1] = _dot(u1, m_ref[1]) + _dot(hprev, cp_ref[1])
    hp_ref[0] = jnp.concatenate([re[n_chunks - 1:, :], im[n_chunks - 1:, :]], axis=1)
    ar = apow_ref[0, 0:1, :LANES]
    ai = apow_ref[0, 0:1, LANES:]
    h0r = h0[:, :LANES]
    h0i = h0[:, LANES:]
    hs_ref[0] = jnp.concatenate([ar * h0r - ai * h0i + bb[n_chunks:, :LANES],
                                 ar * h0i + ai * h0r + bb[n_chunks:, LANES:]], axis=1)


def _ssm_core(ublk, tabs, h0, n_chunks):
    g, rows, width = ublk.shape
    pairs = g // 2
    ns = rows - n_chunks
    levels = max(1, (n_chunks - 1).bit_length())
    pad = 1 << (levels - 1)
    pad = max(pad, 8)
    kern = functools.partial(_ssm_core_kernel, n_chunks=n_chunks, levels=levels)
    pair3 = lambda a, b: pl.BlockSpec((2, a, b), lambda i: (i, 0, 0))
    one3 = lambda a, b: pl.BlockSpec((1, a, b), lambda i: (i, 0, 0))
    return pl.pallas_call(
        kern,
        grid=(pairs,),
        in_specs=[pair3(rows, width), pair3(width, width), pair3(width, width), pair3(width, width),
                  one3(tabs["apow"].shape[1], width), one3(ns, width)],
        out_specs=[pair3(rows, width), one3(1, width), one3(ns, width)],
        out_shape=[jax.ShapeDtypeStruct((g, rows, width), F32), jax.ShapeDtypeStruct((pairs, 1, width), F32),
                   jax.ShapeDtypeStruct((pairs, ns, width), F32)],
        scratch_shapes=[pltpu.VMEM((pad + n_chunks, LANES), F32), pltpu.VMEM((pad + n_chunks, LANES), F32)],
        compiler_params=_cparams(("parallel",)),
        name="ssm_core",
    )(ublk, tabs["mbig"], tabs["bpair"], tabs["cpair"], tabs["apow"], h0)


def _gelu_tanh(y):
    c = math.sqrt(2.0 / math.pi)
    return 0.5 * y * (1.0 + jnp.tanh(c * (y + 0.044715 * (y * y * y))))


def _ssm_out_kernel(y_ref, u_ref, x_ref, d_ref, wglu_ref, g_ref, b_ref, wrh_ref, wrl_ref, br_ref,
                    h_ref, hb_ref, gates_ref):
    y = y_ref[...] + d_ref[...] * u_ref[...]
    z = _dot(_gelu_tanh(y).astype(BF16), wglu_ref[...])
    dm = z.shape[1] // 2
    t = z[:, :dm] * jax.nn.sigmoid(z[:, dm:])
    _post_mix(t, x_ref[...], g_ref[...], b_ref[...], wrh_ref[...], wrl_ref[...], br_ref[...],
              h_ref, hb_ref, gates_ref)


def _ssm_out(y, u, x, d_skip, wglu, lw):
    n, d = x.shape
    specs, shapes = _post_out(n, d)
    return pl.pallas_call(
        _ssm_out_kernel,
        grid=(n // ROW_TILE,),
        in_specs=[_rows(d), _rows(d), _rows(d), _full((1, d)), _full(wglu.shape), _full((1, d)), _full((1, d)),
                  _full(lw["wr_hi"].shape), _full(lw["wr_lo"].shape), _full((1, ROUTER_LANES))],
        out_specs=specs, out_shape=shapes,
        compiler_params=_cparams(("parallel",)),
        name="ssm_out_ln_router",
    )(y, u, x, d_skip, wglu, lw["ln1_g"], lw["ln1_b"], lw["wr_hi"], lw["wr_lo"], lw["br"])


def _rope_tables(pos):
    half = QK_ROPE // 2
    inv = ROPE_BASE ** (-jnp.arange(half, dtype=F32) / half)
    ang = pos.astype(F32)[:, None] * inv[None, :]
    cos, sin = jnp.cos(ang), jnp.sin(ang)
    n = pos.shape[0]
    ones = jnp.ones((n, QK_NOPE), F32)
    zeros = jnp.zeros((n, SLOT - QK_NOPE - QK_ROPE), F32)
    ct = jnp.concatenate([ones, cos, cos, zeros], axis=1)
    st = jnp.concatenate([jnp.zeros((n, QK_NOPE), F32), -sin, sin, zeros], axis=1)
    return ct, st


def _slot_cols(w, width):
    k = w.shape[0]
    return jnp.pad(w, ((0, 0), (0, 0), (0, SLOT - width))).reshape(k, N_HEADS * SLOT)


def _mla_tables(w_in, q_norm, kv_norm, w_uq, w_uk, w_uv, w_o):
    half = QK_ROPE // 2
    o = Q_LORA + KV_LORA
    d = w_in.shape[0]
    kpe_w = w_in[:, o:]
    kpe_sw = jnp.concatenate([kpe_w[:, half:], kpe_w[:, :half]], axis=1)
    zl = jnp.zeros((d, QK_NOPE), F32)
    zr = jnp.zeros((d, SLOT - QK_NOPE - QK_ROPE), F32)
    w_in_e = jnp.concatenate([w_in[:, :o], zl, kpe_w, zr, zl, kpe_sw, zr], axis=1)
    wq = w_uq.reshape(Q_LORA, N_HEADS, QK_NOPE + QK_ROPE)
    pe = wq[:, :, QK_NOPE:]
    pe_sw = jnp.concatenate([pe[:, :, half:], pe[:, :, :half]], axis=2)
    wqb = jnp.concatenate([jnp.zeros_like(wq[:, :, :QK_NOPE]), pe_sw], axis=2)
    vone = jnp.zeros((N_HEADS, SLOT), F32).at[:, V_DIM].set(1.0).reshape(1, N_HEADS * SLOT)
    wabs = jnp.pad(jnp.transpose(w_uk, (1, 2, 0)), ((0, 0), (0, SLOT - QK_NOPE), (0, 0)))
    wuv = jnp.pad(jnp.transpose(w_uv, (1, 0, 2)), ((0, 0), (0, 0), (0, SLOT - V_DIM)))
    wo = jnp.pad(w_o.reshape(N_HEADS, V_DIM, -1), ((0, 0), (0, SLOT - V_DIM), (0, 0)))
    return dict(
        w_in=w_in_e.astype(BF16), qn=q_norm.reshape(1, -1), kvn=kv_norm.reshape(1, -1),
        wqa=_slot_cols(wq, QK_NOPE + QK_ROPE).astype(BF16), wqb=_slot_cols(wqb, QK_NOPE + QK_ROPE).astype(BF16),
        wk=_slot_cols(w_uk, QK_NOPE).astype(BF16), wv=_slot_cols(w_uv, V_DIM).astype(BF16), vone=vone,
        wabs=wabs.astype(BF16), wuv=wuv.astype(BF16), wo=wo.reshape(N_HEADS * SLOT, -1).astype(BF16))


def _ssm_tables(a_re, a_im, log_dt, b_re, b_im, c_re, c_im, levels):
    t = SSM_T
    g, p = a_re.shape
    c = SSM_GROUP_CH
    lam = lax.complex(a_re, a_im)
    dt = jnp.exp(log_dt)[:, None]
    lam_bar = jnp.exp(lam * dt)
    b_bar = ((lam_bar - 1.0) / lam)[:, :, None] * lax.complex(b_re, b_im)
    c_mat = lax.complex(c_re, c_im)
    pw = [jnp.ones_like(lam_bar)]
    for _ in range(t):
        pw.append(pw[-1] * lam_bar)
    pw = jnp.stack(pw)
    taps = jnp.real(jnp.einsum("gop,tgp,gpi->tgoi", c_mat, pw[:t], b_bar, precision=lax.Precision.HIGHEST))
    lag = jnp.arange(t)[None, :] - jnp.arange(t)[:, None]
    blocks = jnp.where((lag >= 0)[:, :, None, None, None], taps[jnp.clip(lag, 0, t - 1)], 0.0)
    mbig = jnp.transpose(blocks, (2, 0, 4, 1, 3)).reshape(g, t * c, t * c)
    bfl = pw[t - 1 - jnp.arange(t)][:, :, :, None] * b_bar[None]
    bfl = jnp.transpose(bfl, (1, 0, 3, 2)).reshape(g, t * c, p)
    par = (jnp.arange(g) % 2)[:, None, None]
    z = jnp.zeros_like(jnp.real(bfl))

    def pair_cols(v):
        return jnp.where(par == 0, jnp.concatenate([v, z], axis=2), jnp.concatenate([z, v], axis=2))

    bpair = jnp.concatenate([pair_cols(jnp.real(bfl)), pair_cols(jnp.imag(bfl))], axis=2)
    cfl = c_mat[None] * pw[1:t + 1][:, :, None, :]
    cfl = jnp.transpose(cfl, (1, 3, 0, 2)).reshape(g, p, t * c)
    zc = jnp.zeros_like(jnp.real(cfl))

    def pair_rows(v):
        return jnp.where(par == 0, jnp.concatenate([v, zc], axis=1), jnp.concatenate([zc, v], axis=1))

    cpair = jnp.concatenate([pair_rows(jnp.real(cfl)), pair_rows(-jnp.imag(cfl))], axis=1)
    a = pw[t]
    ap = [a]
    for _ in range(levels - 1):
        ap.append(ap[-1] * ap[-1])
    ap = jnp.stack(ap, axis=1).reshape(g // 2, 2, levels, p)
    ap = jnp.transpose(ap, (0, 2, 1, 3)).reshape(g // 2, levels, 2 * p)
    apow = jnp.concatenate([jnp.real(ap), jnp.imag(ap)], axis=2)
    rows = -(-levels // 8) * 8
    apow = jnp.pad(apow, ((0, 0), (0, rows - levels), (0, 0)))
    return dict(mbig=mbig.astype(BF16), bpair=bpair.astype(BF16), cpair=cpair.astype(BF16), apow=apow)


def _pair_states(re, im):
    b, g, p = re.shape
    r = jnp.transpose(re.reshape(b, g // 2, 2 * p), (1, 0, 2))
    i = jnp.transpose(im.reshape(b, g // 2, 2 * p), (1, 0, 2))
    return jnp.concatenate([r, i], axis=2)


def _unpair_states(h):
    pairs, b, w = h.shape
    p = w // 4
    r = jnp.transpose(h[:, :, :2 * p], (1, 0, 2)).reshape(b, pairs * 2, p)
    i = jnp.transpose(h[:, :, 2 * p:], (1, 0, 2)).reshape(b, pairs * 2, p)
    return r, i


def _layer_tables(i, ln1_g, ln1_b, ln2_g, ln2_b, w_rg, b_rg, w_re, b_re, w_gate, w_up, w_down, w_proj, w_pg):
    d = w_rg.shape[1]
    wr = jnp.concatenate([w_rg[i], jnp.transpose(w_re[i], (1, 0, 2)).reshape(d, N_EXPERTS)], axis=1)
    wr = jnp.pad(wr, ((0, 0), (0, ROUTER_LANES - wr.shape[1])))
    wr_hi = wr.astype(BF16)
    wr_lo = (wr - wr_hi.astype(F32)).astype(BF16)
    br = jnp.pad(jnp.concatenate([b_rg[i], b_re[i].reshape(-1)]), (0, ROUTER_LANES - MOE_GROUPS - N_EXPERTS))
    ff = w_gate.shape[-1]
    return dict(
        ln1_g=ln1_g[i].reshape(1, d), ln1_b=ln1_b[i].reshape(1, d), ln2_g=ln2_g[i].reshape(1, d),
        ln2_b=ln2_b[i].reshape(1, d), wr_hi=wr_hi, wr_lo=wr_lo, br=br.reshape(1, ROUTER_LANES),
        wg=w_gate[i].reshape(N_EXPERTS, d, ff).astype(BF16), wu=w_up[i].reshape(N_EXPERTS, d, ff).astype(BF16),
        wd=w_down[i].reshape(N_EXPERTS, ff, d).astype(BF16),
        w_proj=w_proj[i].astype(BF16), w_pg=w_pg[i].astype(BF16))


def _moe_tile(n):
    best = LANES
    for t in range(LANES, 2048 + 1, LANES):
        if n % t == 0:
            best = t
    return best


def kernel(x_prompt, x_sample, p_prompt, p_sample, cache_mla_ckv, cache_mla_kpe, state_ssm_re, state_ssm_im, mla_w_in, mla_q_norm, mla_kv_norm, mla_w_uq, mla_w_uk, mla_w_uv, mla_w_o, ssm_w_in, ssm_a_re, ssm_a_im, ssm_log_dt, ssm_b_re, ssm_b_im, ssm_c_re, ssm_c_im, ssm_d, ssm_w_glu, ln1_g, ln1_b, ln2_g, ln2_b, moe_w_rg, moe_b_rg, moe_w_re, moe_b_re, moe_w_gate, moe_w_up, moe_w_down, ple_w_proj, ple_w_gate):
    bp, n_prompt, d = x_prompt.shape
    nb, seq, _ = x_sample.shape
    past = cache_mla_ckv.shape[2]
    assert bp == 1 and seq == SSM_T and n_prompt % ROW_TILE == 0 and (nb * seq) % ROW_TILE == 0
    assert n_prompt % CHUNK == 0 and past % CHUNK == 0 and seq <= CHUNK
    n_samp = nb * seq
    n = n_prompt + n_samp
    x = jnp.concatenate([x_prompt.reshape(n_prompt, d), x_sample.reshape(n_samp, d)], axis=0)
    p_all = jnp.concatenate([p_prompt.reshape(DEPTH, n_prompt, -1), p_sample.reshape(DEPTH, n_samp, -1)], axis=1)
    layer_args = (ln1_g, ln1_b, ln2_g, ln2_b, moe_w_rg, moe_b_rg, moe_w_re, moe_b_re, moe_w_gate, moe_w_up,
                  moe_w_down, ple_w_proj, ple_w_gate)
    moe_tile = _moe_tile(n)

    def ffn(h, hb, gates, p, lw):
        y = _moe_dense(hb, gates, lw["wg"], lw["wu"], lw["wd"], moe_tile)
        return _ln2_ple(h, y, p, lw)

    lw = _layer_tables(0, *layer_args)
    mw = _mla_tables(mla_w_in[0], mla_q_norm[0], mla_kv_norm[0], mla_w_uq[0], mla_w_uk[0], mla_w_uv[0], mla_w_o[0])
    pos = jnp.concatenate([jnp.arange(n_prompt, dtype=jnp.int32),
                           past + jnp.tile(jnp.arange(seq, dtype=jnp.int32), nb)])
    ct, st = _rope_tables(pos)
    q, k, v, ckv, kpe_slot = _mla_proj(x, ct, st, mw)
    o_all = _prompt_attention(q, k, v, n_prompt)
    o_all = _sample_attention(o_all, q, cache_mla_ckv[0], cache_mla_kpe[0], ckv, kpe_slot, mw["wabs"], mw["wuv"],
                              n_prompt)
    h, hb, gates = _attn_out(o_all, x, mw["wo"], lw)
    x = ffn(h, hb, gates, p_all[0], lw)
    kpe = kpe_slot[:, QK_NOPE:QK_NOPE + QK_ROPE]

    lw = _layer_tables(1, *layer_args)
    n_chunks = n_prompt // SSM_T
    rows = n // SSM_T
    groups = d // SSM_GROUP_CH
    levels = max(1, (n_chunks - 1).bit_length())
    tabs = _ssm_tables(ssm_a_re[0], ssm_a_im[0], ssm_log_dt[0], ssm_b_re[0], ssm_b_im[0], ssm_c_re[0],
                       ssm_c_im[0], levels)
    u = _ssm_in(x, ssm_w_in[0].astype(BF16))
    ublk = jnp.transpose(u.reshape(rows, SSM_T, groups, SSM_GROUP_CH), (2, 0, 1, 3))
    ublk = ublk.reshape(groups, rows, SSM_T * SSM_GROUP_CH).astype(BF16)
    h0 = _pair_states(state_ssm_re[0], state_ssm_im[0])
    yblk, hp, hs = _ssm_core(ublk, tabs, h0, n_chunks)
    y = jnp.transpose(yblk.reshape(groups, rows, SSM_T, SSM_GROUP_CH), (1, 2, 0, 3)).reshape(n, d)
    h, hb, gates = _ssm_out(y, u, x, ssm_d[0].reshape(1, d), ssm_w_glu[0].astype(BF16), lw)
    x = ffn(h, hb, gates, p_all[1], lw)
    re_p, im_p = _unpair_states(hp)
    re_s, im_s = _unpair_states(hs)

    return (x[:n_prompt].reshape(1, n_prompt, d), x[n_prompt:].reshape(nb, seq, d),
            ckv[:n_prompt].reshape(1, 1, n_prompt, KV_LORA), kpe[:n_prompt].reshape(1, 1, n_prompt, QK_ROPE),
            re_p[None], im_p[None],
            ckv[n_prompt:].reshape(1, nb, seq, KV_LORA), kpe[n_prompt:].reshape(1, nb, seq, QK_ROPE),
            re_s[None], im_s[None])
```

```python
import functools
import math

import jax
import jax.numpy as jnp
from jax import lax
from jax.experimental import pallas as pl
from jax.experimental.pallas import tpu as pltpu

F32 = jnp.float32
BF16 = jnp.bfloat16

N_HEADS = 16
Q_LORA = 256
KV_LORA = 128
QK_NOPE = 64
QK_ROPE = 32
V_DIM = 64
ROPE_BASE = 10000.0
ATTN_SCALE = (QK_NOPE + QK_ROPE) ** -0.5
CHUNK = 64
SSM_GROUP_CH = 16
SSM_STATE = 64
MOE_GROUPS = 4
MOE_EPG = 8
N_EXPERTS = MOE_GROUPS * MOE_EPG
DEPTH = 2
ALPHA = (2.0 * DEPTH) ** 0.25
LN_EPS = 1e-5
RMS_EPS = 1e-6
NEG = -1e30

LANES = 128
SLOT = LANES
ROW_TILE = 256
SSM_T = 16
VMEM_LIMIT = 56 * 1024 * 1024
ROUTER_LANES = LANES
EXPERT_LANE0 = MOE_GROUPS


def _cparams(sem):
    return pltpu.CompilerParams(dimension_semantics=sem, vmem_limit_bytes=VMEM_LIMIT)


def _dot(a, b):
    return jnp.dot(a, b, preferred_element_type=F32)


def _dot_nt(a, b):
    return lax.dot_general(a, b, (((1,), (1,)), ((), ())), preferred_element_type=F32)


def _full(shape):
    nd = len(shape)
    return pl.BlockSpec(shape, lambda *_: (0,) * nd)


def _rows(width, rows=ROW_TILE):
    return pl.BlockSpec((rows, width), lambda i: (i, 0))


def _layernorm(t, g, b):
    mu = jnp.mean(t, axis=-1, keepdims=True)
    d = t - mu
    var = jnp.mean(d * d, axis=-1, keepdims=True)
    return d * lax.rsqrt(var + LN_EPS) * g + b


def _rmsnorm(t, g):
    return t * lax.rsqrt(jnp.mean(t * t, axis=-1, keepdims=True) + RMS_EPS) * g


def _mla_proj_kernel(x_ref, ct_ref, st_ref, w_in_ref, qn_ref, kvn_ref, wqa_ref, wqb_ref, wk_ref, wv_ref,
                     vone_ref, q_ref, k_ref, v_ref, ckv_ref, kpe_ref):
    x = x_ref[...].astype(BF16)
    z = _dot(x, w_in_ref[...])
    cq = _rmsnorm(z[:, :Q_LORA], qn_ref[...])
    ckv = _rmsnorm(z[:, Q_LORA:Q_LORA + KV_LORA], kvn_ref[...])
    ct = ct_ref[...]
    st = st_ref[...]
    o = Q_LORA + KV_LORA
    kpe = z[:, o:o + SLOT] * ct + z[:, o + SLOT:o + 2 * SLOT] * st
    ckv_ref[...] = ckv
    kpe_ref[...] = kpe
    cqb = cq.astype(BF16)
    ckvb = ckv.astype(BF16)
    qa = _dot(cqb, wqa_ref[...])
    qb = _dot(cqb, wqb_ref[...])
    kn = _dot(ckvb, wk_ref[...])
    vv = _dot(ckvb, wv_ref[...]) + vone_ref[...]
    scale = ATTN_SCALE * math.log2(math.e)
    for h in range(N_HEADS):
        sl = slice(h * SLOT, (h + 1) * SLOT)
        q_ref[:, sl] = ((qa[:, sl] * ct + qb[:, sl] * st) * scale).astype(BF16)
        k_ref[:, sl] = (kn[:, sl] + kpe).astype(BF16)
    v_ref[...] = vv.astype(BF16)


def _mla_proj(x, ct, st, w):
    n = x.shape[0]
    wide = N_HEADS * SLOT
    return pl.pallas_call(
        _mla_proj_kernel,
        grid=(n // ROW_TILE,),
        in_specs=[_rows(x.shape[1]), _rows(SLOT), _rows(SLOT), _full(w["w_in"].shape), _full((1, Q_LORA)),
                  _full((1, KV_LORA)), _full(w["wqa"].shape), _full(w["wqb"].shape), _full(w["wk"].shape),
                  _full(w["wv"].shape), _full((1, wide))],
        out_specs=[_rows(wide), _rows(wide), _rows(wide), _rows(KV_LORA), _rows(SLOT)],
        out_shape=[jax.ShapeDtypeStruct((n, wide), BF16)] * 3
        + [jax.ShapeDtypeStruct((n, KV_LORA), F32), jax.ShapeDtypeStruct((n, SLOT), F32)],
        compiler_params=_cparams(("parallel",)),
        name="mla_proj",
    )(x, ct, st, w["w_in"], w["qn"], w["kvn"], w["wqa"], w["wqb"], w["wk"], w["wv"], w["vone"])


ATT_TQ = 512
ATT_TK = 512
ATT_G = 2


def _attn_kernel(q_ref, k_ref, v_ref, o_ref):
    qi = pl.program_id(1)
    qs = [q_ref[:, g * SLOT:(g + 1) * SLOT] for g in range(ATT_G)]

    def step(off, carry, mask):
        out = []
        for g in range(ATT_G):
            m, acc = carry[g]
            sl = slice(g * SLOT, (g + 1) * SLOT)
            s = _dot_nt(qs[g], k_ref[pl.ds(off, ATT_TK), sl])
            if mask is not None:
                s = jnp.where(mask, s, NEG)
            m_new = jnp.maximum(m, jnp.max(s, axis=-1, keepdims=True))
            p = jnp.exp2(s - m_new)
            pv = _dot(p.astype(BF16), v_ref[pl.ds(off, ATT_TK), sl])
            out.append((m_new, acc * jnp.exp2(m - m_new) + pv))
        return tuple(out)

    def body(j, carry):
        return step(pl.multiple_of(j * ATT_TK, ATT_TK), carry, None)

    init = tuple((jnp.full((ATT_TQ, 1), NEG, F32), jnp.zeros((ATT_TQ, SLOT), F32)) for _ in range(ATT_G))
    carry = lax.fori_loop(0, qi * (ATT_TQ // ATT_TK), body, init)
    rc = lax.broadcasted_iota(jnp.int32, (ATT_TQ, ATT_TQ), 0) // CHUNK
    cc = lax.broadcasted_iota(jnp.int32, (ATT_TQ, ATT_TQ), 1) // CHUNK
    carry = step(pl.multiple_of(qi * ATT_TQ, ATT_TQ), carry, cc <= rc)
    lane = lax.broadcasted_iota(jnp.int32, (ATT_TQ, SLOT), 1)
    for g in range(ATT_G):
        acc = carry[g][1]
        l = jnp.sum(jnp.where(lane == V_DIM, acc, 0.0), axis=-1, keepdims=True)
        o_ref[:, g * SLOT:(g + 1) * SLOT] = (acc / l).astype(BF16)


def _prompt_attention(q, k, v, n_prompt):
    n = q.shape[0]
    wide = ATT_G * SLOT
    assert ATT_TQ == ATT_TK and n_prompt % ATT_TQ == 0 and N_HEADS % ATT_G == 0
    return pl.pallas_call(
        _attn_kernel,
        grid=(N_HEADS // ATT_G, n_prompt // ATT_TQ),
        in_specs=[pl.BlockSpec((ATT_TQ, wide), lambda h, i: (i, h)),
                  pl.BlockSpec((n_prompt, wide), lambda h, i: (0, h)),
                  pl.BlockSpec((n_prompt, wide), lambda h, i: (0, h))],
        out_specs=pl.BlockSpec((ATT_TQ, wide), lambda h, i: (i, h)),
        out_shape=jax.ShapeDtypeStruct((n, N_HEADS * SLOT), BF16),
        compiler_params=_cparams(("parallel", "arbitrary")),
        name="prompt_attention",
    )(q, k, v)


def _sample_attn_kernel(o_in_ref, q_ref, ckvc_ref, kpec_ref, ckvn_ref, kpen_ref, wabs_ref, wuv_ref, o_ref):
    del o_in_ref
    q = q_ref[...]
    seq = q.shape[0]
    qa, qp = [], []
    for h in range(N_HEADS):
        qh = q[:, h * SLOT:(h + 1) * SLOT]
        qa.append(_dot(qh, wabs_ref[h]))
        qp.append(qh[:, QK_NOPE:QK_NOPE + QK_ROPE])
    qa = jnp.concatenate(qa, axis=0).astype(BF16)
    qp = jnp.concatenate(qp, axis=0)
    ckvc = ckvc_ref[0].astype(BF16)
    kpec = kpec_ref[0].astype(BF16)
    ckvn = ckvn_ref[...].astype(BF16)
    kpen = kpen_ref[...][:, QK_NOPE:QK_NOPE + QK_ROPE].astype(BF16)
    s_c = _dot_nt(qa, ckvc) + _dot_nt(qp, kpec)
    s_n = _dot_nt(qa, ckvn) + _dot_nt(qp, kpen)
    m = jnp.maximum(jnp.max(s_c, axis=-1, keepdims=True), jnp.max(s_n, axis=-1, keepdims=True))
    p_c = jnp.exp2(s_c - m)
    p_n = jnp.exp2(s_n - m)
    l = jnp.sum(p_c, axis=-1, keepdims=True) + jnp.sum(p_n, axis=-1, keepdims=True)
    ol = (_dot(p_c.astype(BF16), ckvc) + _dot(p_n.astype(BF16), ckvn)) / l
    olb = ol.astype(BF16)
    for h in range(N_HEADS):
        o_ref[:, h * SLOT:(h + 1) * SLOT] = _dot(olb[h * seq:(h + 1) * seq], wuv_ref[h]).astype(BF16)


def _sample_attention(o_all, q, cache_ckv, cache_kpe, ckv_new, kpe_new, wabs, wuv, n_prompt):
    nb, past, _ = cache_ckv.shape
    seq = (q.shape[0] - n_prompt) // nb
    base = n_prompt // seq
    wide = N_HEADS * SLOT
    return pl.pallas_call(
        _sample_attn_kernel,
        grid=(nb,),
        in_specs=[pl.BlockSpec(memory_space=pl.ANY),
                  pl.BlockSpec((seq, wide), lambda b: (base + b, 0)),
                  pl.BlockSpec((1, past, KV_LORA), lambda b: (b, 0, 0)),
                  pl.BlockSpec((1, past, QK_ROPE), lambda b: (b, 0, 0)),
                  pl.BlockSpec((seq, KV_LORA), lambda b: (base + b, 0)),
                  pl.BlockSpec((seq, SLOT), lambda b: (base + b, 0)),
                  _full(wabs.shape), _full(wuv.shape)],
        out_specs=pl.BlockSpec((seq, wide), lambda b: (base + b, 0)),
        out_shape=jax.ShapeDtypeStruct(o_all.shape, o_all.dtype),
        input_output_aliases={0: 0},
        compiler_params=_cparams(("parallel",)),
        name="sample_attention",
    )(o_all, q, cache_ckv, cache_kpe, ckv_new, kpe_new, wabs, wuv)


def _router_gates(lg):
    lane = lax.broadcasted_iota(jnp.int32, lg.shape, 1)
    lanef = lane.astype(F32)
    big = float(ROUTER_LANES)
    is_g = lane < MOE_GROUPS
    gl = jnp.where(is_g, lg, NEG)
    gmax = jnp.max(gl, axis=-1, keepdims=True)
    gsel = jnp.min(jnp.where(gl == gmax, lanef, big), axis=-1, keepdims=True)
    gate_g = 1.0 / jnp.sum(jnp.where(is_g, jnp.exp(gl - gmax), 0.0), axis=-1, keepdims=True)
    lo = EXPERT_LANE0 + MOE_EPG * gsel
    el = jnp.where(jnp.logical_and(lanef >= lo, lanef < lo + MOE_EPG), lg, NEG)
    m1 = jnp.max(el, axis=-1, keepdims=True)
    i1 = jnp.min(jnp.where(el == m1, lanef, big), axis=-1, keepdims=True)
    el2 = jnp.where(lanef == i1, NEG, el)
    m2 = jnp.max(el2, axis=-1, keepdims=True)
    i2 = jnp.min(jnp.where(el2 == m2, lanef, big), axis=-1, keepdims=True)
    r = jnp.exp(m2 - m1)
    w1 = 1.0 / (1.0 + r)
    w2 = r / (1.0 + r)
    return gate_g * (jnp.where(lanef == i1, w1, 0.0) + jnp.where(lanef == i2, w2, 0.0))


def _post_mix(t, x, g, b, wr_hi, wr_lo, br, h_ref, hb_ref, gates_ref):
    h = _layernorm(ALPHA * x + t, g, b)
    h_ref[...] = h
    hb = h.astype(BF16)
    hb_ref[...] = hb
    h_lo = (h - hb.astype(F32)).astype(BF16)
    lg = _dot(hb, wr_hi) + (_dot(hb, wr_lo) + _dot(h_lo, wr_hi)) + br
    gates_ref[...] = _router_gates(lg)


def _attn_out_kernel(o_ref, x_ref, wo_ref, g_ref, b_ref, wrh_ref, wrl_ref, br_ref, h_ref, hb_ref, gates_ref):
    t = _dot(o_ref[...], wo_ref[...])
    _post_mix(t, x_ref[...], g_ref[...], b_ref[...], wrh_ref[...], wrl_ref[...], br_ref[...],
              h_ref, hb_ref, gates_ref)


def _post_out(n, d):
    specs = [_rows(d), _rows(d), _rows(ROUTER_LANES)]
    shapes = [jax.ShapeDtypeStruct((n, d), F32), jax.ShapeDtypeStruct((n, d), BF16),
              jax.ShapeDtypeStruct((n, ROUTER_LANES), F32)]
    return specs, shapes


def _attn_out(o_all, x, wo, lw):
    n, d = x.shape
    specs, shapes = _post_out(n, d)
    return pl.pallas_call(
        _attn_out_kernel,
        grid=(n // ROW_TILE,),
        in_specs=[_rows(o_all.shape[1]), _rows(d), _full(wo.shape), _full((1, d)), _full((1, d)),
                  _full(lw["wr_hi"].shape), _full(lw["wr_lo"].shape), _full((1, ROUTER_LANES))],
        out_specs=specs, out_shape=shapes,
        compiler_params=_cparams(("parallel",)),
        name="attn_out_ln_router",
    )(o_all, x, wo, lw["ln1_g"], lw["ln1_b"], lw["wr_hi"], lw["wr_lo"], lw["br"])


def _moe_dense_kernel(hb_ref, gates_ref, wg_ref, wu_ref, wd_ref, y_ref):
    e = pl.program_id(1)
    x = hb_ref[...]
    g = _dot(x, wg_ref[0])
    u = _dot(x, wu_ref[0])
    gates = gates_ref[...]
    lane = lax.broadcasted_iota(jnp.int32, gates.shape, 1)
    ge = jnp.sum(jnp.where(lane == e + EXPERT_LANE0, gates, 0.0), axis=-1, keepdims=True)
    hdn = (g * jax.nn.sigmoid(g)) * u * ge
    y = _dot(hdn.astype(BF16), wd_ref[0])

    @pl.when(e == 0)
    def _():
        y_ref[...] = y

    @pl.when(e > 0)
    def _():
        y_ref[...] += y


def _moe_dense(hb, gates, wg, wu, wd, tile):
    n, d = hb.shape
    ff = wg.shape[-1]
    return pl.pallas_call(
        _moe_dense_kernel,
        grid=(n // tile, N_EXPERTS),
        in_specs=[pl.BlockSpec((tile, d), lambda i, e: (i, 0)),
                  pl.BlockSpec((tile, ROUTER_LANES), lambda i, e: (i, 0)),
                  pl.BlockSpec((1, d, ff), lambda i, e: (e, 0, 0)),
                  pl.BlockSpec((1, d, ff), lambda i, e: (e, 0, 0)),
                  pl.BlockSpec((1, ff, d), lambda i, e: (e, 0, 0))],
        out_specs=pl.BlockSpec((tile, d), lambda i, e: (i, 0)),
        out_shape=jax.ShapeDtypeStruct((n, d), F32),
        compiler_params=_cparams(("parallel", "arbitrary")),
        name="moe_dense",
    )(hb, gates, wg, wu, wd)


def _ln2_ple_kernel(h_ref, y_ref, p_ref, g_ref, b_ref, wp_ref, wpg_ref, x_ref):
    h2 = _layernorm(ALPHA * h_ref[...] + y_ref[...], g_ref[...], b_ref[...])
    proj = _dot(p_ref[...].astype(BF16), wp_ref[...])
    gate = jax.nn.sigmoid(_dot(h2.astype(BF16), wpg_ref[...]))
    x_ref[...] = h2 + proj * gate


def _ln2_ple(h, y, p, lw):
    n, d = h.shape
    return pl.pallas_call(
        _ln2_ple_kernel,
        grid=(n // ROW_TILE,),
        in_specs=[_rows(d), _rows(d), _rows(p.shape[1]), _full((1, d)), _full((1, d)),
                  _full(lw["w_proj"].shape), _full(lw["w_pg"].shape)],
        out_specs=_rows(d),
        out_shape=jax.ShapeDtypeStruct((n, d), F32),
        compiler_params=_cparams(("parallel",)),
        name="ln2_ple",
    )(h, y, p, lw["ln2_g"], lw["ln2_b"], lw["w_proj"], lw["w_pg"])


def _ssm_in_kernel(x_ref, w_ref, u_ref):
    u_ref[...] = _dot(x_ref[...].astype(BF16), w_ref[...])


def _ssm_in(x, w):
    n, d = x.shape
    return pl.pallas_call(
        _ssm_in_kernel,
        grid=(n // ROW_TILE,),
        in_specs=[_rows(d), _full(w.shape)],
        out_specs=_rows(w.shape[1]),
        out_shape=jax.ShapeDtypeStruct((n, w.shape[1]), F32),
        compiler_params=_cparams(("parallel",)),
        name="ssm_in",
    )(x, w)


def _ssm_core_kernel(u_ref, m_ref, bp_ref, cp_ref, apow_ref, h0_ref, y_ref, hp_ref, hs_ref, sre_ref, sim_ref,
                     *, n_chunks, levels):
    pad = sre_ref.shape[0] - n_chunks
    u0 = u_ref[0]
    u1 = u_ref[1]
    bb = _dot(u0, bp_ref[0]) + _dot(u1, bp_ref[1])
    re = bb[:n_chunks, :LANES]
    im = bb[:n_chunks, LANES:]
    zeros = jnp.zeros((pad, LANES), F32)
    sre_ref[:pad, :] = zeros
    sim_ref[:pad, :] = zeros

    def shifted(d):
        return sre_ref[pl.ds(pad - d, n_chunks), :], sim_ref[pl.ds(pad - d, n_chunks), :]

    for lv in range(levels):
        d = 1 << lv
        sre_ref[pad:, :] = re
        sim_ref[pad:, :] = im
        pr, pi = shifted(d)
        ar = apow_ref[0, lv:lv + 1, :LANES]
        ai = apow_ref[0, lv:lv + 1, LANES:]
        re, im = re + ar * pr - ai * pi, im + ar * pi + ai * pr
    sre_ref[pad:, :] = re
    sim_ref[pad:, :] = im
    pr, pi = shifted(1)
    h0 = h0_ref[0]
    hprev = jnp.concatenate([jnp.concatenate([pr, pi], axis=1), h0], axis=0).astype(BF16)
    y_ref[0] = _dot(u0, m_ref[0]) + _dot(hprev, cp_ref[0])
    y_ref[1] = _dot(u1, m_ref[1]) + _dot(hprev, cp_ref[1])
    hp_ref[0] = jnp.concatenate([re[n_chunks - 1:, :], im[n_chunks - 1:, :]], axis=1)
    ar = apow_ref[0, 0:1, :LANES]
    ai = apow_ref[0, 0:1, LANES:]
    h0r = h0[:, :LANES]
    h0i = h0[:, LANES:]
    hs_ref[0] = jnp.concatenate([ar * h0r - ai * h0i + bb[n_chunks:, :LANES],
                                 ar * h0i + ai * h0r + bb[n_chunks:, LANES:]], axis=1)


def _ssm_core(ublk, tabs, h0, n_chunks):
    g, rows, width = ublk.shape
    pairs = g // 2
    ns = rows - n_chunks
    levels = max(1, (n_chunks - 1).bit_length())
    pad = 1 << (levels - 1)
    pad = max(pad, 8)
    kern = functools.partial(_ssm_core_kernel, n_chunks=n_chunks, levels=levels)
    pair3 = lambda a, b: pl.BlockSpec((2, a, b), lambda i: (i, 0, 0))
    one3 = lambda a, b: pl.BlockSpec((1, a, b), lambda i: (i, 0, 0))
    return pl.pallas_call(
        kern,
        grid=(pairs,),
        in_specs=[pair3(rows, width), pair3(width, width), pair3(width, width), pair3(width, width),
                  one3(tabs["apow"].shape[1], width), one3(ns, width)],
        out_specs=[pair3(rows, width), one3(1, width), one3(ns, width)],
        out_shape=[jax.ShapeDtypeStruct((g, rows, width), F32), jax.ShapeDtypeStruct((pairs, 1, width), F32),
                   jax.ShapeDtypeStruct((pairs, ns, width), F32)],
        scratch_shapes=[pltpu.VMEM((pad + n_chunks, LANES), F32), pltpu.VMEM((pad + n_chunks, LANES), F32)],
        compiler_params=_cparams(("parallel",)),
        name="ssm_core",
    )(ublk, tabs["mbig"], tabs["bpair"], tabs["cpair"], tabs["apow"], h0)


def _gelu_tanh(y):
    c = math.sqrt(2.0 / math.pi)
    return 0.5 * y * (1.0 + jnp.tanh(c * (y + 0.044715 * (y * y * y))))


def _ssm_out_kernel(y_ref, u_ref, x_ref, d_ref, wglu_ref, g_ref, b_ref, wrh_ref, wrl_ref, br_ref,
                    h_ref, hb_ref, gates_ref):
    y = y_ref[...] + d_ref[...] * u_ref[...]
    z = _dot(_gelu_tanh(y).astype(BF16), wglu_ref[...])
    dm = z.shape[1] // 2
    t = z[:, :dm] * jax.nn.sigmoid(z[:, dm:])
    _post_mix(t, x_ref[...], g_ref[...], b_ref[...], wrh_ref[...], wrl_ref[...], br_ref[...],
              h_ref, hb_ref, gates_ref)


def _ssm_out(y, u, x, d_skip, wglu, lw):
    n, d = x.shape
    specs, shapes = _post_out(n, d)
    return pl.pallas_call(
        _ssm_out_kernel,
        grid=(n // ROW_TILE,),
        in_specs=[_rows(d), _rows(d), _rows(d), _full((1, d)), _full(wglu.shape), _full((1, d)), _full((1, d)),
                  _full(lw["wr_hi"].shape), _full(lw["wr_lo"].shape), _full((1, ROUTER_LANES))],
        out_specs=specs, out_shape=shapes,
        compiler_params=_cparams(("parallel",)),
        name="ssm_out_ln_router",
    )(y, u, x, d_skip, wglu, lw["ln1_g"], lw["ln1_b"], lw["wr_hi"], lw["wr_lo"], lw["br"])


def _rope_tables(pos):
    half = QK_ROPE // 2
    inv = ROPE_BASE ** (-jnp.arange(half, dtype=F32) / half)
    ang = pos.astype(F32)[:, None] * inv[None, :]
    cos, sin = jnp.cos(ang), jnp.sin(ang)
    n = pos.shape[0]
    ones = jnp.ones((n, QK_NOPE), F32)
    zeros = jnp.zeros((n, SLOT - QK_NOPE - QK_ROPE), F32)
    ct = jnp.concatenate([ones, cos, cos, zeros], axis=1)
    st = jnp.concatenate([jnp.zeros((n, QK_NOPE), F32), -sin, sin, zeros], axis=1)
    return ct, st


def _slot_cols(w, width):
    k = w.shape[0]
    return jnp.pad(w, ((0, 0), (0, 0), (0, SLOT - width))).reshape(k, N_HEADS * SLOT)


def _mla_tables(w_in, q_norm, kv_norm, w_uq, w_uk, w_uv, w_o):
    half = QK_ROPE // 2
    o = Q_LORA + KV_LORA
    d = w_in.shape[0]
    kpe_w = w_in[:, o:]
    kpe_sw = jnp.concatenate([kpe_w[:, half:], kpe_w[:, :half]], axis=1)
    zl = jnp.zeros((d, QK_NOPE), F32)
    zr = jnp.zeros((d, SLOT - QK_NOPE - QK_ROPE), F32)
    w_in_e = jnp.concatenate([w_in[:, :o], zl, kpe_w, zr, zl, kpe_sw, zr], axis=1)
    wq = w_uq.reshape(Q_LORA, N_HEADS, QK_NOPE + QK_ROPE)
    pe = wq[:, :, QK_NOPE:]
    pe_sw = jnp.concatenate([pe[:, :, half:], pe[:, :, :half]], axis=2)
    wqb = jnp.concatenate([jnp.zeros_like(wq[:, :, :QK_NOPE]), pe_sw], axis=2)
    vone = jnp.zeros((N_HEADS, SLOT), F32).at[:, V_DIM].set(1.0).reshape(1, N_HEADS * SLOT)
    wabs = jnp.pad(jnp.transpose(w_uk, (1, 2, 0)), ((0, 0), (0, SLOT - QK_NOPE), (0, 0)))
    wuv = jnp.pad(jnp.transpose(w_uv, (1, 0, 2)), ((0, 0), (0, 0), (0, SLOT - V_DIM)))
    wo = jnp.pad(w_o.reshape(N_HEADS, V_DIM, -1), ((0, 0), (0, SLOT - V_DIM), (0, 0)))
    return dict(
        w_in=w_in_e.astype(BF16), qn=q_norm.reshape(1, -1), kvn=kv_norm.reshape(1, -1),
        wqa=_slot_cols(wq, QK_NOPE + QK_ROPE).astype(BF16), wqb=_slot_cols(wqb, QK_NOPE + QK_ROPE).astype(BF16),
        wk=_slot_cols(w_uk, QK_NOPE).astype(BF16), wv=_slot_cols(w_uv, V_DIM).astype(BF16), vone=vone,
        wabs=wabs.astype(BF16), wuv=wuv.astype(BF16), wo=wo.reshape(N_HEADS * SLOT, -1).astype(BF16))


def _cmul(a, b):
    return a[0] * b[0] - a[1] * b[1], a[0] * b[1] + a[1] * b[0]


def _ssm_tables(a_re, a_im, log_dt, b_re, b_im, c_re, c_im, levels):
    t = SSM_T
    g, p = a_re.shape
    c = SSM_GROUP_CH
    hi = lax.Precision.HIGHEST
    dt = jnp.exp(log_dt)[:, None]
    mag = jnp.exp(a_re * dt)
    lam_bar = (mag * jnp.cos(a_im * dt), mag * jnp.sin(a_im * dt))
    den = a_re * a_re + a_im * a_im
    quo = (((lam_bar[0] - 1.0) * a_re + lam_bar[1] * a_im) / den,
           (lam_bar[1] * a_re - (lam_bar[0] - 1.0) * a_im) / den)
    b_bar = _cmul((quo[0][:, :, None], quo[1][:, :, None]), (b_re, b_im))
    pw = [(jnp.ones_like(a_re), jnp.zeros_like(a_re))]
    for _ in range(t):
        pw.append(_cmul(pw[-1], lam_bar))
    pw_r = jnp.stack([x[0] for x in pw])
    pw_i = jnp.stack([x[1] for x in pw])
    cp = _cmul((c_re[None], c_im[None]), (pw_r[:t, :, None, :], pw_i[:t, :, None, :]))
    taps = (jnp.einsum("tgop,gpi->tgoi", cp[0], b_bar[0], precision=hi)
            - jnp.einsum("tgop,gpi->tgoi", cp[1], b_bar[1], precision=hi))
    lag = jnp.arange(t)[None, :] - jnp.arange(t)[:, None]
    blocks = jnp.where((lag >= 0)[:, :, None, None, None], taps[jnp.clip(lag, 0, t - 1)], 0.0)
    mbig = jnp.transpose(blocks, (2, 0, 4, 1, 3)).reshape(g, t * c, t * c)
    rev = t - 1 - jnp.arange(t)
    bfl = _cmul((pw_r[rev][:, :, :, None], pw_i[rev][:, :, :, None]), (b_bar[0][None], b_bar[1][None]))
    bfl = [jnp.transpose(x, (1, 0, 3, 2)).reshape(g, t * c, p) for x in bfl]
    par = (jnp.arange(g) % 2)[:, None, None]
    z = jnp.zeros_like(bfl[0])

    def pair_cols(v):
        return jnp.where(par == 0, jnp.concatenate([v, z], axis=2), jnp.concatenate([z, v], axis=2))

    bpair = jnp.concatenate([pair_cols(bfl[0]), pair_cols(bfl[1])], axis=2)
    cfl = _cmul((c_re[None], c_im[None]), (pw_r[1:t + 1, :, None, :], pw_i[1:t + 1, :, None, :]))
    cfl = [jnp.transpose(x, (1, 3, 0, 2)).reshape(g, p, t * c) for x in cfl]
    zc = jnp.zeros_like(cfl[0])

    def pair_rows(v):
        return jnp.where(par == 0, jnp.concatenate([v, zc], axis=1), jnp.concatenate([zc, v], axis=1))

    cpair = jnp.concatenate([pair_rows(cfl[0]), pair_rows(-cfl[1])], axis=1)
    ap = [pw[t]]
    for _ in range(levels - 1):
        ap.append(_cmul(ap[-1], ap[-1]))

    def pair_lanes(xs):
        v = jnp.stack(xs, axis=1).reshape(g // 2, 2, levels, p)
        return jnp.transpose(v, (0, 2, 1, 3)).reshape(g // 2, levels, 2 * p)

    apow = jnp.concatenate([pair_lanes([x[0] for x in ap]), pair_lanes([x[1] for x in ap])], axis=2)
    rows = -(-levels // 8) * 8
    apow = jnp.pad(apow, ((0, 0), (0, rows - levels), (0, 0)))
    return dict(mbig=mbig.astype(BF16), bpair=bpair.astype(BF16), cpair=cpair.astype(BF16), apow=apow)


def _pair_states(re, im):
    b, g, p = re.shape
    r = jnp.transpose(re.reshape(b, g // 2, 2 * p), (1, 0, 2))
    i = jnp.transpose(im.reshape(b, g // 2, 2 * p), (1, 0, 2))
    return jnp.concatenate([r, i], axis=2)


def _unpair_states(h):
    pairs, b, w = h.shape
    p = w // 4
    r = jnp.transpose(h[:, :, :2 * p], (1, 0, 2)).reshape(b, pairs * 2, p)
    i = jnp.transpose(h[:, :, 2 * p:], (1, 0, 2)).reshape(b, pairs * 2, p)
    return r, i


def _layer_tables(i, ln1_g, ln1_b, ln2_g, ln2_b, w_rg, b_rg, w_re, b_re, w_gate, w_up, w_down, w_proj, w_pg):
    d = w_rg.shape[1]
    wr = jnp.concatenate([w_rg[i], jnp.transpose(w_re[i], (1, 0, 2)).reshape(d, N_EXPERTS)], axis=1)
    wr = jnp.pad(wr, ((0, 0), (0, ROUTER_LANES - wr.shape[1])))
    wr_hi = wr.astype(BF16)
    wr_lo = (wr - wr_hi.astype(F32)).astype(BF16)
    br = jnp.pad(jnp.concatenate([b_rg[i], b_re[i].reshape(-1)]), (0, ROUTER_LANES - MOE_GROUPS - N_EXPERTS))
    ff = w_gate.shape[-1]
    return dict(
        ln1_g=ln1_g[i].reshape(1, d), ln1_b=ln1_b[i].reshape(1, d), ln2_g=ln2_g[i].reshape(1, d),
        ln2_b=ln2_b[i].reshape(1, d), wr_hi=wr_hi, wr_lo=wr_lo, br=br.reshape(1, ROUTER_LANES),
        wg=w_gate[i].reshape(N_EXPERTS, d, ff).astype(BF16), wu=w_up[i].reshape(N_EXPERTS, d, ff).astype(BF16),
        wd=w_down[i].reshape(N_EXPERTS, ff, d).astype(BF16),
        w_proj=w_proj[i].astype(BF16), w_pg=w_pg[i].astype(BF16))


def _moe_tile(n):
    best = LANES
    for t in range(LANES, 2048 + 1, LANES):
        if n % t == 0:
            best = t
    return best


def kernel(x_prompt, x_sample, p_prompt, p_sample, cache_mla_ckv, cache_mla_kpe, state_ssm_re, state_ssm_im, mla_w_in, mla_q_norm, mla_kv_norm, mla_w_uq, mla_w_uk, mla_w_uv, mla_w_o, ssm_w_in, ssm_a_re, ssm_a_im, ssm_log_dt, ssm_b_re, ssm_b_im, ssm_c_re, ssm_c_im, ssm_d, ssm_w_glu, ln1_g, ln1_b, ln2_g, ln2_b, moe_w_rg, moe_b_rg, moe_w_re, moe_b_re, moe_w_gate, moe_w_up, moe_w_down, ple_w_proj, ple_w_gate):
    bp, n_prompt, d = x_prompt.shape
    nb, seq, _ = x_sample.shape
    past = cache_mla_ckv.shape[2]
    assert bp == 1 and seq == SSM_T and n_prompt % ROW_TILE == 0 and (nb * seq) % ROW_TILE == 0
    assert n_prompt % CHUNK == 0 and past % CHUNK == 0 and seq <= CHUNK
    n_samp = nb * seq
    n = n_prompt + n_samp
    x = jnp.concatenate([x_prompt.reshape(n_prompt, d), x_sample.reshape(n_samp, d)], axis=0)
    p_all = jnp.concatenate([p_prompt.reshape(DEPTH, n_prompt, -1), p_sample.reshape(DEPTH, n_samp, -1)], axis=1)
    layer_args = (ln1_g, ln1_b, ln2_g, ln2_b, moe_w_rg, moe_b_rg, moe_w_re, moe_b_re, moe_w_gate, moe_w_up,
                  moe_w_down, ple_w_proj, ple_w_gate)
    moe_tile = _moe_tile(n)

    def ffn(h, hb, gates, p, lw):
        y = _moe_dense(hb, gates, lw["wg"], lw["wu"], lw["wd"], moe_tile)
        return _ln2_ple(h, y, p, lw)

    lw = _layer_tables(0, *layer_args)
    mw = _mla_tables(mla_w_in[0], mla_q_norm[0], mla_kv_norm[0], mla_w_uq[0], mla_w_uk[0], mla_w_uv[0], mla_w_o[0])
    pos = jnp.concatenate([jnp.arange(n_prompt, dtype=jnp.int32),
                           past + jnp.tile(jnp.arange(seq, dtype=jnp.int32), nb)])
    ct, st = _rope_tables(pos)
    q, k, v, ckv, kpe_slot = _mla_proj(x, ct, st, mw)
    o_all = _prompt_attention(q, k, v, n_prompt)
    o_all = _sample_attention(o_all, q, cache_mla_ckv[0], cache_mla_kpe[0], ckv, kpe_slot, mw["wabs"], mw["wuv"],
                              n_prompt)
    h, hb, gates = _attn_out(o_all, x, mw["wo"], lw)
    x = ffn(h, hb, gates, p_all[0], lw)
    kpe = kpe_slot[:, QK_NOPE:QK_NOPE + QK_ROPE]

    lw = _layer_tables(1, *layer_args)
    n_chunks = n_prompt // SSM_T
    rows = n // SSM_T
    groups = d // SSM_GROUP_CH
    levels = max(1, (n_chunks - 1).bit_length())
    tabs = _ssm_tables(ssm_a_re[0], ssm_a_im[0], ssm_log_dt[0], ssm_b_re[0], ssm_b_im[0], ssm_c_re[0],
                       ssm_c_im[0], levels)
    u = _ssm_in(x, ssm_w_in[0].astype(BF16))
    ublk = jnp.transpose(u.reshape(rows, SSM_T, groups, SSM_GROUP_CH), (2, 0, 1, 3))
    ublk = ublk.reshape(groups, rows, SSM_T * SSM_GROUP_CH).astype(BF16)
    h0 = _pair_states(state_ssm_re[0], state_ssm_im[0])
    yblk, hp, hs = _ssm_core(ublk, tabs, h0, n_chunks)
    y = jnp.transpose(yblk.reshape(groups, rows, SSM_T, SSM_GROUP_CH), (1, 2, 0, 3)).reshape(n, d)
    h, hb, gates = _ssm_out(y, u, x, ssm_d[0].reshape(1, d), ssm_w_glu[0].astype(BF16), lw)
    x = ffn(h, hb, gates, p_all[1], lw)
    re_p, im_p = _unpair_states(hp)
    re_s, im_s = _unpair_states(hs)

    return (x[:n_prompt].reshape(1, n_prompt, d), x[n_prompt:].reshape(nb, seq, d),
            ckv[:n_prompt].reshape(1, 1, n_prompt, KV_LORA), kpe[:n_prompt].reshape(1, 1, n_prompt, QK_ROPE),
            re_p[None], im_p[None],
            ckv[n_prompt:].reshape(1, nb, seq, KV_LORA), kpe[n_prompt:].reshape(1, nb, seq, QK_ROPE),
            re_s[None], im_s[None])
```

```python
import functools
import math

import jax
import jax.numpy as jnp
from jax import lax
from jax.experimental import pallas as pl
from jax.experimental.pallas import tpu as pltpu

F32 = jnp.float32
BF16 = jnp.bfloat16

N_HEADS = 16
Q_LORA = 256
KV_LORA = 128
QK_NOPE = 64
QK_ROPE = 32
V_DIM = 64
ROPE_BASE = 10000.0
ATTN_SCALE = (QK_NOPE + QK_ROPE) ** -0.5
CHUNK = 64
SSM_GROUP_CH = 16
SSM_STATE = 64
MOE_GROUPS = 4
MOE_EPG = 8
N_EXPERTS = MOE_GROUPS * MOE_EPG
DEPTH = 2
ALPHA = (2.0 * DEPTH) ** 0.25
LN_EPS = 1e-5
RMS_EPS = 1e-6
NEG = -1e30

LANES = 128
SLOT = LANES
ROW_TILE = 256
SSM_T = 16
VMEM_LIMIT = 56 * 1024 * 1024
ROUTER_LANES = LANES
EXPERT_LANE0 = MOE_GROUPS


def _cparams(sem):
    return pltpu.CompilerParams(dimension_semantics=sem, vmem_limit_bytes=VMEM_LIMIT)


def _dot(a, b):
    return jnp.dot(a, b, preferred_element_type=F32)


def _dot_nt(a, b):
    return lax.dot_general(a, b, (((1,), (1,)), ((), ())), preferred_element_type=F32)


def _full(shape):
    nd = len(shape)
    return pl.BlockSpec(shape, lambda *_: (0,) * nd)


def _rows(width, rows=ROW_TILE):
    return pl.BlockSpec((rows, width), lambda i: (i, 0))


def _layernorm(t, g, b):
    mu = jnp.mean(t, axis=-1, keepdims=True)
    d = t - mu
    var = jnp.mean(d * d, axis=-1, keepdims=True)
    return d * lax.rsqrt(var + LN_EPS) * g + b


def _rmsnorm(t, g):
    return t * lax.rsqrt(jnp.mean(t * t, axis=-1, keepdims=True) + RMS_EPS) * g


def _mla_proj_kernel(x_ref, ct_ref, st_ref, w_in_ref, qn_ref, kvn_ref, wqa_ref, wqb_ref, wk_ref, wv_ref,
                     vone_ref, q_ref, k_ref, v_ref, ckv_ref, kpe_ref):
    x = x_ref[...].astype(BF16)
    z = _dot(x, w_in_ref[...])
    cq = _rmsnorm(z[:, :Q_LORA], qn_ref[...])
    ckv = _rmsnorm(z[:, Q_LORA:Q_LORA + KV_LORA], kvn_ref[...])
    ct = ct_ref[...]
    st = st_ref[...]
    o = Q_LORA + KV_LORA
    kpe = z[:, o:o + SLOT] * ct + z[:, o + SLOT:o + 2 * SLOT] * st
    ckv_ref[...] = ckv
    kpe_ref[...] = kpe
    cqb = cq.astype(BF16)
    ckvb = ckv.astype(BF16)
    qa = _dot(cqb, wqa_ref[...])
    qb = _dot(cqb, wqb_ref[...])
    kn = _dot(ckvb, wk_ref[...])
    vv = _dot(ckvb, wv_ref[...]) + vone_ref[...]
    scale = ATTN_SCALE * math.log2(math.e)
    for h in range(N_HEADS):
        sl = slice(h * SLOT, (h + 1) * SLOT)
        q_ref[:, sl] = ((qa[:, sl] * ct + qb[:, sl] * st) * scale).astype(BF16)
        k_ref[:, sl] = (kn[:, sl] + kpe).astype(BF16)
    v_ref[...] = vv.astype(BF16)


def _mla_proj(x, ct, st, w):
    n = x.shape[0]
    wide = N_HEADS * SLOT
    return pl.pallas_call(
        _mla_proj_kernel,
        grid=(n // ROW_TILE,),
        in_specs=[_rows(x.shape[1]), _rows(SLOT), _rows(SLOT), _full(w["w_in"].shape), _full((1, Q_LORA)),
                  _full((1, KV_LORA)), _full(w["wqa"].shape), _full(w["wqb"].shape), _full(w["wk"].shape),
                  _full(w["wv"].shape), _full((1, wide))],
        out_specs=[_rows(wide), _rows(wide), _rows(wide), _rows(KV_LORA), _rows(SLOT)],
        out_shape=[jax.ShapeDtypeStruct((n, wide), BF16)] * 3
        + [jax.ShapeDtypeStruct((n, KV_LORA), F32), jax.ShapeDtypeStruct((n, SLOT), F32)],
        compiler_params=_cparams(("parallel",)),
        name="mla_proj",
    )(x, ct, st, w["w_in"], w["qn"], w["kvn"], w["wqa"], w["wqb"], w["wk"], w["wv"], w["vone"])


ATT_TQ = 1024
ATT_TK = 2048
ATT_G = 2


def _attn_kernel(q_ref, k_ref, v_ref, o_ref):
    qi = pl.program_id(1)
    slots = [slice(g * SLOT, (g + 1) * SLOT) for g in range(ATT_G)]
    ratio = ATT_TK // ATT_TQ

    def update(off, tk, state, mask):
        out = []
        for sl, (m, acc) in zip(slots, state):
            s = _dot_nt(q_ref[:, sl], k_ref[pl.ds(off, tk), sl])
            if mask is not None:
                s = jnp.where(mask, s, NEG)
            m_new = jnp.maximum(m, jnp.max(s, axis=-1, keepdims=True))
            p = jnp.exp2(s - m_new)
            pv = _dot(p.astype(BF16), v_ref[pl.ds(off, tk), sl])
            out.append((m_new, acc * jnp.exp2(m - m_new) + pv))
        return tuple(out)

    def body(j, state):
        return update(pl.multiple_of(j * ATT_TK, ATT_TK), ATT_TK, state, None)

    init = tuple((jnp.full((ATT_TQ, 1), NEG, F32), jnp.zeros((ATT_TQ, SLOT), F32)) for _ in range(ATT_G))
    state = lax.fori_loop(0, qi // ratio, body, init)

    def tail(rem):
        tk = (rem + 1) * ATT_TQ
        rc = lax.broadcasted_iota(jnp.int32, (ATT_TQ, tk), 0) // CHUNK
        cc = lax.broadcasted_iota(jnp.int32, (ATT_TQ, tk), 1) // CHUNK - rem * (ATT_TQ // CHUNK)
        return lambda st: update(pl.multiple_of((qi - rem) * ATT_TQ, ATT_TQ), tk, st, cc <= rc)

    state = lax.switch(qi % ratio, [tail(rem) for rem in range(ratio)], state)
    lane = lax.broadcasted_iota(jnp.int32, (ATT_TQ, SLOT), 1)
    for g in range(ATT_G):
        acc = state[g][1]
        l = jnp.sum(jnp.where(lane == V_DIM, acc, 0.0), axis=-1, keepdims=True)
        o_ref[:, g * SLOT:(g + 1) * SLOT] = (acc / l).astype(BF16)


def _prompt_attention(q, k, v, n_prompt):
    n = q.shape[0]
    wide = ATT_G * SLOT
    assert ATT_TK % ATT_TQ == 0 and ATT_TQ % CHUNK == 0 and n_prompt % ATT_TQ == 0 and N_HEADS % ATT_G == 0
    return pl.pallas_call(
        _attn_kernel,
        grid=(N_HEADS // ATT_G, n_prompt // ATT_TQ),
        in_specs=[pl.BlockSpec((ATT_TQ, wide), lambda h, i: (i, h)),
                  pl.BlockSpec((n_prompt, wide), lambda h, i: (0, h)),
                  pl.BlockSpec((n_prompt, wide), lambda h, i: (0, h))],
        out_specs=pl.BlockSpec((ATT_TQ, wide), lambda h, i: (i, h)),
        out_shape=jax.ShapeDtypeStruct((n_prompt, N_HEADS * SLOT), BF16),
        compiler_params=_cparams(("parallel", "arbitrary")),
        name="prompt_attention",
    )(q, k, v)


def _sample_attn_kernel(q_ref, ckvc_ref, kpec_ref, ckvn_ref, kpen_ref, wabs_ref, wuv_ref, o_ref):
    q = q_ref[...]
    seq = q.shape[0]
    qa, qp = [], []
    for h in range(N_HEADS):
        qh = q[:, h * SLOT:(h + 1) * SLOT]
        qa.append(_dot(qh, wabs_ref[h]))
        qp.append(qh[:, QK_NOPE:QK_NOPE + QK_ROPE])
    qa = jnp.concatenate(qa, axis=0).astype(BF16)
    qp = jnp.concatenate(qp, axis=0)
    ckvc = ckvc_ref[0].astype(BF16)
    kpec = kpec_ref[0].astype(BF16)
    ckvn = ckvn_ref[...].astype(BF16)
    kpen = kpen_ref[...][:, QK_NOPE:QK_NOPE + QK_ROPE].astype(BF16)
    s_c = _dot_nt(qa, ckvc) + _dot_nt(qp, kpec)
    s_n = _dot_nt(qa, ckvn) + _dot_nt(qp, kpen)
    m = jnp.maximum(jnp.max(s_c, axis=-1, keepdims=True), jnp.max(s_n, axis=-1, keepdims=True))
    p_c = jnp.exp2(s_c - m)
    p_n = jnp.exp2(s_n - m)
    l = jnp.sum(p_c, axis=-1, keepdims=True) + jnp.sum(p_n, axis=-1, keepdims=True)
    ol = (_dot(p_c.astype(BF16), ckvc) + _dot(p_n.astype(BF16), ckvn)) / l
    olb = ol.astype(BF16)
    for h in range(N_HEADS):
        o_ref[:, h * SLOT:(h + 1) * SLOT] = _dot(olb[h * seq:(h + 1) * seq], wuv_ref[h]).astype(BF16)


def _sample_attention(q, cache_ckv, cache_kpe, ckv_new, kpe_new, wabs, wuv, n_prompt):
    nb, past, _ = cache_ckv.shape
    seq = (q.shape[0] - n_prompt) // nb
    base = n_prompt // seq
    wide = N_HEADS * SLOT
    return pl.pallas_call(
        _sample_attn_kernel,
        grid=(nb,),
        in_specs=[pl.BlockSpec((seq, wide), lambda b: (base + b, 0)),
                  pl.BlockSpec((1, past, KV_LORA), lambda b: (b, 0, 0)),
                  pl.BlockSpec((1, past, QK_ROPE), lambda b: (b, 0, 0)),
                  pl.BlockSpec((seq, KV_LORA), lambda b: (base + b, 0)),
                  pl.BlockSpec((seq, SLOT), lambda b: (base + b, 0)),
                  _full(wabs.shape), _full(wuv.shape)],
        out_specs=pl.BlockSpec((seq, wide), lambda b: (b, 0)),
        out_shape=jax.ShapeDtypeStruct((nb * seq, wide), BF16),
        compiler_params=_cparams(("parallel",)),
        name="sample_attention",
    )(q, cache_ckv, cache_kpe, ckv_new, kpe_new, wabs, wuv)


ROUTE_E1, ROUTE_E2, ROUTE_W1, ROUTE_W2, ROUTE_R1, ROUTE_R2 = range(6)


def _lane_col(tile, k):
    lane = lax.broadcasted_iota(jnp.int32, tile.shape, 1)
    return jnp.sum(jnp.where(lane == k, tile, 0.0), axis=-1, keepdims=True)


def _route(lg, base):
    lane = lax.broadcasted_iota(jnp.int32, lg.shape, 1)
    lanef = lane.astype(F32)
    big = float(ROUTER_LANES)
    is_g = lane < MOE_GROUPS
    gl = jnp.where(is_g, lg, NEG)
    gmax = jnp.max(gl, axis=-1, keepdims=True)
    gsel = jnp.min(jnp.where(gl == gmax, lanef, big), axis=-1, keepdims=True)
    gate_g = 1.0 / jnp.sum(jnp.where(is_g, jnp.exp(gl - gmax), 0.0), axis=-1, keepdims=True)
    lo = EXPERT_LANE0 + MOE_EPG * gsel
    el = jnp.where(jnp.logical_and(lanef >= lo, lanef < lo + MOE_EPG), lg, NEG)
    m1 = jnp.max(el, axis=-1, keepdims=True)
    i1 = jnp.min(jnp.where(el == m1, lanef, big), axis=-1, keepdims=True)
    el2 = jnp.where(lanef == i1, NEG, el)
    m2 = jnp.max(el2, axis=-1, keepdims=True)
    i2 = jnp.min(jnp.where(el2 == m2, lanef, big), axis=-1, keepdims=True)
    r = jnp.exp(m2 - m1)
    w1 = 1.0 / (1.0 + r)
    w2 = r / (1.0 + r)
    hit1 = lanef == i1
    hit2 = lanef == i2
    onehot = jnp.where(hit1, 1.0, 0.0) + jnp.where(hit2, 1.0, 0.0)
    t = lg.shape[0]
    earlier = lax.broadcasted_iota(jnp.int32, (t, t), 1) < lax.broadcasted_iota(jnp.int32, (t, t), 0)
    ahead = _dot(jnp.where(earlier, 1.0, 0.0).astype(BF16), onehot.astype(BF16)) + base
    r1 = jnp.sum(jnp.where(hit1, ahead, 0.0), axis=-1, keepdims=True)
    r2 = jnp.sum(jnp.where(hit2, ahead, 0.0), axis=-1, keepdims=True)
    cols = {ROUTE_E1: i1 - EXPERT_LANE0, ROUTE_E2: i2 - EXPERT_LANE0, ROUTE_W1: gate_g * w1,
            ROUTE_W2: gate_g * w2, ROUTE_R1: r1, ROUTE_R2: r2}
    route = jnp.zeros(lg.shape, F32)
    for k, c in cols.items():
        route = jnp.where(lane == k, c, route)
    return route, base + jnp.sum(onehot, axis=0, keepdims=True)


def _post_mix(t, x, g, b, wr_hi, wr_lo, br, h_ref, route_ref, cnt_ref, base_ref):
    @pl.when(pl.program_id(0) == 0)
    def _():
        base_ref[...] = jnp.zeros(base_ref.shape, F32)

    h = _layernorm(ALPHA * x + t, g, b)
    h_ref[...] = h
    hb = h.astype(BF16)
    h_lo = (h - hb.astype(F32)).astype(BF16)
    lg = _dot(hb, wr_hi) + (_dot(hb, wr_lo) + _dot(h_lo, wr_hi)) + br
    route, base = _route(lg, base_ref[...])
    route_ref[...] = route
    base_ref[...] = base
    cnt_ref[...] = jnp.broadcast_to(base, cnt_ref.shape)


def _attn_out_kernel(op_ref, os_ref, x_ref, wo_ref, g_ref, b_ref, wrh_ref, wrl_ref, br_ref, h_ref, route_ref,
                     cnt_ref, base_ref, *, prompt_tiles):
    o = jnp.where(pl.program_id(0) < prompt_tiles, op_ref[...], os_ref[...])
    t = _dot(o, wo_ref[...])
    _post_mix(t, x_ref[...], g_ref[...], b_ref[...], wrh_ref[...], wrl_ref[...], br_ref[...],
              h_ref, route_ref, cnt_ref, base_ref)


CNT_ROWS = 8


def _post_out(n, d):
    specs = [_rows(d), _rows(ROUTER_LANES), _full((CNT_ROWS, ROUTER_LANES))]
    shapes = [jax.ShapeDtypeStruct((n, d), F32), jax.ShapeDtypeStruct((n, ROUTER_LANES), F32),
              jax.ShapeDtypeStruct((CNT_ROWS, ROUTER_LANES), F32)]
    scratch = [pltpu.VMEM((1, ROUTER_LANES), F32)]
    return specs, shapes, scratch


def _attn_out(o_prompt, o_sample, x, wo, lw):
    n, d = x.shape
    wide = o_prompt.shape[1]
    pt = o_prompt.shape[0] // ROW_TILE
    specs, shapes, scratch = _post_out(n, d)
    return pl.pallas_call(
        functools.partial(_attn_out_kernel, prompt_tiles=pt),
        grid=(n // ROW_TILE,),
        in_specs=[pl.BlockSpec((ROW_TILE, wide), lambda i: (jnp.minimum(i, pt - 1), 0)),
                  pl.BlockSpec((ROW_TILE, wide), lambda i: (jnp.maximum(i - pt, 0), 0)),
                  _rows(d), _full(wo.shape), _full((1, d)), _full((1, d)),
                  _full(lw["wr_hi"].shape), _full(lw["wr_lo"].shape), _full((1, ROUTER_LANES))],
        out_specs=specs, out_shape=shapes, scratch_shapes=scratch,
        compiler_params=_cparams(("arbitrary",)),
        name="attn_out_ln_router",
    )(o_prompt, o_sample, x, wo, lw["ln1_g"], lw["ln1_b"], lw["wr_hi"], lw["wr_lo"], lw["br"])


MOE_TM = 256
ROW_BLOCK = 512
ROW_UNROLL = 8


def _dispatch_plan(route, counts):
    n = route.shape[0]
    nt_max = pl.cdiv(2 * n, MOE_TM) + N_EXPERTS
    cnt = counts[0, EXPERT_LANE0:EXPERT_LANE0 + N_EXPERTS].astype(jnp.int32)
    ntile = (cnt + MOE_TM - 1) // MOE_TM
    tile_end = jnp.cumsum(ntile)
    row_off = (tile_end - ntile) * MOE_TM
    e = route[:, (ROUTE_E1, ROUTE_E2)].astype(jnp.int32)
    rank = route[:, (ROUTE_R1, ROUTE_R2)].astype(jnp.int32)
    dest = (row_off[e] + rank).T.reshape(2 * n)
    tile_expert = jnp.searchsorted(tile_end, jnp.arange(nt_max, dtype=jnp.int32), side="right")
    tile_expert = jnp.minimum(tile_expert, N_EXPERTS - 1).astype(jnp.int32)
    return dest, tile_expert, tile_end[-1:].astype(jnp.int32), nt_max


def _row_permute_kernel(sidx_ref, didx_ref, src_ref, *rest):
    out_ref, sem = rest[-2:]
    step = pl.program_id(0)

    def row_copy(s, d):
        return pltpu.make_async_copy(src_ref.at[pl.ds(s, 1)], out_ref.at[pl.ds(d, 1)], sem)

    def issue(i, c):
        for k in range(ROW_UNROLL):
            r = i * ROW_UNROLL + k
            row_copy(sidx_ref[0, 0, r], didx_ref[0, 0, r]).start()
        return c

    def wait_block():
        def wait(i, c):
            for _ in range(ROW_UNROLL):
                row_copy(0, 0).wait()
            return c

        lax.fori_loop(0, ROW_BLOCK // ROW_UNROLL, wait, 0)

    lax.fori_loop(0, ROW_BLOCK // ROW_UNROLL, issue, 0)
    pl.when(step > 0)(wait_block)
    pl.when(step == pl.num_programs(0) - 1)(wait_block)


def _row_permute(src, sidx, didx, rows=None, init=None):
    d = src.shape[1]
    m = sidx.shape[0]
    assert sidx.shape == didx.shape and m % ROW_BLOCK == 0
    any_spec = pl.BlockSpec(memory_space=pl.ANY)
    idx_spec = pl.BlockSpec((1, 1, ROW_BLOCK), lambda i: (i, 0, 0), memory_space=pltpu.SMEM)
    args = (sidx.reshape(m // ROW_BLOCK, 1, ROW_BLOCK), didx.reshape(m // ROW_BLOCK, 1, ROW_BLOCK), src)
    args += () if init is None else (init,)
    out_rows = rows if init is None else init.shape[0]
    return pl.pallas_call(
        _row_permute_kernel,
        grid=(m // ROW_BLOCK,),
        in_specs=[idx_spec, idx_spec] + [any_spec] * (len(args) - 2),
        out_specs=any_spec,
        out_shape=jax.ShapeDtypeStruct((out_rows, d), src.dtype),
        scratch_shapes=[pltpu.SemaphoreType.DMA(())],
        input_output_aliases={} if init is None else {3: 0},
        compiler_params=_cparams(("arbitrary",)),
        name="row_permute",
    )(*args)


def _moe_mlp_kernel(texp_ref, nused_ref, x_ref, wg_ref, wu_ref, wd_ref, y_ref):
    del texp_ref

    @pl.when(pl.program_id(0) < nused_ref[0])
    def _():
        x = x_ref[...].astype(BF16)
        g = _dot(x, wg_ref[0].astype(BF16))
        u = _dot(x, wu_ref[0].astype(BF16))
        hdn = (g * jax.nn.sigmoid(g)) * u
        y_ref[...] = _dot(hdn.astype(BF16), wd_ref[0].astype(BF16))


def _moe_mlp(tile_expert, n_used, xs, wg, wu, wd):
    rows, d = xs.shape
    ff = wg.shape[-1]

    def tile(t, texp, nused):
        return jnp.minimum(t, nused[0] - 1)

    x_spec = pl.BlockSpec((MOE_TM, d), lambda t, texp, nused: (tile(t, texp, nused), 0))
    w_in = pl.BlockSpec((1, d, ff), lambda t, texp, nused: (texp[tile(t, texp, nused)], 0, 0))
    w_out = pl.BlockSpec((1, ff, d), lambda t, texp, nused: (texp[tile(t, texp, nused)], 0, 0))
    return pl.pallas_call(
        _moe_mlp_kernel,
        grid_spec=pltpu.PrefetchScalarGridSpec(
            num_scalar_prefetch=2, grid=(rows // MOE_TM,), in_specs=[x_spec, w_in, w_in, w_out], out_specs=x_spec),
        out_shape=jax.ShapeDtypeStruct((rows, d), F32),
        compiler_params=_cparams(("arbitrary",)),
        name="moe_mlp",
    )(tile_expert, n_used, xs, wg, wu, wd)


def _ln2_ple_kernel(h_ref, y1_ref, y2_ref, route_ref, p_ref, g_ref, b_ref, wp_ref, wpg_ref, x_ref):
    route = route_ref[...]
    moe = _lane_col(route, ROUTE_W1) * y1_ref[...] + _lane_col(route, ROUTE_W2) * y2_ref[...]
    h2 = _layernorm(ALPHA * h_ref[...] + moe, g_ref[...], b_ref[...])
    proj = _dot(p_ref[...].astype(BF16), wp_ref[...])
    gate = jax.nn.sigmoid(_dot(h2.astype(BF16), wpg_ref[...]))
    x_ref[...] = h2 + proj * gate


def _ln2_ple(h, y12, route, p, lw):
    n, d = h.shape
    second = n // ROW_TILE
    return pl.pallas_call(
        _ln2_ple_kernel,
        grid=(n // ROW_TILE,),
        in_specs=[_rows(d), _rows(d), pl.BlockSpec((ROW_TILE, d), lambda i: (i + second, 0)), _rows(ROUTER_LANES),
                  _rows(p.shape[1]), _full((1, d)), _full((1, d)), _full(lw["w_proj"].shape),
                  _full(lw["w_pg"].shape)],
        out_specs=_rows(d),
        out_shape=jax.ShapeDtypeStruct((n, d), F32),
        compiler_params=_cparams(("parallel",)),
        name="ln2_ple",
    )(h, y12, y12, route, p, lw["ln2_g"], lw["ln2_b"], lw["w_proj"], lw["w_pg"])


def _routed_ffn(h, route, counts, p, lw):
    n, d = h.shape
    dest, tile_expert, n_used, nt_max = _dispatch_plan(route, counts)
    token = jnp.tile(jnp.arange(n, dtype=jnp.int32), 2)
    xs = _row_permute(h, token, dest, init=jnp.zeros((nt_max * MOE_TM, d), F32))
    ys = _moe_mlp(tile_expert, n_used, xs, lw["wg"], lw["wu"], lw["wd"])
    y12 = _row_permute(ys, dest, jnp.arange(2 * n, dtype=jnp.int32), rows=2 * n)
    return _ln2_ple(h, y12, route, p, lw)


def _ssm_in_kernel(x_ref, w_ref, u_ref):
    u_ref[...] = _dot(x_ref[...].astype(BF16), w_ref[...])


def _ssm_in(x, w):
    n, d = x.shape
    return pl.pallas_call(
        _ssm_in_kernel,
        grid=(n // ROW_TILE,),
        in_specs=[_rows(d), _full(w.shape)],
        out_specs=_rows(w.shape[1]),
        out_shape=jax.ShapeDtypeStruct((n, w.shape[1]), F32),
        compiler_params=_cparams(("parallel",)),
        name="ssm_in",
    )(x, w)


def _ssm_core_kernel(u_ref, m_ref, bp_ref, cp_ref, apow_ref, h0_ref, y_ref, hp_ref, hs_ref, sre_ref, sim_ref,
                     *, n_chunks, levels):
    pad = sre_ref.shape[0] - n_chunks
    u0 = u_ref[0]
    u1 = u_ref[1]
    bb = _dot(u0, bp_ref[0]) + _dot(u1, bp_ref[1])
    re = bb[:n_chunks, :LANES]
    im = bb[:n_chunks, LANES:]
    zeros = jnp.zeros((pad, LANES), F32)
    sre_ref[:pad, :] = zeros
    sim_ref[:pad, :] = zeros

    def shifted(d):
        return sre_ref[pl.ds(pad - d, n_chunks), :], sim_ref[pl.ds(pad - d, n_chunks), :]

    for lv in range(levels):
        d = 1 << lv
        sre_ref[pad:, :] = re
        sim_ref[pad:, :] = im
        pr, pi = shifted(d)
        ar = apow_ref[0, lv:lv + 1, :LANES]
        ai = apow_ref[0, lv:lv + 1, LANES:]
        re, im = re + ar * pr - ai * pi, im + ar * pi + ai * pr
    sre_ref[pad:, :] = re
    sim_ref[pad:, :] = im
    pr, pi = shifted(1)
    h0 = h0_ref[0]
    hprev = jnp.concatenate([jnp.concatenate([pr, pi], axis=1), h0], axis=0).astype(BF16)
    y_ref[0] = _dot(u0, m_ref[0]) + _dot(hprev, cp_ref[0])
    y_ref[1] = _dot(u1, m_ref[1]) + _dot(hprev, cp_ref[1])
    hp_ref[0] = jnp.concatenate([re[n_chunks - 1:, :], im[n_chunks - 1:, :]], axis=1)
    ar = apow_ref[0, 0:1, :LANES]
    ai = apow_ref[0, 0:1, LANES:]
    h0r = h0[:, :LANES]
    h0i = h0[:, LANES:]
    hs_ref[0] = jnp.concatenate([ar * h0r - ai * h0i + bb[n_chunks:, :LANES],
                                 ar * h0i + ai * h0r + bb[n_chunks:, LANES:]], axis=1)


def _ssm_core(ublk, tabs, h0, n_chunks):
    g, rows, width = ublk.shape
    pairs = g // 2
    ns = rows - n_chunks
    levels = max(1, (n_chunks - 1).bit_length())
    pad = 1 << (levels - 1)
    pad = max(pad, 8)
    kern = functools.partial(_ssm_core_kernel, n_chunks=n_chunks, levels=levels)
    pair3 = lambda a, b: pl.BlockSpec((2, a, b), lambda i: (i, 0, 0))
    one3 = lambda a, b: pl.BlockSpec((1, a, b), lambda i: (i, 0, 0))
    return pl.pallas_call(
        kern,
        grid=(pairs,),
        in_specs=[pair3(rows, width), pair3(width, width), pair3(width, width), pair3(width, width),
                  one3(tabs["apow"].shape[1], width), one3(ns, width)],
        out_specs=[pair3(rows, width), one3(1, width), one3(ns, width)],
        out_shape=[jax.ShapeDtypeStruct((g, rows, width), F32), jax.ShapeDtypeStruct((pairs, 1, width), F32),
                   jax.ShapeDtypeStruct((pairs, ns, width), F32)],
        scratch_shapes=[pltpu.VMEM((pad + n_chunks, LANES), F32), pltpu.VMEM((pad + n_chunks, LANES), F32)],
        compiler_params=_cparams(("parallel",)),
        name="ssm_core",
    )(ublk, tabs["mbig"], tabs["bpair"], tabs["cpair"], tabs["apow"], h0)


def _gelu_tanh(y):
    c = math.sqrt(2.0 / math.pi)
    return 0.5 * y * (1.0 + jnp.tanh(c * (y + 0.044715 * (y * y * y))))


def _ssm_out_kernel(y_ref, u_ref, x_ref, d_ref, wglu_ref, g_ref, b_ref, wrh_ref, wrl_ref, br_ref,
                    h_ref, route_ref, cnt_ref, base_ref):
    y = y_ref[...] + d_ref[...] * u_ref[...]
    z = _dot(_gelu_tanh(y).astype(BF16), wglu_ref[...])
    dm = z.shape[1] // 2
    t = z[:, :dm] * jax.nn.sigmoid(z[:, dm:])
    _post_mix(t, x_ref[...], g_ref[...], b_ref[...], wrh_ref[...], wrl_ref[...], br_ref[...],
              h_ref, route_ref, cnt_ref, base_ref)


def _ssm_out(y, u, x, d_skip, wglu, lw):
    n, d = x.shape
    specs, shapes, scratch = _post_out(n, d)
    return pl.pallas_call(
        _ssm_out_kernel,
        grid=(n // ROW_TILE,),
        in_specs=[_rows(d), _rows(d), _rows(d), _full((1, d)), _full(wglu.shape), _full((1, d)), _full((1, d)),
                  _full(lw["wr_hi"].shape), _full(lw["wr_lo"].shape), _full((1, ROUTER_LANES))],
        out_specs=specs, out_shape=shapes, scratch_shapes=scratch,
        compiler_params=_cparams(("arbitrary",)),
        name="ssm_out_ln_router",
    )(y, u, x, d_skip, wglu, lw["ln1_g"], lw["ln1_b"], lw["wr_hi"], lw["wr_lo"], lw["br"])


def _rope_tables(pos):
    half = QK_ROPE // 2
    inv = ROPE_BASE ** (-jnp.arange(half, dtype=F32) / half)
    ang = pos.astype(F32)[:, None] * inv[None, :]
    cos, sin = jnp.cos(ang), jnp.sin(ang)
    n = pos.shape[0]
    ones = jnp.ones((n, QK_NOPE), F32)
    zeros = jnp.zeros((n, SLOT - QK_NOPE - QK_ROPE), F32)
    ct = jnp.concatenate([ones, cos, cos, zeros], axis=1)
    st = jnp.concatenate([jnp.zeros((n, QK_NOPE), F32), -sin, sin, zeros], axis=1)
    return ct, st


def _slot_cols(w, width):
    k = w.shape[0]
    return jnp.pad(w, ((0, 0), (0, 0), (0, SLOT - width))).reshape(k, N_HEADS * SLOT)


def _mla_tables(w_in, q_norm, kv_norm, w_uq, w_uk, w_uv, w_o):
    half = QK_ROPE // 2
    o = Q_LORA + KV_LORA
    d = w_in.shape[0]
    kpe_w = w_in[:, o:]
    kpe_sw = jnp.concatenate([kpe_w[:, half:], kpe_w[:, :half]], axis=1)
    zl = jnp.zeros((d, QK_NOPE), F32)
    zr = jnp.zeros((d, SLOT - QK_NOPE - QK_ROPE), F32)
    w_in_e = jnp.concatenate([w_in[:, :o], zl, kpe_w, zr, zl, kpe_sw, zr], axis=1)
    wq = w_uq.reshape(Q_LORA, N_HEADS, QK_NOPE + QK_ROPE)
    pe = wq[:, :, QK_NOPE:]
    pe_sw = jnp.concatenate([pe[:, :, half:], pe[:, :, :half]], axis=2)
    wqb = jnp.concatenate([jnp.zeros_like(wq[:, :, :QK_NOPE]), pe_sw], axis=2)
    vone = jnp.zeros((N_HEADS, SLOT), F32).at[:, V_DIM].set(1.0).reshape(1, N_HEADS * SLOT)
    wabs = jnp.pad(jnp.transpose(w_uk, (1, 2, 0)), ((0, 0), (0, SLOT - QK_NOPE), (0, 0)))
    wuv = jnp.pad(jnp.transpose(w_uv, (1, 0, 2)), ((0, 0), (0, 0), (0, SLOT - V_DIM)))
    wo = jnp.pad(w_o.reshape(N_HEADS, V_DIM, -1), ((0, 0), (0, SLOT - V_DIM), (0, 0)))
    return dict(
        w_in=w_in_e.astype(BF16), qn=q_norm.reshape(1, -1), kvn=kv_norm.reshape(1, -1),
        wqa=_slot_cols(wq, QK_NOPE + QK_ROPE).astype(BF16), wqb=_slot_cols(wqb, QK_NOPE + QK_ROPE).astype(BF16),
        wk=_slot_cols(w_uk, QK_NOPE).astype(BF16), wv=_slot_cols(w_uv, V_DIM).astype(BF16), vone=vone,
        wabs=wabs.astype(BF16), wuv=wuv.astype(BF16), wo=wo.reshape(N_HEADS * SLOT, -1).astype(BF16))


def _cmul(a, b):
    return a[0] * b[0] - a[1] * b[1], a[0] * b[1] + a[1] * b[0]


def _ssm_tables(a_re, a_im, log_dt, b_re, b_im, c_re, c_im, levels):
    t = SSM_T
    g, p = a_re.shape
    c = SSM_GROUP_CH
    hi = lax.Precision.HIGHEST
    dt = jnp.exp(log_dt)[:, None]
    mag = jnp.exp(a_re * dt)
    lam_bar = (mag * jnp.cos(a_im * dt), mag * jnp.sin(a_im * dt))
    den = a_re * a_re + a_im * a_im
    quo = (((lam_bar[0] - 1.0) * a_re + lam_bar[1] * a_im) / den,
           (lam_bar[1] * a_re - (lam_bar[0] - 1.0) * a_im) / den)
    b_bar = _cmul((quo[0][:, :, None], quo[1][:, :, None]), (b_re, b_im))
    pw = [(jnp.ones_like(a_re), jnp.zeros_like(a_re))]
    for _ in range(t):
        pw.append(_cmul(pw[-1], lam_bar))
    pw_r = jnp.stack([x[0] for x in pw])
    pw_i = jnp.stack([x[1] for x in pw])
    cp = _cmul((c_re[None], c_im[None]), (pw_r[:t, :, None, :], pw_i[:t, :, None, :]))
    taps = (jnp.einsum("tgop,gpi->tgoi", cp[0], b_bar[0], precision=hi)
            - jnp.einsum("tgop,gpi->tgoi", cp[1], b_bar[1], precision=hi))
    lag = jnp.arange(t)[None, :] - jnp.arange(t)[:, None]
    blocks = jnp.where((lag >= 0)[:, :, None, None, None], taps[jnp.clip(lag, 0, t - 1)], 0.0)
    mbig = jnp.transpose(blocks, (2, 0, 4, 1, 3)).reshape(g, t * c, t * c)
    rev = t - 1 - jnp.arange(t)
    bfl = _cmul((pw_r[rev][:, :, :, None], pw_i[rev][:, :, :, None]), (b_bar[0][None], b_bar[1][None]))
    bfl = [jnp.transpose(x, (1, 0, 3, 2)).reshape(g, t * c, p) for x in bfl]
    par = (jnp.arange(g) % 2)[:, None, None]
    z = jnp.zeros_like(bfl[0])

    def pair_cols(v):
        return jnp.where(par == 0, jnp.concatenate([v, z], axis=2), jnp.concatenate([z, v], axis=2))

    bpair = jnp.concatenate([pair_cols(bfl[0]), pair_cols(bfl[1])], axis=2)
    cfl = _cmul((c_re[None], c_im[None]), (pw_r[1:t + 1, :, None, :], pw_i[1:t + 1, :, None, :]))
    cfl = [jnp.transpose(x, (1, 3, 0, 2)).reshape(g, p, t * c) for x in cfl]
    zc = jnp.zeros_like(cfl[0])

    def pair_rows(v):
        return jnp.where(par == 0, jnp.concatenate([v, zc], axis=1), jnp.concatenate([zc, v], axis=1))

    cpair = jnp.concatenate([pair_rows(cfl[0]), pair_rows(-cfl[1])], axis=1)
    ap = [pw[t]]
    for _ in range(levels - 1):
        ap.append(_cmul(ap[-1], ap[-1]))

    def pair_lanes(xs):
        v = jnp.stack(xs, axis=1).reshape(g // 2, 2, levels, p)
        return jnp.transpose(v, (0, 2, 1, 3)).reshape(g // 2, levels, 2 * p)

    apow = jnp.concatenate([pair_lanes([x[0] for x in ap]), pair_lanes([x[1] for x in ap])], axis=2)
    rows = -(-levels // 8) * 8
    apow = jnp.pad(apow, ((0, 0), (0, rows - levels), (0, 0)))
    return dict(mbig=mbig.astype(BF16), bpair=bpair.astype(BF16), cpair=cpair.astype(BF16), apow=apow)


def _pair_states(re, im):
    b, g, p = re.shape
    r = jnp.transpose(re.reshape(b, g // 2, 2 * p), (1, 0, 2))
    i = jnp.transpose(im.reshape(b, g // 2, 2 * p), (1, 0, 2))
    return jnp.concatenate([r, i], axis=2)


def _unpair_states(h):
    pairs, b, w = h.shape
    p = w // 4
    r = jnp.transpose(h[:, :, :2 * p], (1, 0, 2)).reshape(b, pairs * 2, p)
    i = jnp.transpose(h[:, :, 2 * p:], (1, 0, 2)).reshape(b, pairs * 2, p)
    return r, i


def _layer_tables(i, ln1_g, ln1_b, ln2_g, ln2_b, w_rg, b_rg, w_re, b_re, w_gate, w_up, w_down, w_proj, w_pg):
    d = w_rg.shape[1]
    wr = jnp.concatenate([w_rg[i], jnp.transpose(w_re[i], (1, 0, 2)).reshape(d, N_EXPERTS)], axis=1)
    wr = jnp.pad(wr, ((0, 0), (0, ROUTER_LANES - wr.shape[1])))
    wr_hi = wr.astype(BF16)
    wr_lo = (wr - wr_hi.astype(F32)).astype(BF16)
    br = jnp.pad(jnp.concatenate([b_rg[i], b_re[i].reshape(-1)]), (0, ROUTER_LANES - MOE_GROUPS - N_EXPERTS))
    ff = w_gate.shape[-1]
    return dict(
        ln1_g=ln1_g[i].reshape(1, d), ln1_b=ln1_b[i].reshape(1, d), ln2_g=ln2_g[i].reshape(1, d),
        ln2_b=ln2_b[i].reshape(1, d), wr_hi=wr_hi, wr_lo=wr_lo, br=br.reshape(1, ROUTER_LANES),
        wg=w_gate[i].reshape(N_EXPERTS, d, ff), wu=w_up[i].reshape(N_EXPERTS, d, ff),
        wd=w_down[i].reshape(N_EXPERTS, ff, d),
        w_proj=w_proj[i].astype(BF16), w_pg=w_pg[i].astype(BF16))


def kernel(x_prompt, x_sample, p_prompt, p_sample, cache_mla_ckv, cache_mla_kpe, state_ssm_re, state_ssm_im, mla_w_in, mla_q_norm, mla_kv_norm, mla_w_uq, mla_w_uk, mla_w_uv, mla_w_o, ssm_w_in, ssm_a_re, ssm_a_im, ssm_log_dt, ssm_b_re, ssm_b_im, ssm_c_re, ssm_c_im, ssm_d, ssm_w_glu, ln1_g, ln1_b, ln2_g, ln2_b, moe_w_rg, moe_b_rg, moe_w_re, moe_b_re, moe_w_gate, moe_w_up, moe_w_down, ple_w_proj, ple_w_gate):
    bp, n_prompt, d = x_prompt.shape
    nb, seq, _ = x_sample.shape
    past = cache_mla_ckv.shape[2]
    assert bp == 1 and seq == SSM_T and n_prompt % ROW_TILE == 0 and (nb * seq) % ROW_TILE == 0
    assert n_prompt % CHUNK == 0 and past % CHUNK == 0 and seq <= CHUNK
    n_samp = nb * seq
    n = n_prompt + n_samp
    x = jnp.concatenate([x_prompt.reshape(n_prompt, d), x_sample.reshape(n_samp, d)], axis=0)
    p_all = jnp.concatenate([p_prompt.reshape(DEPTH, n_prompt, -1), p_sample.reshape(DEPTH, n_samp, -1)], axis=1)
    layer_args = (ln1_g, ln1_b, ln2_g, ln2_b, moe_w_rg, moe_b_rg, moe_w_re, moe_b_re, moe_w_gate, moe_w_up,
                  moe_w_down, ple_w_proj, ple_w_gate)
    lw = _layer_tables(0, *layer_args)
    mw = _mla_tables(mla_w_in[0], mla_q_norm[0], mla_kv_norm[0], mla_w_uq[0], mla_w_uk[0], mla_w_uv[0], mla_w_o[0])
    pos = jnp.concatenate([jnp.arange(n_prompt, dtype=jnp.int32),
                           past + jnp.tile(jnp.arange(seq, dtype=jnp.int32), nb)])
    ct, st = _rope_tables(pos)
    q, k, v, ckv, kpe_slot = _mla_proj(x, ct, st, mw)
    o_prompt = _prompt_attention(q, k, v, n_prompt)
    o_sample = _sample_attention(q, cache_mla_ckv[0], cache_mla_kpe[0], ckv, kpe_slot, mw["wabs"], mw["wuv"],
                                 n_prompt)
    h, route, counts = _attn_out(o_prompt, o_sample, x, mw["wo"], lw)
    x = _routed_ffn(h, route, counts, p_all[0], lw)
    kpe = kpe_slot[:, QK_NOPE:QK_NOPE + QK_ROPE]

    lw = _layer_tables(1, *layer_args)
    n_chunks = n_prompt // SSM_T
    rows = n // SSM_T
    groups = d // SSM_GROUP_CH
    levels = max(1, (n_chunks - 1).bit_length())
    tabs = _ssm_tables(ssm_a_re[0], ssm_a_im[0], ssm_log_dt[0], ssm_b_re[0], ssm_b_im[0], ssm_c_re[0],
                       ssm_c_im[0], levels)
    u = _ssm_in(x, ssm_w_in[0].astype(BF16))
    ublk = jnp.transpose(u.reshape(rows, SSM_T, groups, SSM_GROUP_CH), (2, 0, 1, 3))
    ublk = ublk.reshape(groups, rows, SSM_T * SSM_GROUP_CH).astype(BF16)
    h0 = _pair_states(state_ssm_re[0], state_ssm_im[0])
    yblk, hp, hs = _ssm_core(ublk, tabs, h0, n_chunks)
    y = jnp.transpose(yblk.reshape(groups, rows, SSM_T, SSM_GROUP_CH), (1, 2, 0, 3)).reshape(n, d)
    h, route, counts = _ssm_out(y, u, x, ssm_d[0].reshape(1, d), ssm_w_glu[0].astype(BF16), lw)
    x = _routed_ffn(h, route, counts, p_all[1], lw)
    re_p, im_p = _unpair_states(hp)
    re_s, im_s = _unpair_states(hs)

    return (x[:n_prompt].reshape(1, n_prompt, d), x[n_prompt:].reshape(nb, seq, d),
            ckv[:n_prompt].reshape(1, 1, n_prompt, KV_LORA), kpe[:n_prompt].reshape(1, 1, n_prompt, QK_ROPE),
            re_p[None], im_p[None],
            ckv[n_prompt:].reshape(1, nb, seq, KV_LORA), kpe[n_prompt:].reshape(1, nb, seq, QK_ROPE),
            re_s[None], im_s[None])
```

```python
import functools
import math

import jax
import jax.numpy as jnp
from jax import lax
from jax.experimental import pallas as pl
from jax.experimental.pallas import tpu as pltpu

F32 = jnp.float32
BF16 = jnp.bfloat16

N_HEADS = 16
Q_LORA = 256
KV_LORA = 128
QK_NOPE = 64
QK_ROPE = 32
V_DIM = 64
ROPE_BASE = 10000.0
ATTN_SCALE = (QK_NOPE + QK_ROPE) ** -0.5
CHUNK = 64
SSM_GROUP_CH = 16
SSM_STATE = 64
MOE_GROUPS = 4
MOE_EPG = 8
N_EXPERTS = MOE_GROUPS * MOE_EPG
DEPTH = 2
ALPHA = (2.0 * DEPTH) ** 0.25
LN_EPS = 1e-5
RMS_EPS = 1e-6
NEG = -1e30

LANES = 128
SLOT = LANES
ROW_TILE = 256
SSM_T = 16
VMEM_LIMIT = 56 * 1024 * 1024
ROUTER_LANES = LANES
EXPERT_LANE0 = MOE_GROUPS


def _cparams(sem):
    return pltpu.CompilerParams(dimension_semantics=sem, vmem_limit_bytes=VMEM_LIMIT)


def _dot(a, b):
    return jnp.dot(a, b, preferred_element_type=F32)


def _dot_nt(a, b):
    return lax.dot_general(a, b, (((1,), (1,)), ((), ())), preferred_element_type=F32)


def _full(shape):
    nd = len(shape)
    return pl.BlockSpec(shape, lambda *_: (0,) * nd)


def _rows(width, rows=ROW_TILE):
    return pl.BlockSpec((rows, width), lambda i: (i, 0))


def _layernorm(t, g, b):
    mu = jnp.mean(t, axis=-1, keepdims=True)
    d = t - mu
    var = jnp.mean(d * d, axis=-1, keepdims=True)
    return d * lax.rsqrt(var + LN_EPS) * g + b


def _rmsnorm(t, g):
    return t * lax.rsqrt(jnp.mean(t * t, axis=-1, keepdims=True) + RMS_EPS) * g


def _mla_proj_kernel(x_ref, ct_ref, st_ref, w_in_ref, qn_ref, kvn_ref, wqa_ref, wqb_ref, wk_ref, wv_ref,
                     vone_ref, q_ref, k_ref, v_ref, ckv_ref, kpe_ref):
    x = x_ref[...].astype(BF16)
    z = _dot(x, w_in_ref[...])
    cq = _rmsnorm(z[:, :Q_LORA], qn_ref[...])
    ckv = _rmsnorm(z[:, Q_LORA:Q_LORA + KV_LORA], kvn_ref[...])
    ct = ct_ref[...]
    st = st_ref[...]
    o = Q_LORA + KV_LORA
    kpe = z[:, o:o + SLOT] * ct + z[:, o + SLOT:o + 2 * SLOT] * st
    ckv_ref[...] = ckv
    kpe_ref[...] = kpe
    cqb = cq.astype(BF16)
    ckvb = ckv.astype(BF16)
    qa = _dot(cqb, wqa_ref[...])
    qb = _dot(cqb, wqb_ref[...])
    kn = _dot(ckvb, wk_ref[...])
    vv = _dot(ckvb, wv_ref[...]) + vone_ref[...]
    scale = ATTN_SCALE * math.log2(math.e)
    for h in range(N_HEADS):
        sl = slice(h * SLOT, (h + 1) * SLOT)
        q_ref[:, sl] = ((qa[:, sl] * ct + qb[:, sl] * st) * scale).astype(BF16)
        k_ref[:, sl] = (kn[:, sl] + kpe).astype(BF16)
    v_ref[...] = vv.astype(BF16)


def _mla_proj(x, ct, st, w):
    n = x.shape[0]
    wide = N_HEADS * SLOT
    return pl.pallas_call(
        _mla_proj_kernel,
        grid=(n // ROW_TILE,),
        in_specs=[_rows(x.shape[1]), _rows(SLOT), _rows(SLOT), _full(w["w_in"].shape), _full((1, Q_LORA)),
                  _full((1, KV_LORA)), _full(w["wqa"].shape), _full(w["wqb"].shape), _full(w["wk"].shape),
                  _full(w["wv"].shape), _full((1, wide))],
        out_specs=[_rows(wide), _rows(wide), _rows(wide), _rows(KV_LORA), _rows(SLOT)],
        out_shape=[jax.ShapeDtypeStruct((n, wide), BF16)] * 3
        + [jax.ShapeDtypeStruct((n, KV_LORA), F32), jax.ShapeDtypeStruct((n, SLOT), F32)],
        compiler_params=_cparams(("parallel",)),
        name="mla_proj",
    )(x, ct, st, w["w_in"], w["qn"], w["kvn"], w["wqa"], w["wqb"], w["wk"], w["wv"], w["vone"])


ATT_TQ = 1024
ATT_TK = 2048
ATT_G = 2


def _attn_kernel(q_ref, k_ref, v_ref, o_ref):
    qi = pl.program_id(1)
    slots = [slice(g * SLOT, (g + 1) * SLOT) for g in range(ATT_G)]
    ratio = ATT_TK // ATT_TQ

    def update(off, tk, state, mask):
        out = []
        for sl, (m, acc) in zip(slots, state):
            s = _dot_nt(q_ref[:, sl], k_ref[pl.ds(off, tk), sl])
            if mask is not None:
                s = jnp.where(mask, s, NEG)
            m_new = jnp.maximum(m, jnp.max(s, axis=-1, keepdims=True))
            p = jnp.exp2(s - m_new)
            pv = _dot(p.astype(BF16), v_ref[pl.ds(off, tk), sl])
            out.append((m_new, acc * jnp.exp2(m - m_new) + pv))
        return tuple(out)

    def body(j, state):
        return update(pl.multiple_of(j * ATT_TK, ATT_TK), ATT_TK, state, None)

    init = tuple((jnp.full((ATT_TQ, 1), NEG, F32), jnp.zeros((ATT_TQ, SLOT), F32)) for _ in range(ATT_G))
    state = lax.fori_loop(0, qi // ratio, body, init)

    def tail(rem):
        tk = (rem + 1) * ATT_TQ
        rc = lax.broadcasted_iota(jnp.int32, (ATT_TQ, tk), 0) // CHUNK
        cc = lax.broadcasted_iota(jnp.int32, (ATT_TQ, tk), 1) // CHUNK - rem * (ATT_TQ // CHUNK)
        return lambda st: update(pl.multiple_of((qi - rem) * ATT_TQ, ATT_TQ), tk, st, cc <= rc)

    state = lax.switch(qi % ratio, [tail(rem) for rem in range(ratio)], state)
    lane = lax.broadcasted_iota(jnp.int32, (ATT_TQ, SLOT), 1)
    for g in range(ATT_G):
        acc = state[g][1]
        l = jnp.sum(jnp.where(lane == V_DIM, acc, 0.0), axis=-1, keepdims=True)
        o_ref[:, g * SLOT:(g + 1) * SLOT] = (acc / l).astype(BF16)


def _prompt_attention(q, k, v, n_prompt):
    n = q.shape[0]
    wide = ATT_G * SLOT
    assert ATT_TK % ATT_TQ == 0 and ATT_TQ % CHUNK == 0 and n_prompt % ATT_TQ == 0 and N_HEADS % ATT_G == 0
    return pl.pallas_call(
        _attn_kernel,
        grid=(N_HEADS // ATT_G, n_prompt // ATT_TQ),
        in_specs=[pl.BlockSpec((ATT_TQ, wide), lambda h, i: (i, h)),
                  pl.BlockSpec((n_prompt, wide), lambda h, i: (0, h)),
                  pl.BlockSpec((n_prompt, wide), lambda h, i: (0, h))],
        out_specs=pl.BlockSpec((ATT_TQ, wide), lambda h, i: (i, h)),
        out_shape=jax.ShapeDtypeStruct((n_prompt, N_HEADS * SLOT), BF16),
        compiler_params=_cparams(("parallel", "arbitrary")),
        name="prompt_attention",
    )(q, k, v)


def _sample_attn_kernel(q_ref, ckvc_ref, kpec_ref, ckvn_ref, kpen_ref, wabs_ref, wuv_ref, o_ref):
    q = q_ref[...]
    seq = q.shape[0]
    qa, qp = [], []
    for h in range(N_HEADS):
        qh = q[:, h * SLOT:(h + 1) * SLOT]
        qa.append(_dot(qh, wabs_ref[h]))
        qp.append(qh[:, QK_NOPE:QK_NOPE + QK_ROPE])
    qa = jnp.concatenate(qa, axis=0).astype(BF16)
    qp = jnp.concatenate(qp, axis=0)
    ckvc = ckvc_ref[0].astype(BF16)
    kpec = kpec_ref[0].astype(BF16)
    ckvn = ckvn_ref[...].astype(BF16)
    kpen = kpen_ref[...][:, QK_NOPE:QK_NOPE + QK_ROPE].astype(BF16)
    s_c = _dot_nt(qa, ckvc) + _dot_nt(qp, kpec)
    s_n = _dot_nt(qa, ckvn) + _dot_nt(qp, kpen)
    m = jnp.maximum(jnp.max(s_c, axis=-1, keepdims=True), jnp.max(s_n, axis=-1, keepdims=True))
    p_c = jnp.exp2(s_c - m)
    p_n = jnp.exp2(s_n - m)
    l = jnp.sum(p_c, axis=-1, keepdims=True) + jnp.sum(p_n, axis=-1, keepdims=True)
    ol = (_dot(p_c.astype(BF16), ckvc) + _dot(p_n.astype(BF16), ckvn)) / l
    olb = ol.astype(BF16)
    for h in range(N_HEADS):
        o_ref[:, h * SLOT:(h + 1) * SLOT] = _dot(olb[h * seq:(h + 1) * seq], wuv_ref[h]).astype(BF16)


def _sample_attention(q, cache_ckv, cache_kpe, ckv_new, kpe_new, wabs, wuv, n_prompt):
    nb, past, _ = cache_ckv.shape
    seq = (q.shape[0] - n_prompt) // nb
    base = n_prompt // seq
    wide = N_HEADS * SLOT
    return pl.pallas_call(
        _sample_attn_kernel,
        grid=(nb,),
        in_specs=[pl.BlockSpec((seq, wide), lambda b: (base + b, 0)),
                  pl.BlockSpec((1, past, KV_LORA), lambda b: (b, 0, 0)),
                  pl.BlockSpec((1, past, QK_ROPE), lambda b: (b, 0, 0)),
                  pl.BlockSpec((seq, KV_LORA), lambda b: (base + b, 0)),
                  pl.BlockSpec((seq, SLOT), lambda b: (base + b, 0)),
                  _full(wabs.shape), _full(wuv.shape)],
        out_specs=pl.BlockSpec((seq, wide), lambda b: (b, 0)),
        out_shape=jax.ShapeDtypeStruct((nb * seq, wide), BF16),
        compiler_params=_cparams(("parallel",)),
        name="sample_attention",
    )(q, cache_ckv, cache_kpe, ckv_new, kpe_new, wabs, wuv)


ROUTE_E1, ROUTE_E2, ROUTE_W1, ROUTE_W2 = range(4)


def _lane_col(tile, k):
    lane = lax.broadcasted_iota(jnp.int32, tile.shape, 1)
    return jnp.sum(jnp.where(lane == k, tile, 0.0), axis=-1, keepdims=True)


def _route(lg):
    lane = lax.broadcasted_iota(jnp.int32, lg.shape, 1)
    lanef = lane.astype(F32)
    big = float(ROUTER_LANES)
    is_g = lane < MOE_GROUPS
    gl = jnp.where(is_g, lg, NEG)
    gmax = jnp.max(gl, axis=-1, keepdims=True)
    gsel = jnp.min(jnp.where(gl == gmax, lanef, big), axis=-1, keepdims=True)
    gate_g = 1.0 / jnp.sum(jnp.where(is_g, jnp.exp(gl - gmax), 0.0), axis=-1, keepdims=True)
    lo = EXPERT_LANE0 + MOE_EPG * gsel
    el = jnp.where(jnp.logical_and(lanef >= lo, lanef < lo + MOE_EPG), lg, NEG)
    m1 = jnp.max(el, axis=-1, keepdims=True)
    i1 = jnp.min(jnp.where(el == m1, lanef, big), axis=-1, keepdims=True)
    el2 = jnp.where(lanef == i1, NEG, el)
    m2 = jnp.max(el2, axis=-1, keepdims=True)
    i2 = jnp.min(jnp.where(el2 == m2, lanef, big), axis=-1, keepdims=True)
    r = jnp.exp(m2 - m1)
    w1 = 1.0 / (1.0 + r)
    w2 = r / (1.0 + r)
    cols = {ROUTE_E1: i1 - EXPERT_LANE0, ROUTE_E2: i2 - EXPERT_LANE0, ROUTE_W1: gate_g * w1, ROUTE_W2: gate_g * w2}
    route = jnp.zeros(lg.shape, F32)
    for k, c in cols.items():
        route = jnp.where(lane == k, c, route)
    return route


def _post_mix(t, x, g, b, wr_hi, wr_lo, br, h_ref, hb_ref, route_ref):
    h = _layernorm(ALPHA * x + t, g, b)
    h_ref[...] = h
    hb = h.astype(BF16)
    hb_ref[...] = hb
    h_lo = (h - hb.astype(F32)).astype(BF16)
    lg = _dot(hb, wr_hi) + (_dot(hb, wr_lo) + _dot(h_lo, wr_hi)) + br
    route_ref[...] = _route(lg)


def _attn_out_kernel(op_ref, os_ref, x_ref, wo_ref, g_ref, b_ref, wrh_ref, wrl_ref, br_ref, h_ref, hb_ref,
                     route_ref, *, prompt_tiles):
    o = jnp.where(pl.program_id(0) < prompt_tiles, op_ref[...], os_ref[...])
    t = _dot(o, wo_ref[...])
    _post_mix(t, x_ref[...], g_ref[...], b_ref[...], wrh_ref[...], wrl_ref[...], br_ref[...],
              h_ref, hb_ref, route_ref)


def _post_out(n, d):
    specs = [_rows(d), _rows(d), _rows(ROUTER_LANES)]
    shapes = [jax.ShapeDtypeStruct((n, d), F32), jax.ShapeDtypeStruct((n, d), BF16),
              jax.ShapeDtypeStruct((n, ROUTER_LANES), F32)]
    return specs, shapes


def _attn_out(o_prompt, o_sample, x, wo, lw):
    n, d = x.shape
    wide = o_prompt.shape[1]
    pt = o_prompt.shape[0] // ROW_TILE
    specs, shapes = _post_out(n, d)
    return pl.pallas_call(
        functools.partial(_attn_out_kernel, prompt_tiles=pt),
        grid=(n // ROW_TILE,),
        in_specs=[pl.BlockSpec((ROW_TILE, wide), lambda i: (jnp.minimum(i, pt - 1), 0)),
                  pl.BlockSpec((ROW_TILE, wide), lambda i: (jnp.maximum(i - pt, 0), 0)),
                  _rows(d), _full(wo.shape), _full((1, d)), _full((1, d)),
                  _full(lw["wr_hi"].shape), _full(lw["wr_lo"].shape), _full((1, ROUTER_LANES))],
        out_specs=specs, out_shape=shapes,
        compiler_params=_cparams(("parallel",)),
        name="attn_out_ln_router",
    )(o_prompt, o_sample, x, wo, lw["ln1_g"], lw["ln1_b"], lw["wr_hi"], lw["wr_lo"], lw["br"])


def _moe_dense_kernel(hb_ref, route_ref, wg_ref, wu_ref, wd_ref, y_ref):
    e = pl.program_id(1).astype(F32)
    x = hb_ref[...]
    g = _dot(x, wg_ref[0])
    u = _dot(x, wu_ref[0])
    route = route_ref[...]
    gate = (jnp.where(_lane_col(route, ROUTE_E1) == e, _lane_col(route, ROUTE_W1), 0.0)
            + jnp.where(_lane_col(route, ROUTE_E2) == e, _lane_col(route, ROUTE_W2), 0.0))
    hdn = (g * jax.nn.sigmoid(g)) * u * gate
    y = _dot(hdn.astype(BF16), wd_ref[0])

    @pl.when(pl.program_id(1) == 0)
    def _():
        y_ref[...] = y

    @pl.when(pl.program_id(1) > 0)
    def _():
        y_ref[...] += y


def _moe_tile(n):
    return max(t for t in range(LANES, 2048 + 1, LANES) if n % t == 0)


def _moe_dense(hb, route, wg, wu, wd):
    n, d = hb.shape
    ff = wg.shape[-1]
    tile = _moe_tile(n)
    return pl.pallas_call(
        _moe_dense_kernel,
        grid=(n // tile, N_EXPERTS),
        in_specs=[pl.BlockSpec((tile, d), lambda i, e: (i, 0)),
                  pl.BlockSpec((tile, ROUTER_LANES), lambda i, e: (i, 0)),
                  pl.BlockSpec((1, d, ff), lambda i, e: (e, 0, 0)),
                  pl.BlockSpec((1, d, ff), lambda i, e: (e, 0, 0)),
                  pl.BlockSpec((1, ff, d), lambda i, e: (e, 0, 0))],
        out_specs=pl.BlockSpec((tile, d), lambda i, e: (i, 0)),
        out_shape=jax.ShapeDtypeStruct((n, d), F32),
        compiler_params=_cparams(("parallel", "arbitrary")),
        name="moe_dense",
    )(hb, route, wg, wu, wd)


def _ln2_ple_kernel(h_ref, y_ref, p_ref, g_ref, b_ref, wp_ref, wpg_ref, x_ref):
    h2 = _layernorm(ALPHA * h_ref[...] + y_ref[...], g_ref[...], b_ref[...])
    proj = _dot(p_ref[...].astype(BF16), wp_ref[...])
    gate = jax.nn.sigmoid(_dot(h2.astype(BF16), wpg_ref[...]))
    x_ref[...] = h2 + proj * gate


def _ln2_ple(h, y, p, lw):
    n, d = h.shape
    return pl.pallas_call(
        _ln2_ple_kernel,
        grid=(n // ROW_TILE,),
        in_specs=[_rows(d), _rows(d), _rows(p.shape[1]), _full((1, d)), _full((1, d)),
                  _full(lw["w_proj"].shape), _full(lw["w_pg"].shape)],
        out_specs=_rows(d),
        out_shape=jax.ShapeDtypeStruct((n, d), F32),
        compiler_params=_cparams(("parallel",)),
        name="ln2_ple",
    )(h, y, p, lw["ln2_g"], lw["ln2_b"], lw["w_proj"], lw["w_pg"])


def _ffn(h, hb, route, p, lw):
    y = _moe_dense(hb, route, lw["wg"], lw["wu"], lw["wd"])
    return _ln2_ple(h, y, p, lw)


def _ssm_in_kernel(x_ref, w_ref, u_ref):
    u_ref[...] = _dot(x_ref[...].astype(BF16), w_ref[...])


def _ssm_in(x, w):
    n, d = x.shape
    return pl.pallas_call(
        _ssm_in_kernel,
        grid=(n // ROW_TILE,),
        in_specs=[_rows(d), _full(w.shape)],
        out_specs=_rows(w.shape[1]),
        out_shape=jax.ShapeDtypeStruct((n, w.shape[1]), F32),
        compiler_params=_cparams(("parallel",)),
        name="ssm_in",
    )(x, w)


def _ssm_core_kernel(u_ref, m_ref, bp_ref, cp_ref, apow_ref, h0_ref, y_ref, hp_ref, hs_ref, sre_ref, sim_ref,
                     *, n_chunks, levels):
    pad = sre_ref.shape[0] - n_chunks
    u0 = u_ref[0]
    u1 = u_ref[1]
    bb = _dot(u0, bp_ref[0]) + _dot(u1, bp_ref[1])
    re = bb[:n_chunks, :LANES]
    im = bb[:n_chunks, LANES:]
    zeros = jnp.zeros((pad, LANES), F32)
    sre_ref[:pad, :] = zeros
    sim_ref[:pad, :] = zeros

    def shifted(d):
        return sre_ref[pl.ds(pad - d, n_chunks), :], sim_ref[pl.ds(pad - d, n_chunks), :]

    for lv in range(levels):
        d = 1 << lv
        sre_ref[pad:, :] = re
        sim_ref[pad:, :] = im
        pr, pi = shifted(d)
        ar = apow_ref[0, lv:lv + 1, :LANES]
        ai = apow_ref[0, lv:lv + 1, LANES:]
        re, im = re + ar * pr - ai * pi, im + ar * pi + ai * pr
    sre_ref[pad:, :] = re
    sim_ref[pad:, :] = im
    pr, pi = shifted(1)
    h0 = h0_ref[0]
    hprev = jnp.concatenate([jnp.concatenate([pr, pi], axis=1), h0], axis=0).astype(BF16)
    y_ref[0] = _dot(u0, m_ref[0]) + _dot(hprev, cp_ref[0])
    y_ref[1] = _dot(u1, m_ref[1]) + _dot(hprev, cp_ref[1])
    hp_ref[0] = jnp.concatenate([re[n_chunks - 1:, :], im[n_chunks - 1:, :]], axis=1)
    ar = apow_ref[0, 0:1, :LANES]
    ai = apow_ref[0, 0:1, LANES:]
    h0r = h0[:, :LANES]
    h0i = h0[:, LANES:]
    hs_ref[0] = jnp.concatenate([ar * h0r - ai * h0i + bb[n_chunks:, :LANES],
                                 ar * h0i + ai * h0r + bb[n_chunks:, LANES:]], axis=1)


def _ssm_core(ublk, tabs, h0, n_chunks):
    g, rows, width = ublk.shape
    pairs = g // 2
    ns = rows - n_chunks
    levels = max(1, (n_chunks - 1).bit_length())
    pad = 1 << (levels - 1)
    pad = max(pad, 8)
    kern = functools.partial(_ssm_core_kernel, n_chunks=n_chunks, levels=levels)
    pair3 = lambda a, b: pl.BlockSpec((2, a, b), lambda i: (i, 0, 0))
    one3 = lambda a, b: pl.BlockSpec((1, a, b), lambda i: (i, 0, 0))
    return pl.pallas_call(
        kern,
        grid=(pairs,),
        in_specs=[pair3(rows, width), pair3(width, width), pair3(width, width), pair3(width, width),
                  one3(tabs["apow"].shape[1], width), one3(ns, width)],
        out_specs=[pair3(rows, width), one3(1, width), one3(ns, width)],
        out_shape=[jax.ShapeDtypeStruct((g, rows, width), F32), jax.ShapeDtypeStruct((pairs, 1, width), F32),
                   jax.ShapeDtypeStruct((pairs, ns, width), F32)],
        scratch_shapes=[pltpu.VMEM((pad + n_chunks, LANES), F32), pltpu.VMEM((pad + n_chunks, LANES), F32)],
        compiler_params=_cparams(("parallel",)),
        name="ssm_core",
    )(ublk, tabs["mbig"], tabs["bpair"], tabs["cpair"], tabs["apow"], h0)


def _gelu_tanh(y):
    c = math.sqrt(2.0 / math.pi)
    return 0.5 * y * (1.0 + jnp.tanh(c * (y + 0.044715 * (y * y * y))))


def _ssm_out_kernel(y_ref, u_ref, x_ref, d_ref, wglu_ref, g_ref, b_ref, wrh_ref, wrl_ref, br_ref,
                    h_ref, hb_ref, route_ref):
    y = y_ref[...] + d_ref[...] * u_ref[...]
    z = _dot(_gelu_tanh(y).astype(BF16), wglu_ref[...])
    dm = z.shape[1] // 2
    t = z[:, :dm] * jax.nn.sigmoid(z[:, dm:])
    _post_mix(t, x_ref[...], g_ref[...], b_ref[...], wrh_ref[...], wrl_ref[...], br_ref[...],
              h_ref, hb_ref, route_ref)


def _ssm_out(y, u, x, d_skip, wglu, lw):
    n, d = x.shape
    specs, shapes = _post_out(n, d)
    return pl.pallas_call(
        _ssm_out_kernel,
        grid=(n // ROW_TILE,),
        in_specs=[_rows(d), _rows(d), _rows(d), _full((1, d)), _full(wglu.shape), _full((1, d)), _full((1, d)),
                  _full(lw["wr_hi"].shape), _full(lw["wr_lo"].shape), _full((1, ROUTER_LANES))],
        out_specs=specs, out_shape=shapes,
        compiler_params=_cparams(("parallel",)),
        name="ssm_out_ln_router",
    )(y, u, x, d_skip, wglu, lw["ln1_g"], lw["ln1_b"], lw["wr_hi"], lw["wr_lo"], lw["br"])


def _rope_tables(pos):
    half = QK_ROPE // 2
    inv = ROPE_BASE ** (-jnp.arange(half, dtype=F32) / half)
    ang = pos.astype(F32)[:, None] * inv[None, :]
    cos, sin = jnp.cos(ang), jnp.sin(ang)
    n = pos.shape[0]
    ones = jnp.ones((n, QK_NOPE), F32)
    zeros = jnp.zeros((n, SLOT - QK_NOPE - QK_ROPE), F32)
    ct = jnp.concatenate([ones, cos, cos, zeros], axis=1)
    st = jnp.concatenate([jnp.zeros((n, QK_NOPE), F32), -sin, sin, zeros], axis=1)
    return ct, st


def _slot_cols(w, width):
    k = w.shape[0]
    return jnp.pad(w, ((0, 0), (0, 0), (0, SLOT - width))).reshape(k, N_HEADS * SLOT)


def _mla_tables(w_in, q_norm, kv_norm, w_uq, w_uk, w_uv, w_o):
    half = QK_ROPE // 2
    o = Q_LORA + KV_LORA
    d = w_in.shape[0]
    kpe_w = w_in[:, o:]
    kpe_sw = jnp.concatenate([kpe_w[:, half:], kpe_w[:, :half]], axis=1)
    zl = jnp.zeros((d, QK_NOPE), F32)
    zr = jnp.zeros((d, SLOT - QK_NOPE - QK_ROPE), F32)
    w_in_e = jnp.concatenate([w_in[:, :o], zl, kpe_w, zr, zl, kpe_sw, zr], axis=1)
    wq = w_uq.reshape(Q_LORA, N_HEADS, QK_NOPE + QK_ROPE)
    pe = wq[:, :, QK_NOPE:]
    pe_sw = jnp.concatenate([pe[:, :, half:], pe[:, :, :half]], axis=2)
    wqb = jnp.concatenate([jnp.zeros_like(wq[:, :, :QK_NOPE]), pe_sw], axis=2)
    vone = jnp.zeros((N_HEADS, SLOT), F32).at[:, V_DIM].set(1.0).reshape(1, N_HEADS * SLOT)
    wabs = jnp.pad(jnp.transpose(w_uk, (1, 2, 0)), ((0, 0), (0, SLOT - QK_NOPE), (0, 0)))
    wuv = jnp.pad(jnp.transpose(w_uv, (1, 0, 2)), ((0, 0), (0, 0), (0, SLOT - V_DIM)))
    wo = jnp.pad(w_o.reshape(N_HEADS, V_DIM, -1), ((0, 0), (0, SLOT - V_DIM), (0, 0)))
    return dict(
        w_in=w_in_e.astype(BF16), qn=q_norm.reshape(1, -1), kvn=kv_norm.reshape(1, -1),
        wqa=_slot_cols(wq, QK_NOPE + QK_ROPE).astype(BF16), wqb=_slot_cols(wqb, QK_NOPE + QK_ROPE).astype(BF16),
        wk=_slot_cols(w_uk, QK_NOPE).astype(BF16), wv=_slot_cols(w_uv, V_DIM).astype(BF16), vone=vone,
        wabs=wabs.astype(BF16), wuv=wuv.astype(BF16), wo=wo.reshape(N_HEADS * SLOT, -1).astype(BF16))


def _cmul(a, b):
    return a[0] * b[0] - a[1] * b[1], a[0] * b[1] + a[1] * b[0]


def _ssm_tables(a_re, a_im, log_dt, b_re, b_im, c_re, c_im, levels):
    t = SSM_T
    g, p = a_re.shape
    c = SSM_GROUP_CH
    hi = lax.Precision.HIGHEST
    dt = jnp.exp(log_dt)[:, None]
    mag = jnp.exp(a_re * dt)
    lam_bar = (mag * jnp.cos(a_im * dt), mag * jnp.sin(a_im * dt))
    den = a_re * a_re + a_im * a_im
    quo = (((lam_bar[0] - 1.0) * a_re + lam_bar[1] * a_im) / den,
           (lam_bar[1] * a_re - (lam_bar[0] - 1.0) * a_im) / den)
    b_bar = _cmul((quo[0][:, :, None], quo[1][:, :, None]), (b_re, b_im))
    pw = [(jnp.ones_like(a_re), jnp.zeros_like(a_re))]
    for _ in range(t):
        pw.append(_cmul(pw[-1], lam_bar))
    pw_r = jnp.stack([x[0] for x in pw])
    pw_i = jnp.stack([x[1] for x in pw])
    cp = _cmul((c_re[None], c_im[None]), (pw_r[:t, :, None, :], pw_i[:t, :, None, :]))
    taps = (jnp.einsum("tgop,gpi->tgoi", cp[0], b_bar[0], precision=hi)
            - jnp.einsum("tgop,gpi->tgoi", cp[1], b_bar[1], precision=hi))
    lag = jnp.arange(t)[None, :] - jnp.arange(t)[:, None]
    blocks = jnp.where((lag >= 0)[:, :, None, None, None], taps[jnp.clip(lag, 0, t - 1)], 0.0)
    mbig = jnp.transpose(blocks, (2, 0, 4, 1, 3)).reshape(g, t * c, t * c)
    rev = t - 1 - jnp.arange(t)
    bfl = _cmul((pw_r[rev][:, :, :, None], pw_i[rev][:, :, :, None]), (b_bar[0][None], b_bar[1][None]))
    bfl = [jnp.transpose(x, (1, 0, 3, 2)).reshape(g, t * c, p) for x in bfl]
    par = (jnp.arange(g) % 2)[:, None, None]
    z = jnp.zeros_like(bfl[0])

    def pair_cols(v):
        return jnp.where(par == 0, jnp.concatenate([v, z], axis=2), jnp.concatenate([z, v], axis=2))

    bpair = jnp.concatenate([pair_cols(bfl[0]), pair_cols(bfl[1])], axis=2)
    cfl = _cmul((c_re[None], c_im[None]), (pw_r[1:t + 1, :, None, :], pw_i[1:t + 1, :, None, :]))
    cfl = [jnp.transpose(x, (1, 3, 0, 2)).reshape(g, p, t * c) for x in cfl]
    zc = jnp.zeros_like(cfl[0])

    def pair_rows(v):
        return jnp.where(par == 0, jnp.concatenate([v, zc], axis=1), jnp.concatenate([zc, v], axis=1))

    cpair = jnp.concatenate([pair_rows(cfl[0]), pair_rows(-cfl[1])], axis=1)
    ap = [pw[t]]
    for _ in range(levels - 1):
        ap.append(_cmul(ap[-1], ap[-1]))

    def pair_lanes(xs):
        v = jnp.stack(xs, axis=1).reshape(g // 2, 2, levels, p)
        return jnp.transpose(v, (0, 2, 1, 3)).reshape(g // 2, levels, 2 * p)

    apow = jnp.concatenate([pair_lanes([x[0] for x in ap]), pair_lanes([x[1] for x in ap])], axis=2)
    rows = -(-levels // 8) * 8
    apow = jnp.pad(apow, ((0, 0), (0, rows - levels), (0, 0)))
    return dict(mbig=mbig.astype(BF16), bpair=bpair.astype(BF16), cpair=cpair.astype(BF16), apow=apow)


def _pair_states(re, im):
    b, g, p = re.shape
    r = jnp.transpose(re.reshape(b, g // 2, 2 * p), (1, 0, 2))
    i = jnp.transpose(im.reshape(b, g // 2, 2 * p), (1, 0, 2))
    return jnp.concatenate([r, i], axis=2)


def _unpair_states(h):
    pairs, b, w = h.shape
    p = w // 4
    r = jnp.transpose(h[:, :, :2 * p], (1, 0, 2)).reshape(b, pairs * 2, p)
    i = jnp.transpose(h[:, :, 2 * p:], (1, 0, 2)).reshape(b, pairs * 2, p)
    return r, i


def _layer_tables(i, ln1_g, ln1_b, ln2_g, ln2_b, w_rg, b_rg, w_re, b_re, w_gate, w_up, w_down, w_proj, w_pg):
    d = w_rg.shape[1]
    wr = jnp.concatenate([w_rg[i], jnp.transpose(w_re[i], (1, 0, 2)).reshape(d, N_EXPERTS)], axis=1)
    wr = jnp.pad(wr, ((0, 0), (0, ROUTER_LANES - wr.shape[1])))
    wr_hi = wr.astype(BF16)
    wr_lo = (wr - wr_hi.astype(F32)).astype(BF16)
    br = jnp.pad(jnp.concatenate([b_rg[i], b_re[i].reshape(-1)]), (0, ROUTER_LANES - MOE_GROUPS - N_EXPERTS))
    ff = w_gate.shape[-1]
    return dict(
        ln1_g=ln1_g[i].reshape(1, d), ln1_b=ln1_b[i].reshape(1, d), ln2_g=ln2_g[i].reshape(1, d),
        ln2_b=ln2_b[i].reshape(1, d), wr_hi=wr_hi, wr_lo=wr_lo, br=br.reshape(1, ROUTER_LANES),
        wg=w_gate[i].reshape(N_EXPERTS, d, ff).astype(BF16), wu=w_up[i].reshape(N_EXPERTS, d, ff).astype(BF16),
        wd=w_down[i].reshape(N_EXPERTS, ff, d).astype(BF16),
        w_proj=w_proj[i].astype(BF16), w_pg=w_pg[i].astype(BF16))


def kernel(x_prompt, x_sample, p_prompt, p_sample, cache_mla_ckv, cache_mla_kpe, state_ssm_re, state_ssm_im, mla_w_in, mla_q_norm, mla_kv_norm, mla_w_uq, mla_w_uk, mla_w_uv, mla_w_o, ssm_w_in, ssm_a_re, ssm_a_im, ssm_log_dt, ssm_b_re, ssm_b_im, ssm_c_re, ssm_c_im, ssm_d, ssm_w_glu, ln1_g, ln1_b, ln2_g, ln2_b, moe_w_rg, moe_b_rg, moe_w_re, moe_b_re, moe_w_gate, moe_w_up, moe_w_down, ple_w_proj, ple_w_gate):
    bp, n_prompt, d = x_prompt.shape
    nb, seq, _ = x_sample.shape
    past = cache_mla_ckv.shape[2]
    assert bp == 1 and seq == SSM_T and n_prompt % ROW_TILE == 0 and (nb * seq) % ROW_TILE == 0
    assert n_prompt % CHUNK == 0 and past % CHUNK == 0 and seq <= CHUNK
    n_samp = nb * seq
    n = n_prompt + n_samp
    x = jnp.concatenate([x_prompt.reshape(n_prompt, d), x_sample.reshape(n_samp, d)], axis=0)
    p_all = jnp.concatenate([p_prompt.reshape(DEPTH, n_prompt, -1), p_sample.reshape(DEPTH, n_samp, -1)], axis=1)
    layer_args = (ln1_g, ln1_b, ln2_g, ln2_b, moe_w_rg, moe_b_rg, moe_w_re, moe_b_re, moe_w_gate, moe_w_up,
                  moe_w_down, ple_w_proj, ple_w_gate)
    lw = _layer_tables(0, *layer_args)
    mw = _mla_tables(mla_w_in[0], mla_q_norm[0], mla_kv_norm[0], mla_w_uq[0], mla_w_uk[0], mla_w_uv[0], mla_w_o[0])
    pos = jnp.concatenate([jnp.arange(n_prompt, dtype=jnp.int32),
                           past + jnp.tile(jnp.arange(seq, dtype=jnp.int32), nb)])
    ct, st = _rope_tables(pos)
    q, k, v, ckv, kpe_slot = _mla_proj(x, ct, st, mw)
    o_prompt = _prompt_attention(q, k, v, n_prompt)
    o_sample = _sample_attention(q, cache_mla_ckv[0], cache_mla_kpe[0], ckv, kpe_slot, mw["wabs"], mw["wuv"],
                                 n_prompt)
    h, hb, route = _attn_out(o_prompt, o_sample, x, mw["wo"], lw)
    x = _ffn(h, hb, route, p_all[0], lw)
    kpe = kpe_slot[:, QK_NOPE:QK_NOPE + QK_ROPE]

    lw = _layer_tables(1, *layer_args)
    n_chunks = n_prompt // SSM_T
    rows = n // SSM_T
    groups = d // SSM_GROUP_CH
    levels = max(1, (n_chunks - 1).bit_length())
    tabs = _ssm_tables(ssm_a_re[0], ssm_a_im[0], ssm_log_dt[0], ssm_b_re[0], ssm_b_im[0], ssm_c_re[0],
                       ssm_c_im[0], levels)
    u = _ssm_in(x, ssm_w_in[0].astype(BF16))
    ublk = jnp.transpose(u.reshape(rows, SSM_T, groups, SSM_GROUP_CH), (2, 0, 1, 3))
    ublk = ublk.reshape(groups, rows, SSM_T * SSM_GROUP_CH).astype(BF16)
    h0 = _pair_states(state_ssm_re[0], state_ssm_im[0])
    yblk, hp, hs = _ssm_core(ublk, tabs, h0, n_chunks)
    y = jnp.transpose(yblk.reshape(groups, rows, SSM_T, SSM_GROUP_CH), (1, 2, 0, 3)).reshape(n, d)
    h, hb, route = _ssm_out(y, u, x, ssm_d[0].reshape(1, d), ssm_w_glu[0].astype(BF16), lw)
    x = _ffn(h, hb, route, p_all[1], lw)
    re_p, im_p = _unpair_states(hp)
    re_s, im_s = _unpair_states(hs)

    return (x[:n_prompt].reshape(1, n_prompt, d), x[n_prompt:].reshape(nb, seq, d),
            ckv[:n_prompt].reshape(1, 1, n_prompt, KV_LORA), kpe[:n_prompt].reshape(1, 1, n_prompt, QK_ROPE),
            re_p[None], im_p[None],
            ckv[n_prompt:].reshape(1, nb, seq, KV_LORA), kpe[n_prompt:].reshape(1, nb, seq, QK_ROPE),
            re_s[None], im_s[None])
```

```python
import functools
import math

import jax
import jax.numpy as jnp
from jax import lax
from jax.experimental import pallas as pl
from jax.experimental.pallas import tpu as pltpu

F32 = jnp.float32
BF16 = jnp.bfloat16

N_HEADS = 16
Q_LORA = 256
KV_LORA = 128
QK_NOPE = 64
QK_ROPE = 32
V_DIM = 64
ROPE_BASE = 10000.0
ATTN_SCALE = (QK_NOPE + QK_ROPE) ** -0.5
CHUNK = 64
SSM_GROUP_CH = 16
SSM_STATE = 64
MOE_GROUPS = 4
MOE_EPG = 8
N_EXPERTS = MOE_GROUPS * MOE_EPG
DEPTH = 2
ALPHA = (2.0 * DEPTH) ** 0.25
LN_EPS = 1e-5
RMS_EPS = 1e-6
NEG = -1e30

LANES = 128
SLOT = LANES
ROW_TILE = 256
SSM_T = 16
VMEM_LIMIT = 56 * 1024 * 1024
ROUTER_LANES = LANES
EXPERT_LANE0 = MOE_GROUPS


def _cparams(sem):
    return pltpu.CompilerParams(dimension_semantics=sem, vmem_limit_bytes=VMEM_LIMIT)


def _dot(a, b):
    return jnp.dot(a, b, preferred_element_type=F32)


def _dot_nt(a, b):
    return lax.dot_general(a, b, (((1,), (1,)), ((), ())), preferred_element_type=F32)


def _full(shape):
    nd = len(shape)
    return pl.BlockSpec(shape, lambda *_: (0,) * nd)


def _rows(width, rows=ROW_TILE):
    return pl.BlockSpec((rows, width), lambda i: (i, 0))


def _layernorm(t, g, b):
    mu = jnp.mean(t, axis=-1, keepdims=True)
    d = t - mu
    var = jnp.mean(d * d, axis=-1, keepdims=True)
    return d * lax.rsqrt(var + LN_EPS) * g + b


def _rmsnorm(t, g):
    return t * lax.rsqrt(jnp.mean(t * t, axis=-1, keepdims=True) + RMS_EPS) * g


def _mla_proj_kernel(x_ref, ct_ref, st_ref, w_in_ref, qn_ref, kvn_ref, wqa_ref, wqb_ref, wk_ref, wv_ref,
                     vone_ref, q_ref, k_ref, v_ref, ckv_ref, kpe_ref):
    x = x_ref[...].astype(BF16)
    z = _dot(x, w_in_ref[...])
    cq = _rmsnorm(z[:, :Q_LORA], qn_ref[...])
    ckv = _rmsnorm(z[:, Q_LORA:Q_LORA + KV_LORA], kvn_ref[...])
    ct = ct_ref[...]
    st = st_ref[...]
    o = Q_LORA + KV_LORA
    kpe = z[:, o:o + SLOT] * ct + z[:, o + SLOT:o + 2 * SLOT] * st
    ckv_ref[...] = ckv
    kpe_ref[...] = kpe
    cqb = cq.astype(BF16)
    ckvb = ckv.astype(BF16)
    qa = _dot(cqb, wqa_ref[...])
    qb = _dot(cqb, wqb_ref[...])
    kn = _dot(ckvb, wk_ref[...])
    vv = _dot(ckvb, wv_ref[...]) + vone_ref[...]
    scale = ATTN_SCALE * math.log2(math.e)
    for h in range(N_HEADS):
        sl = slice(h * SLOT, (h + 1) * SLOT)
        q_ref[:, sl] = ((qa[:, sl] * ct + qb[:, sl] * st) * scale).astype(BF16)
        k_ref[:, sl] = (kn[:, sl] + kpe).astype(BF16)
    v_ref[...] = vv.astype(BF16)


def _mla_proj(x, ct, st, w):
    n = x.shape[0]
    wide = N_HEADS * SLOT
    return pl.pallas_call(
        _mla_proj_kernel,
        grid=(n // ROW_TILE,),
        in_specs=[_rows(x.shape[1]), _rows(SLOT), _rows(SLOT), _full(w["w_in"].shape), _full((1, Q_LORA)),
                  _full((1, KV_LORA)), _full(w["wqa"].shape), _full(w["wqb"].shape), _full(w["wk"].shape),
                  _full(w["wv"].shape), _full((1, wide))],
        out_specs=[_rows(wide), _rows(wide), _rows(wide), _rows(KV_LORA), _rows(SLOT)],
        out_shape=[jax.ShapeDtypeStruct((n, wide), BF16)] * 3
        + [jax.ShapeDtypeStruct((n, KV_LORA), F32), jax.ShapeDtypeStruct((n, SLOT), F32)],
        compiler_params=_cparams(("parallel",)),
        name="mla_proj",
    )(x, ct, st, w["w_in"], w["qn"], w["kvn"], w["wqa"], w["wqb"], w["wk"], w["wv"], w["vone"])


ATT_TQ = 1024
ATT_TK = 2048
ATT_G = 2


def _attn_kernel(q_ref, k_ref, v_ref, o_ref):
    qi = pl.program_id(1)
    slots = [slice(g * SLOT, (g + 1) * SLOT) for g in range(ATT_G)]
    ratio = ATT_TK // ATT_TQ

    def update(off, tk, state, mask):
        out = []
        for sl, (m, acc) in zip(slots, state):
            s = _dot_nt(q_ref[:, sl], k_ref[pl.ds(off, tk), sl])
            if mask is not None:
                s = jnp.where(mask, s, NEG)
            m_new = jnp.maximum(m, jnp.max(s, axis=-1, keepdims=True))
            p = jnp.exp2(s - m_new)
            pv = _dot(p.astype(BF16), v_ref[pl.ds(off, tk), sl])
            out.append((m_new, acc * jnp.exp2(m - m_new) + pv))
        return tuple(out)

    def body(j, state):
        return update(pl.multiple_of(j * ATT_TK, ATT_TK), ATT_TK, state, None)

    init = tuple((jnp.full((ATT_TQ, 1), NEG, F32), jnp.zeros((ATT_TQ, SLOT), F32)) for _ in range(ATT_G))
    state = lax.fori_loop(0, qi // ratio, body, init)

    def tail(rem):
        tk = (rem + 1) * ATT_TQ
        rc = lax.broadcasted_iota(jnp.int32, (ATT_TQ, tk), 0) // CHUNK
        cc = lax.broadcasted_iota(jnp.int32, (ATT_TQ, tk), 1) // CHUNK - rem * (ATT_TQ // CHUNK)
        return lambda st: update(pl.multiple_of((qi - rem) * ATT_TQ, ATT_TQ), tk, st, cc <= rc)

    state = lax.switch(qi % ratio, [tail(rem) for rem in range(ratio)], state)
    lane = lax.broadcasted_iota(jnp.int32, (ATT_TQ, SLOT), 1)
    for g in range(ATT_G):
        acc = state[g][1]
        l = jnp.sum(jnp.where(lane == V_DIM, acc, 0.0), axis=-1, keepdims=True)
        o_ref[:, g * SLOT:(g + 1) * SLOT] = (acc / l).astype(BF16)


def _prompt_attention(q, k, v, n_prompt):
    n = q.shape[0]
    wide = ATT_G * SLOT
    assert ATT_TK % ATT_TQ == 0 and ATT_TQ % CHUNK == 0 and n_prompt % ATT_TQ == 0 and N_HEADS % ATT_G == 0
    return pl.pallas_call(
        _attn_kernel,
        grid=(N_HEADS // ATT_G, n_prompt // ATT_TQ),
        in_specs=[pl.BlockSpec((ATT_TQ, wide), lambda h, i: (i, h)),
                  pl.BlockSpec((n_prompt, wide), lambda h, i: (0, h)),
                  pl.BlockSpec((n_prompt, wide), lambda h, i: (0, h))],
        out_specs=pl.BlockSpec((ATT_TQ, wide), lambda h, i: (i, h)),
        out_shape=jax.ShapeDtypeStruct((n_prompt, N_HEADS * SLOT), BF16),
        compiler_params=_cparams(("parallel", "arbitrary")),
        name="prompt_attention",
    )(q, k, v)


def _sample_attn_kernel(q_ref, ckvc_ref, kpec_ref, ckvn_ref, kpen_ref, wabs_ref, wuv_ref, o_ref):
    q = q_ref[...]
    seq = q.shape[0]
    qa, qp = [], []
    for h in range(N_HEADS):
        qh = q[:, h * SLOT:(h + 1) * SLOT]
        qa.append(_dot(qh, wabs_ref[h]))
        qp.append(qh[:, QK_NOPE:QK_NOPE + QK_ROPE])
    qa = jnp.concatenate(qa, axis=0).astype(BF16)
    qp = jnp.concatenate(qp, axis=0)
    ckvc = ckvc_ref[0].astype(BF16)
    kpec = kpec_ref[0].astype(BF16)
    ckvn = ckvn_ref[...].astype(BF16)
    kpen = kpen_ref[...][:, QK_NOPE:QK_NOPE + QK_ROPE].astype(BF16)
    s_c = _dot_nt(qa, ckvc) + _dot_nt(qp, kpec)
    s_n = _dot_nt(qa, ckvn) + _dot_nt(qp, kpen)
    m = jnp.maximum(jnp.max(s_c, axis=-1, keepdims=True), jnp.max(s_n, axis=-1, keepdims=True))
    p_c = jnp.exp2(s_c - m)
    p_n = jnp.exp2(s_n - m)
    l = jnp.sum(p_c, axis=-1, keepdims=True) + jnp.sum(p_n, axis=-1, keepdims=True)
    ol = (_dot(p_c.astype(BF16), ckvc) + _dot(p_n.astype(BF16), ckvn)) / l
    olb = ol.astype(BF16)
    for h in range(N_HEADS):
        o_ref[:, h * SLOT:(h + 1) * SLOT] = _dot(olb[h * seq:(h + 1) * seq], wuv_ref[h]).astype(BF16)


def _sample_attention(q, cache_ckv, cache_kpe, ckv_new, kpe_new, wabs, wuv, n_prompt):
    nb, past, _ = cache_ckv.shape
    seq = (q.shape[0] - n_prompt) // nb
    base = n_prompt // seq
    wide = N_HEADS * SLOT
    return pl.pallas_call(
        _sample_attn_kernel,
        grid=(nb,),
        in_specs=[pl.BlockSpec((seq, wide), lambda b: (base + b, 0)),
                  pl.BlockSpec((1, past, KV_LORA), lambda b: (b, 0, 0)),
                  pl.BlockSpec((1, past, QK_ROPE), lambda b: (b, 0, 0)),
                  pl.BlockSpec((seq, KV_LORA), lambda b: (base + b, 0)),
                  pl.BlockSpec((seq, SLOT), lambda b: (base + b, 0)),
                  _full(wabs.shape), _full(wuv.shape)],
        out_specs=pl.BlockSpec((seq, wide), lambda b: (b, 0)),
        out_shape=jax.ShapeDtypeStruct((nb * seq, wide), BF16),
        compiler_params=_cparams(("parallel",)),
        name="sample_attention",
    )(q, cache_ckv, cache_kpe, ckv_new, kpe_new, wabs, wuv)


ROUTE_E1, ROUTE_E2, ROUTE_W1, ROUTE_W2 = range(4)


def _lane_col(tile, k):
    lane = lax.broadcasted_iota(jnp.int32, tile.shape, 1)
    return jnp.sum(jnp.where(lane == k, tile, 0.0), axis=-1, keepdims=True)


def _route(lg):
    lane = lax.broadcasted_iota(jnp.int32, lg.shape, 1)
    lanef = lane.astype(F32)
    big = float(ROUTER_LANES)
    is_g = lane < MOE_GROUPS
    gl = jnp.where(is_g, lg, NEG)
    gmax = jnp.max(gl, axis=-1, keepdims=True)
    gsel = jnp.min(jnp.where(gl == gmax, lanef, big), axis=-1, keepdims=True)
    gate_g = 1.0 / jnp.sum(jnp.where(is_g, jnp.exp(gl - gmax), 0.0), axis=-1, keepdims=True)
    lo = EXPERT_LANE0 + MOE_EPG * gsel
    el = jnp.where(jnp.logical_and(lanef >= lo, lanef < lo + MOE_EPG), lg, NEG)
    m1 = jnp.max(el, axis=-1, keepdims=True)
    i1 = jnp.min(jnp.where(el == m1, lanef, big), axis=-1, keepdims=True)
    el2 = jnp.where(lanef == i1, NEG, el)
    m2 = jnp.max(el2, axis=-1, keepdims=True)
    i2 = jnp.min(jnp.where(el2 == m2, lanef, big), axis=-1, keepdims=True)
    r = jnp.exp(m2 - m1)
    w1 = 1.0 / (1.0 + r)
    w2 = r / (1.0 + r)
    cols = {ROUTE_E1: i1 - EXPERT_LANE0, ROUTE_E2: i2 - EXPERT_LANE0, ROUTE_W1: gate_g * w1, ROUTE_W2: gate_g * w2}
    route = jnp.zeros(lg.shape, F32)
    for k, c in cols.items():
        route = jnp.where(lane == k, c, route)
    return route


def _post_mix(t, x, g, b, wr_hi, wr_lo, br, h_ref, hb_ref, route_ref):
    h = _layernorm(ALPHA * x + t, g, b)
    h_ref[...] = h
    hb = h.astype(BF16)
    hb_ref[...] = hb
    h_lo = (h - hb.astype(F32)).astype(BF16)
    lg = _dot(hb, wr_hi) + (_dot(hb, wr_lo) + _dot(h_lo, wr_hi)) + br
    route_ref[...] = _route(lg)


def _attn_out_kernel(op_ref, os_ref, x_ref, wo_ref, g_ref, b_ref, wrh_ref, wrl_ref, br_ref, h_ref, hb_ref,
                     route_ref, *, prompt_tiles):
    o = jnp.where(pl.program_id(0) < prompt_tiles, op_ref[...], os_ref[...])
    t = _dot(o, wo_ref[...])
    _post_mix(t, x_ref[...], g_ref[...], b_ref[...], wrh_ref[...], wrl_ref[...], br_ref[...],
              h_ref, hb_ref, route_ref)


def _post_out(n, d):
    specs = [_rows(d), _rows(d), _rows(ROUTER_LANES)]
    shapes = [jax.ShapeDtypeStruct((n, d), F32), jax.ShapeDtypeStruct((n, d), BF16),
              jax.ShapeDtypeStruct((n, ROUTER_LANES), F32)]
    return specs, shapes


def _attn_out(o_prompt, o_sample, x, wo, lw):
    n, d = x.shape
    wide = o_prompt.shape[1]
    pt = o_prompt.shape[0] // ROW_TILE
    specs, shapes = _post_out(n, d)
    return pl.pallas_call(
        functools.partial(_attn_out_kernel, prompt_tiles=pt),
        grid=(n // ROW_TILE,),
        in_specs=[pl.BlockSpec((ROW_TILE, wide), lambda i: (jnp.minimum(i, pt - 1), 0)),
                  pl.BlockSpec((ROW_TILE, wide), lambda i: (jnp.maximum(i - pt, 0), 0)),
                  _rows(d), _full(wo.shape), _full((1, d)), _full((1, d)),
                  _full(lw["wr_hi"].shape), _full(lw["wr_lo"].shape), _full((1, ROUTER_LANES))],
        out_specs=specs, out_shape=shapes,
        compiler_params=_cparams(("parallel",)),
        name="attn_out_ln_router",
    )(o_prompt, o_sample, x, wo, lw["ln1_g"], lw["ln1_b"], lw["wr_hi"], lw["wr_lo"], lw["br"])


MOE_EPS = 4


def _moe_dense_kernel(hb_ref, route_ref, wg_ref, wu_ref, wd_ref, y_ref):
    x = hb_ref[...]
    route = route_ref[...]
    e1 = _lane_col(route, ROUTE_E1)
    e2 = _lane_col(route, ROUTE_E2)
    w1 = _lane_col(route, ROUTE_W1)
    w2 = _lane_col(route, ROUTE_W2)
    y = None
    for k in range(MOE_EPS):
        e = (pl.program_id(1) * MOE_EPS + k).astype(F32)
        g = _dot(x, wg_ref[k])
        u = _dot(x, wu_ref[k])
        gate = jnp.where(e1 == e, w1, 0.0) + jnp.where(e2 == e, w2, 0.0)
        hdn = (g * jax.nn.sigmoid(g)) * u * gate
        yk = _dot(hdn.astype(BF16), wd_ref[k])
        y = yk if y is None else y + yk

    @pl.when(pl.program_id(1) == 0)
    def _():
        y_ref[...] = y

    @pl.when(pl.program_id(1) > 0)
    def _():
        y_ref[...] += y


def _moe_tile(n):
    return max(t for t in range(LANES, 2048 + 1, LANES) if n % t == 0)


def _moe_dense(hb, route, wg, wu, wd):
    n, d = hb.shape
    ff = wg.shape[-1]
    tile = _moe_tile(n)
    return pl.pallas_call(
        _moe_dense_kernel,
        grid=(n // tile, N_EXPERTS // MOE_EPS),
        in_specs=[pl.BlockSpec((tile, d), lambda i, e: (i, 0)),
                  pl.BlockSpec((tile, ROUTER_LANES), lambda i, e: (i, 0)),
                  pl.BlockSpec((MOE_EPS, d, ff), lambda i, e: (e, 0, 0)),
                  pl.BlockSpec((MOE_EPS, d, ff), lambda i, e: (e, 0, 0)),
                  pl.BlockSpec((MOE_EPS, ff, d), lambda i, e: (e, 0, 0))],
        out_specs=pl.BlockSpec((tile, d), lambda i, e: (i, 0)),
        out_shape=jax.ShapeDtypeStruct((n, d), F32),
        compiler_params=_cparams(("parallel", "arbitrary")),
        name="moe_dense",
    )(hb, route, wg, wu, wd)


def _ln2_ple_kernel(h_ref, y_ref, p_ref, g_ref, b_ref, wp_ref, wpg_ref, x_ref):
    h2 = _layernorm(ALPHA * h_ref[...] + y_ref[...], g_ref[...], b_ref[...])
    proj = _dot(p_ref[...].astype(BF16), wp_ref[...])
    gate = jax.nn.sigmoid(_dot(h2.astype(BF16), wpg_ref[...]))
    x_ref[...] = h2 + proj * gate


def _ln2_ple(h, y, p, lw):
    n, d = h.shape
    return pl.pallas_call(
        _ln2_ple_kernel,
        grid=(n // ROW_TILE,),
        in_specs=[_rows(d), _rows(d), _rows(p.shape[1]), _full((1, d)), _full((1, d)),
                  _full(lw["w_proj"].shape), _full(lw["w_pg"].shape)],
        out_specs=_rows(d),
        out_shape=jax.ShapeDtypeStruct((n, d), F32),
        compiler_params=_cparams(("parallel",)),
        name="ln2_ple",
    )(h, y, p, lw["ln2_g"], lw["ln2_b"], lw["w_proj"], lw["w_pg"])


def _ffn(h, hb, route, p, lw):
    y = _moe_dense(hb, route, lw["wg"], lw["wu"], lw["wd"])
    return _ln2_ple(h, y, p, lw)


def _ssm_in_kernel(x_ref, w_ref, u_ref):
    u_ref[...] = _dot(x_ref[...].astype(BF16), w_ref[...])


def _ssm_in(x, w):
    n, d = x.shape
    return pl.pallas_call(
        _ssm_in_kernel,
        grid=(n // ROW_TILE,),
        in_specs=[_rows(d), _full(w.shape)],
        out_specs=_rows(w.shape[1]),
        out_shape=jax.ShapeDtypeStruct((n, w.shape[1]), F32),
        compiler_params=_cparams(("parallel",)),
        name="ssm_in",
    )(x, w)


SSM_GB = LANES // SSM_GROUP_CH
SSM_PAIR = 2 * LANES


def _ssm_core_kernel(u_ref, bd_ref, bst_ref, cst_ref, apow_ref, h0_ref, y_ref, hp_ref, hs_ref, sre_ref, sim_ref,
                     *, n_chunks, levels):
    t = SSM_T
    rows = u_ref.shape[0] // t
    pad = sre_ref.shape[0] - n_chunks
    npair = SSM_GB // 2
    ut = [u_ref[pl.ds(tl, rows, stride=t), :].astype(BF16) for tl in range(t)]
    u2 = [jnp.concatenate([ut[2 * k], ut[2 * k + 1]], axis=1) for k in range(t // 2)]
    bb = None
    for k in range(t // 2):
        part = _dot(u2[k], jnp.concatenate([bst_ref[0, 2 * k], bst_ref[0, 2 * k + 1]], axis=0))
        bb = part if bb is None else bb + part
    zeros = jnp.zeros((pad, LANES), F32)
    sre_ref[:pad, :] = zeros
    sim_ref[:pad, :] = zeros

    def shifted(d):
        return sre_ref[pl.ds(pad - d, n_chunks), :], sim_ref[pl.ds(pad - d, n_chunks), :]

    h0 = h0_ref[0]
    hprev, h_last, h_samp = [], [], []
    for j in range(npair):
        lo = j * SSM_PAIR
        re = bb[:n_chunks, lo:lo + LANES]
        im = bb[:n_chunks, lo + LANES:lo + SSM_PAIR]
        for lv in range(levels):
            sre_ref[pad:, :] = re
            sim_ref[pad:, :] = im
            pr, pi = shifted(1 << lv)
            ar = apow_ref[0, lv:lv + 1, lo:lo + LANES]
            ai = apow_ref[0, lv:lv + 1, lo + LANES:lo + SSM_PAIR]
            re, im = re + ar * pr - ai * pi, im + ar * pi + ai * pr
        sre_ref[pad:, :] = re
        sim_ref[pad:, :] = im
        pr, pi = shifted(1)
        h0j = h0[:, lo:lo + SSM_PAIR]
        hprev.append(jnp.concatenate([jnp.concatenate([pr, pi], axis=1), h0j], axis=0).astype(BF16))
        h_last.append(jnp.concatenate([re[n_chunks - 1:, :], im[n_chunks - 1:, :]], axis=1))
        ar = apow_ref[0, 0:1, lo:lo + LANES]
        ai = apow_ref[0, 0:1, lo + LANES:lo + SSM_PAIR]
        h0r = h0j[:, :LANES]
        h0i = h0j[:, LANES:]
        h_samp.append(jnp.concatenate([ar * h0r - ai * h0i + bb[n_chunks:, lo:lo + LANES],
                                       ar * h0i + ai * h0r + bb[n_chunks:, lo + LANES:lo + SSM_PAIR]], axis=1))
    hp_ref[0] = jnp.concatenate(h_last, axis=1)
    hs_ref[0] = jnp.concatenate(h_samp, axis=1)
    hprev = jnp.concatenate(hprev, axis=1)
    zero_tap = jnp.zeros((LANES, LANES), BF16)

    def tap(lag):
        return bd_ref[0, lag] if lag >= 0 else zero_tap

    for ko in range(t // 2):
        acc = _dot(hprev, jnp.concatenate([cst_ref[0, 2 * ko], cst_ref[0, 2 * ko + 1]], axis=1))
        for ki in range(ko + 1):
            lag = 2 * (ko - ki)
            w = jnp.concatenate([jnp.concatenate([tap(lag), tap(lag + 1)], axis=1),
                                 jnp.concatenate([tap(lag - 1), tap(lag)], axis=1)], axis=0)
            acc = acc + _dot(u2[ki], w)
        y_ref[pl.ds(2 * ko, rows, stride=t), :] = acc[:, :LANES]
        y_ref[pl.ds(2 * ko + 1, rows, stride=t), :] = acc[:, LANES:]


def _ssm_core(u, tabs, h0, n_chunks):
    n, d = u.shape
    nblk = d // LANES
    ns = n // SSM_T - n_chunks
    levels = max(1, (n_chunks - 1).bit_length())
    pad = max(1 << (levels - 1), 8)
    wide = (SSM_GB // 2) * SSM_PAIR
    kern = functools.partial(_ssm_core_kernel, n_chunks=n_chunks, levels=levels)
    once = dict(pipeline_mode=pl.Buffered(1))

    def blk(*shape):
        return pl.BlockSpec((1,) + shape, lambda i: (i,) + (0,) * len(shape), **once)

    return pl.pallas_call(
        kern,
        grid=(nblk,),
        in_specs=[pl.BlockSpec((n, LANES), lambda i: (0, i), **once), blk(SSM_T, LANES, LANES),
                  blk(SSM_T, LANES, wide), blk(SSM_T, wide, LANES), blk(tabs["apow"].shape[1], wide), blk(ns, wide)],
        out_specs=[pl.BlockSpec((n, LANES), lambda i: (0, i)),
                   pl.BlockSpec((1, 1, wide), lambda i: (i, 0, 0)), pl.BlockSpec((1, ns, wide), lambda i: (i, 0, 0))],
        out_shape=[jax.ShapeDtypeStruct((n, d), F32), jax.ShapeDtypeStruct((nblk, 1, wide), F32),
                   jax.ShapeDtypeStruct((nblk, ns, wide), F32)],
        scratch_shapes=[pltpu.VMEM((pad + n_chunks, LANES), F32), pltpu.VMEM((pad + n_chunks, LANES), F32)],
        compiler_params=_cparams(("parallel",)),
        name="ssm_core",
    )(u, tabs["bd"], tabs["bst"], tabs["cst"], tabs["apow"], h0)


def _gelu_tanh(y):
    c = math.sqrt(2.0 / math.pi)
    return 0.5 * y * (1.0 + jnp.tanh(c * (y + 0.044715 * (y * y * y))))


def _ssm_out_kernel(y_ref, u_ref, x_ref, d_ref, wglu_ref, g_ref, b_ref, wrh_ref, wrl_ref, br_ref,
                    h_ref, hb_ref, route_ref):
    y = y_ref[...] + d_ref[...] * u_ref[...]
    z = _dot(_gelu_tanh(y).astype(BF16), wglu_ref[...])
    dm = z.shape[1] // 2
    t = z[:, :dm] * jax.nn.sigmoid(z[:, dm:])
    _post_mix(t, x_ref[...], g_ref[...], b_ref[...], wrh_ref[...], wrl_ref[...], br_ref[...],
              h_ref, hb_ref, route_ref)


def _ssm_out(y, u, x, d_skip, wglu, lw):
    n, d = x.shape
    specs, shapes = _post_out(n, d)
    return pl.pallas_call(
        _ssm_out_kernel,
        grid=(n // ROW_TILE,),
        in_specs=[_rows(d), _rows(d), _rows(d), _full((1, d)), _full(wglu.shape), _full((1, d)), _full((1, d)),
                  _full(lw["wr_hi"].shape), _full(lw["wr_lo"].shape), _full((1, ROUTER_LANES))],
        out_specs=specs, out_shape=shapes,
        compiler_params=_cparams(("parallel",)),
        name="ssm_out_ln_router",
    )(y, u, x, d_skip, wglu, lw["ln1_g"], lw["ln1_b"], lw["wr_hi"], lw["wr_lo"], lw["br"])


def _rope_tables(pos):
    half = QK_ROPE // 2
    inv = ROPE_BASE ** (-jnp.arange(half, dtype=F32) / half)
    ang = pos.astype(F32)[:, None] * inv[None, :]
    cos, sin = jnp.cos(ang), jnp.sin(ang)
    n = pos.shape[0]
    ones = jnp.ones((n, QK_NOPE), F32)
    zeros = jnp.zeros((n, SLOT - QK_NOPE - QK_ROPE), F32)
    ct = jnp.concatenate([ones, cos, cos, zeros], axis=1)
    st = jnp.concatenate([jnp.zeros((n, QK_NOPE), F32), -sin, sin, zeros], axis=1)
    return ct, st


def _slot_cols(w, width):
    k = w.shape[0]
    return jnp.pad(w, ((0, 0), (0, 0), (0, SLOT - width))).reshape(k, N_HEADS * SLOT)


def _mla_tables(w_in, q_norm, kv_norm, w_uq, w_uk, w_uv, w_o):
    half = QK_ROPE // 2
    o = Q_LORA + KV_LORA
    d = w_in.shape[0]
    kpe_w = w_in[:, o:]
    kpe_sw = jnp.concatenate([kpe_w[:, half:], kpe_w[:, :half]], axis=1)
    zl = jnp.zeros((d, QK_NOPE), F32)
    zr = jnp.zeros((d, SLOT - QK_NOPE - QK_ROPE), F32)
    w_in_e = jnp.concatenate([w_in[:, :o], zl, kpe_w, zr, zl, kpe_sw, zr], axis=1)
    wq = w_uq.reshape(Q_LORA, N_HEADS, QK_NOPE + QK_ROPE)
    pe = wq[:, :, QK_NOPE:]
    pe_sw = jnp.concatenate([pe[:, :, half:], pe[:, :, :half]], axis=2)
    wqb = jnp.concatenate([jnp.zeros_like(wq[:, :, :QK_NOPE]), pe_sw], axis=2)
    vone = jnp.zeros((N_HEADS, SLOT), F32).at[:, V_DIM].set(1.0).reshape(1, N_HEADS * SLOT)
    wabs = jnp.pad(jnp.transpose(w_uk, (1, 2, 0)), ((0, 0), (0, SLOT - QK_NOPE), (0, 0)))
    wuv = jnp.pad(jnp.transpose(w_uv, (1, 0, 2)), ((0, 0), (0, 0), (0, SLOT - V_DIM)))
    wo = jnp.pad(w_o.reshape(N_HEADS, V_DIM, -1), ((0, 0), (0, SLOT - V_DIM), (0, 0)))
    return dict(
        w_in=w_in_e.astype(BF16), qn=q_norm.reshape(1, -1), kvn=kv_norm.reshape(1, -1),
        wqa=_slot_cols(wq, QK_NOPE + QK_ROPE).astype(BF16), wqb=_slot_cols(wqb, QK_NOPE + QK_ROPE).astype(BF16),
        wk=_slot_cols(w_uk, QK_NOPE).astype(BF16), wv=_slot_cols(w_uv, V_DIM).astype(BF16), vone=vone,
        wabs=wabs.astype(BF16), wuv=wuv.astype(BF16), wo=wo.reshape(N_HEADS * SLOT, -1).astype(BF16))


def _cmul(a, b):
    return a[0] * b[0] - a[1] * b[1], a[0] * b[1] + a[1] * b[0]


def _ssm_tables(a_re, a_im, log_dt, b_re, b_im, c_re, c_im, levels):
    t = SSM_T
    g, p = a_re.shape
    c = SSM_GROUP_CH
    hi = lax.Precision.HIGHEST
    dt = jnp.exp(log_dt)[:, None]
    mag = jnp.exp(a_re * dt)
    lam_bar = (mag * jnp.cos(a_im * dt), mag * jnp.sin(a_im * dt))
    den = a_re * a_re + a_im * a_im
    quo = (((lam_bar[0] - 1.0) * a_re + lam_bar[1] * a_im) / den,
           (lam_bar[1] * a_re - (lam_bar[0] - 1.0) * a_im) / den)
    b_bar = _cmul((quo[0][:, :, None], quo[1][:, :, None]), (b_re, b_im))
    pw = [(jnp.ones_like(a_re), jnp.zeros_like(a_re))]
    for _ in range(t):
        pw.append(_cmul(pw[-1], lam_bar))
    pw_r = jnp.stack([x[0] for x in pw])
    pw_i = jnp.stack([x[1] for x in pw])
    cp = _cmul((c_re[None], c_im[None]), (pw_r[:t, :, None, :], pw_i[:t, :, None, :]))
    taps = (jnp.einsum("tgop,gpi->tgoi", cp[0], b_bar[0], precision=hi)
            - jnp.einsum("tgop,gpi->tgoi", cp[1], b_bar[1], precision=hi))
    nblk = g // SSM_GB
    width = SSM_GB * c
    eye = jnp.eye(SSM_GB, dtype=F32)
    tg = jnp.transpose(taps.reshape(t, nblk, SSM_GB, c, c), (1, 0, 2, 4, 3))
    bd = (tg[:, :, :, :, None, :] * eye[None, None, :, None, :, None]).reshape(nblk, t, width, width)
    place = eye.reshape(SSM_GB, SSM_GB // 2, 2)
    lanes = (SSM_GB // 2) * SSM_PAIR
    rev = t - 1 - jnp.arange(t)
    bfl = _cmul((pw_r[rev][:, :, :, None], pw_i[rev][:, :, :, None]), (b_bar[0][None], b_bar[1][None]))
    bv = jnp.transpose(jnp.stack(bfl).reshape(2, t, nblk, SSM_GB, p, c), (2, 1, 3, 5, 0, 4))
    bst = bv[:, :, :, :, None, :, None, :] * place[None, None, :, None, :, None, :, None]
    bst = bst.reshape(nblk, t, width, lanes)
    cfl = _cmul((c_re[None], c_im[None]), (pw_r[1:t + 1, :, None, :], pw_i[1:t + 1, :, None, :]))
    cv = jnp.stack([cfl[0], -cfl[1]]).reshape(2, t, nblk, SSM_GB, c, p)
    cv = jnp.transpose(cv, (2, 1, 0, 5, 3, 4))
    cst = cv[:, :, None, :, None, :, :, :] * jnp.transpose(place, (1, 2, 0))[None, None, :, None, :, None, :, None]
    cst = cst.reshape(nblk, t, lanes, width)
    ap = [pw[t]]
    for _ in range(levels - 1):
        ap.append(_cmul(ap[-1], ap[-1]))

    def pair_lanes(xs):
        v = jnp.stack(xs, axis=1).reshape(g // 2, 2, levels, p)
        return jnp.transpose(v, (0, 2, 1, 3)).reshape(g // 2, levels, 2 * p)

    apow = jnp.concatenate([pair_lanes([x[0] for x in ap]), pair_lanes([x[1] for x in ap])], axis=2)
    rows = -(-levels // 8) * 8
    apow = jnp.pad(apow, ((0, 0), (0, rows - levels), (0, 0)))
    return dict(bd=bd.astype(BF16), bst=bst.astype(BF16), cst=cst.astype(BF16), apow=_block_pairs(apow))


def _block_pairs(v):
    pairs, r, w = v.shape
    per = SSM_GB // 2
    return jnp.transpose(v.reshape(pairs // per, per, r, w), (0, 2, 1, 3)).reshape(pairs // per, r, per * w)


def _unblock_pairs(v):
    nblk, r, w = v.shape
    per = SSM_GB // 2
    return jnp.transpose(v.reshape(nblk, r, per, w // per), (0, 2, 1, 3)).reshape(nblk * per, r, w // per)


def _pair_states(re, im):
    b, g, p = re.shape
    r = jnp.transpose(re.reshape(b, g // 2, 2 * p), (1, 0, 2))
    i = jnp.transpose(im.reshape(b, g // 2, 2 * p), (1, 0, 2))
    return jnp.concatenate([r, i], axis=2)


def _unpair_states(h):
    pairs, b, w = h.shape
    p = w // 4
    r = jnp.transpose(h[:, :, :2 * p], (1, 0, 2)).reshape(b, pairs * 2, p)
    i = jnp.transpose(h[:, :, 2 * p:], (1, 0, 2)).reshape(b, pairs * 2, p)
    return r, i


def _layer_tables(i, ln1_g, ln1_b, ln2_g, ln2_b, w_rg, b_rg, w_re, b_re, w_gate, w_up, w_down, w_proj, w_pg):
    d = w_rg.shape[1]
    wr = jnp.concatenate([w_rg[i], jnp.transpose(w_re[i], (1, 0, 2)).reshape(d, N_EXPERTS)], axis=1)
    wr = jnp.pad(wr, ((0, 0), (0, ROUTER_LANES - wr.shape[1])))
    wr_hi = wr.astype(BF16)
    wr_lo = (wr - wr_hi.astype(F32)).astype(BF16)
    br = jnp.pad(jnp.concatenate([b_rg[i], b_re[i].reshape(-1)]), (0, ROUTER_LANES - MOE_GROUPS - N_EXPERTS))
    ff = w_gate.shape[-1]
    return dict(
        ln1_g=ln1_g[i].reshape(1, d), ln1_b=ln1_b[i].reshape(1, d), ln2_g=ln2_g[i].reshape(1, d),
        ln2_b=ln2_b[i].reshape(1, d), wr_hi=wr_hi, wr_lo=wr_lo, br=br.reshape(1, ROUTER_LANES),
        wg=w_gate[i].reshape(N_EXPERTS, d, ff).astype(BF16), wu=w_up[i].reshape(N_EXPERTS, d, ff).astype(BF16),
        wd=w_down[i].reshape(N_EXPERTS, ff, d).astype(BF16),
        w_proj=w_proj[i].astype(BF16), w_pg=w_pg[i].astype(BF16))


def kernel(x_prompt, x_sample, p_prompt, p_sample, cache_mla_ckv, cache_mla_kpe, state_ssm_re, state_ssm_im, mla_w_in, mla_q_norm, mla_kv_norm, mla_w_uq, mla_w_uk, mla_w_uv, mla_w_o, ssm_w_in, ssm_a_re, ssm_a_im, ssm_log_dt, ssm_b_re, ssm_b_im, ssm_c_re, ssm_c_im, ssm_d, ssm_w_glu, ln1_g, ln1_b, ln2_g, ln2_b, moe_w_rg, moe_b_rg, moe_w_re, moe_b_re, moe_w_gate, moe_w_up, moe_w_down, ple_w_proj, ple_w_gate):
    bp, n_prompt, d = x_prompt.shape
    nb, seq, _ = x_sample.shape
    past = cache_mla_ckv.shape[2]
    assert bp == 1 and seq == SSM_T and n_prompt % ROW_TILE == 0 and (nb * seq) % ROW_TILE == 0
    assert n_prompt % CHUNK == 0 and past % CHUNK == 0 and seq <= CHUNK
    n_samp = nb * seq
    n = n_prompt + n_samp
    x = jnp.concatenate([x_prompt.reshape(n_prompt, d), x_sample.reshape(n_samp, d)], axis=0)
    p_all = jnp.concatenate([p_prompt.reshape(DEPTH, n_prompt, -1), p_sample.reshape(DEPTH, n_samp, -1)], axis=1)
    layer_args = (ln1_g, ln1_b, ln2_g, ln2_b, moe_w_rg, moe_b_rg, moe_w_re, moe_b_re, moe_w_gate, moe_w_up,
                  moe_w_down, ple_w_proj, ple_w_gate)
    lw = _layer_tables(0, *layer_args)
    mw = _mla_tables(mla_w_in[0], mla_q_norm[0], mla_kv_norm[0], mla_w_uq[0], mla_w_uk[0], mla_w_uv[0], mla_w_o[0])
    pos = jnp.concatenate([jnp.arange(n_prompt, dtype=jnp.int32),
                           past + jnp.tile(jnp.arange(seq, dtype=jnp.int32), nb)])
    ct, st = _rope_tables(pos)
    q, k, v, ckv, kpe_slot = _mla_proj(x, ct, st, mw)
    o_prompt = _prompt_attention(q, k, v, n_prompt)
    o_sample = _sample_attention(q, cache_mla_ckv[0], cache_mla_kpe[0], ckv, kpe_slot, mw["wabs"], mw["wuv"],
                                 n_prompt)
    h, hb, route = _attn_out(o_prompt, o_sample, x, mw["wo"], lw)
    x = _ffn(h, hb, route, p_all[0], lw)
    kpe = kpe_slot[:, QK_NOPE:QK_NOPE + QK_ROPE]

    lw = _layer_tables(1, *layer_args)
    n_chunks = n_prompt // SSM_T
    levels = max(1, (n_chunks - 1).bit_length())
    tabs = _ssm_tables(ssm_a_re[0], ssm_a_im[0], ssm_log_dt[0], ssm_b_re[0], ssm_b_im[0], ssm_c_re[0],
                       ssm_c_im[0], levels)
    u = _ssm_in(x, ssm_w_in[0].astype(BF16))
    h0 = _block_pairs(_pair_states(state_ssm_re[0], state_ssm_im[0]))
    y, hp, hs = _ssm_core(u, tabs, h0, n_chunks)
    h, hb, route = _ssm_out(y, u, x, ssm_d[0].reshape(1, d), ssm_w_glu[0].astype(BF16), lw)
    x = _ffn(h, hb, route, p_all[1], lw)
    re_p, im_p = _unpair_states(_unblock_pairs(hp))
    re_s, im_s = _unpair_states(_unblock_pairs(hs))

    return (x[:n_prompt].reshape(1, n_prompt, d), x[n_prompt:].reshape(nb, seq, d),
            ckv[:n_prompt].reshape(1, 1, n_prompt, KV_LORA), kpe[:n_prompt].reshape(1, 1, n_prompt, QK_ROPE),
            re_p[None], im_p[None],
            ckv[n_prompt:].reshape(1, nb, seq, KV_LORA), kpe[n_prompt:].reshape(1, nb, seq, QK_ROPE),
            re_s[None], im_s[None])
```

```python
import functools
import math

import jax
import jax.numpy as jnp
import numpy as np
from jax import lax
from jax.experimental import pallas as pl
from jax.experimental.pallas import tpu as pltpu

F32 = jnp.float32
BF16 = jnp.bfloat16

N_HEADS = 16
Q_LORA = 256
KV_LORA = 128
QK_NOPE = 64
QK_ROPE = 32
V_DIM = 64
ROPE_BASE = 10000.0
ATTN_SCALE = (QK_NOPE + QK_ROPE) ** -0.5
CHUNK = 64
SSM_GROUP_CH = 16
SSM_STATE = 64
MOE_GROUPS = 4
MOE_EPG = 8
N_EXPERTS = MOE_GROUPS * MOE_EPG
DEPTH = 2
ALPHA = (2.0 * DEPTH) ** 0.25
LN_EPS = 1e-5
RMS_EPS = 1e-6
NEG = -1e30

LANES = 128
SLOT = LANES
ROW_TILE = 256
SSM_T = 16
VMEM_LIMIT = 56 * 1024 * 1024
ROUTER_LANES = LANES
EXPERT_LANE0 = MOE_GROUPS


def _cparams(sem):
    return pltpu.CompilerParams(dimension_semantics=sem, vmem_limit_bytes=VMEM_LIMIT)


def _dot(a, b):
    return jnp.dot(a, b, preferred_element_type=F32)


def _dot_nt(a, b):
    return lax.dot_general(a, b, (((1,), (1,)), ((), ())), preferred_element_type=F32)


def _full(shape):
    nd = len(shape)
    return pl.BlockSpec(shape, lambda *_: (0,) * nd)


def _rows(width, rows=ROW_TILE):
    return pl.BlockSpec((rows, width), lambda i: (i, 0))


def _layernorm(t, g, b):
    mu = jnp.mean(t, axis=-1, keepdims=True)
    d = t - mu
    var = jnp.mean(d * d, axis=-1, keepdims=True)
    return d * lax.rsqrt(var + LN_EPS) * g + b


def _rmsnorm(t, g):
    return t * lax.rsqrt(jnp.mean(t * t, axis=-1, keepdims=True) + RMS_EPS) * g


def _mla_proj_kernel(x_ref, ct_ref, st_ref, w_in_ref, qn_ref, kvn_ref, wqa_ref, wqb_ref, wk_ref, wv_ref,
                     vone_ref, q_ref, k_ref, v_ref, ckv_ref, kpe_ref):
    x = x_ref[...].astype(BF16)
    z = _dot(x, w_in_ref[...])
    cq = _rmsnorm(z[:, :Q_LORA], qn_ref[...])
    ckv = _rmsnorm(z[:, Q_LORA:Q_LORA + KV_LORA], kvn_ref[...])
    ct = ct_ref[...]
    st = st_ref[...]
    o = Q_LORA + KV_LORA
    kpe = z[:, o:o + SLOT] * ct + z[:, o + SLOT:o + 2 * SLOT] * st
    ckv_ref[...] = ckv
    kpe_ref[...] = kpe
    cqb = cq.astype(BF16)
    ckvb = ckv.astype(BF16)
    qa = _dot(cqb, wqa_ref[...])
    qb = _dot(cqb, wqb_ref[...])
    kn = _dot(ckvb, wk_ref[...])
    vv = _dot(ckvb, wv_ref[...]) + vone_ref[...]
    scale = ATTN_SCALE * math.log2(math.e)
    for h in range(N_HEADS):
        sl = slice(h * SLOT, (h + 1) * SLOT)
        q_ref[:, sl] = ((qa[:, sl] * ct + qb[:, sl] * st) * scale).astype(BF16)
        k_ref[:, sl] = (kn[:, sl] + kpe).astype(BF16)
    v_ref[...] = vv.astype(BF16)


def _mla_proj(x, ct, st, w):
    n = x.shape[0]
    wide = N_HEADS * SLOT
    return pl.pallas_call(
        _mla_proj_kernel,
        grid=(n // ROW_TILE,),
        in_specs=[_rows(x.shape[1]), _rows(SLOT), _rows(SLOT), _full(w["w_in"].shape), _full((1, Q_LORA)),
                  _full((1, KV_LORA)), _full(w["wqa"].shape), _full(w["wqb"].shape), _full(w["wk"].shape),
                  _full(w["wv"].shape), _full((1, wide))],
        out_specs=[_rows(wide), _rows(wide), _rows(wide), _rows(KV_LORA), _rows(SLOT)],
        out_shape=[jax.ShapeDtypeStruct((n, wide), BF16)] * 3
        + [jax.ShapeDtypeStruct((n, KV_LORA), F32), jax.ShapeDtypeStruct((n, SLOT), F32)],
        compiler_params=_cparams(("parallel",)),
        name="mla_proj",
    )(x, ct, st, w["w_in"], w["qn"], w["kvn"], w["wqa"], w["wqb"], w["wk"], w["wv"], w["vone"])


ATT_TQ = 1024
ATT_TK = 2048
ATT_G = 2


def _attn_kernel(q_ref, k_ref, v_ref, o_ref):
    qi = pl.program_id(1)
    slots = [slice(g * SLOT, (g + 1) * SLOT) for g in range(ATT_G)]
    ratio = ATT_TK // ATT_TQ

    def update(off, tk, state, mask):
        out = []
        for sl, (m, acc) in zip(slots, state):
            s = _dot_nt(q_ref[:, sl], k_ref[pl.ds(off, tk), sl])
            if mask is not None:
                s = jnp.where(mask, s, NEG)
            m_new = jnp.maximum(m, jnp.max(s, axis=-1, keepdims=True))
            p = jnp.exp2(s - m_new)
            pv = _dot(p.astype(BF16), v_ref[pl.ds(off, tk), sl])
            out.append((m_new, acc * jnp.exp2(m - m_new) + pv))
        return tuple(out)

    def body(j, state):
        return update(pl.multiple_of(j * ATT_TK, ATT_TK), ATT_TK, state, None)

    init = tuple((jnp.full((ATT_TQ, 1), NEG, F32), jnp.zeros((ATT_TQ, SLOT), F32)) for _ in range(ATT_G))
    state = lax.fori_loop(0, qi // ratio, body, init)

    def tail(rem):
        tk = (rem + 1) * ATT_TQ
        rc = lax.broadcasted_iota(jnp.int32, (ATT_TQ, tk), 0) // CHUNK
        cc = lax.broadcasted_iota(jnp.int32, (ATT_TQ, tk), 1) // CHUNK - rem * (ATT_TQ // CHUNK)
        return lambda st: update(pl.multiple_of((qi - rem) * ATT_TQ, ATT_TQ), tk, st, cc <= rc)

    state = lax.switch(qi % ratio, [tail(rem) for rem in range(ratio)], state)
    lane = lax.broadcasted_iota(jnp.int32, (ATT_TQ, SLOT), 1)
    for g in range(ATT_G):
        acc = state[g][1]
        l = jnp.sum(jnp.where(lane == V_DIM, acc, 0.0), axis=-1, keepdims=True)
        o_ref[:, g * SLOT:(g + 1) * SLOT] = (acc / l).astype(BF16)


def _prompt_attention(q, k, v, n_prompt):
    n = q.shape[0]
    wide = ATT_G * SLOT
    assert ATT_TK % ATT_TQ == 0 and ATT_TQ % CHUNK == 0 and n_prompt % ATT_TQ == 0 and N_HEADS % ATT_G == 0
    return pl.pallas_call(
        _attn_kernel,
        grid=(N_HEADS // ATT_G, n_prompt // ATT_TQ),
        in_specs=[pl.BlockSpec((ATT_TQ, wide), lambda h, i: (i, h)),
                  pl.BlockSpec((n_prompt, wide), lambda h, i: (0, h)),
                  pl.BlockSpec((n_prompt, wide), lambda h, i: (0, h))],
        out_specs=pl.BlockSpec((ATT_TQ, wide), lambda h, i: (i, h)),
        out_shape=jax.ShapeDtypeStruct((n_prompt, N_HEADS * SLOT), BF16),
        compiler_params=_cparams(("parallel", "arbitrary")),
        name="prompt_attention",
    )(q, k, v)


def _sample_attn_kernel(q_ref, ckvc_ref, kpec_ref, ckvn_ref, kpen_ref, wabs_ref, wuv_ref, o_ref):
    q = q_ref[...]
    seq = q.shape[0]
    qa, qp = [], []
    for h in range(N_HEADS):
        qh = q[:, h * SLOT:(h + 1) * SLOT]
        qa.append(_dot(qh, wabs_ref[h]))
        qp.append(qh[:, QK_NOPE:QK_NOPE + QK_ROPE])
    qa = jnp.concatenate(qa, axis=0).astype(BF16)
    qp = jnp.concatenate(qp, axis=0)
    ckvc = ckvc_ref[0].astype(BF16)
    kpec = kpec_ref[0].astype(BF16)
    ckvn = ckvn_ref[...].astype(BF16)
    kpen = kpen_ref[...][:, QK_NOPE:QK_NOPE + QK_ROPE].astype(BF16)
    s_c = _dot_nt(qa, ckvc) + _dot_nt(qp, kpec)
    s_n = _dot_nt(qa, ckvn) + _dot_nt(qp, kpen)
    m = jnp.maximum(jnp.max(s_c, axis=-1, keepdims=True), jnp.max(s_n, axis=-1, keepdims=True))
    p_c = jnp.exp2(s_c - m)
    p_n = jnp.exp2(s_n - m)
    l = jnp.sum(p_c, axis=-1, keepdims=True) + jnp.sum(p_n, axis=-1, keepdims=True)
    ol = (_dot(p_c.astype(BF16), ckvc) + _dot(p_n.astype(BF16), ckvn)) / l
    olb = ol.astype(BF16)
    for h in range(N_HEADS):
        o_ref[:, h * SLOT:(h + 1) * SLOT] = _dot(olb[h * seq:(h + 1) * seq], wuv_ref[h]).astype(BF16)


def _sample_attention(q, cache_ckv, cache_kpe, ckv_new, kpe_new, wabs, wuv, n_prompt):
    nb, past, _ = cache_ckv.shape
    seq = (q.shape[0] - n_prompt) // nb
    base = n_prompt // seq
    wide = N_HEADS * SLOT
    return pl.pallas_call(
        _sample_attn_kernel,
        grid=(nb,),
        in_specs=[pl.BlockSpec((seq, wide), lambda b: (base + b, 0)),
                  pl.BlockSpec((1, past, KV_LORA), lambda b: (b, 0, 0)),
                  pl.BlockSpec((1, past, QK_ROPE), lambda b: (b, 0, 0)),
                  pl.BlockSpec((seq, KV_LORA), lambda b: (base + b, 0)),
                  pl.BlockSpec((seq, SLOT), lambda b: (base + b, 0)),
                  _full(wabs.shape), _full(wuv.shape)],
        out_specs=pl.BlockSpec((seq, wide), lambda b: (b, 0)),
        out_shape=jax.ShapeDtypeStruct((nb * seq, wide), BF16),
        compiler_params=_cparams(("parallel",)),
        name="sample_attention",
    )(q, cache_ckv, cache_kpe, ckv_new, kpe_new, wabs, wuv)


ROUTE_E1, ROUTE_E2, ROUTE_W1, ROUTE_W2 = range(4)


def _lane_col(tile, k):
    lane = lax.broadcasted_iota(jnp.int32, tile.shape, 1)
    return jnp.sum(jnp.where(lane == k, tile, 0.0), axis=-1, keepdims=True)


def _route(lg):
    lane = lax.broadcasted_iota(jnp.int32, lg.shape, 1)
    lanef = lane.astype(F32)
    big = float(ROUTER_LANES)
    is_g = lane < MOE_GROUPS
    gl = jnp.where(is_g, lg, NEG)
    gmax = jnp.max(gl, axis=-1, keepdims=True)
    gsel = jnp.min(jnp.where(gl == gmax, lanef, big), axis=-1, keepdims=True)
    gate_g = 1.0 / jnp.sum(jnp.where(is_g, jnp.exp(gl - gmax), 0.0), axis=-1, keepdims=True)
    lo = EXPERT_LANE0 + MOE_EPG * gsel
    el = jnp.where(jnp.logical_and(lanef >= lo, lanef < lo + MOE_EPG), lg, NEG)
    m1 = jnp.max(el, axis=-1, keepdims=True)
    i1 = jnp.min(jnp.where(el == m1, lanef, big), axis=-1, keepdims=True)
    el2 = jnp.where(lanef == i1, NEG, el)
    m2 = jnp.max(el2, axis=-1, keepdims=True)
    i2 = jnp.min(jnp.where(el2 == m2, lanef, big), axis=-1, keepdims=True)
    r = jnp.exp(m2 - m1)
    w1 = 1.0 / (1.0 + r)
    w2 = r / (1.0 + r)
    cols = {ROUTE_E1: i1 - EXPERT_LANE0, ROUTE_E2: i2 - EXPERT_LANE0, ROUTE_W1: gate_g * w1, ROUTE_W2: gate_g * w2}
    route = jnp.zeros(lg.shape, F32)
    for k, c in cols.items():
        route = jnp.where(lane == k, c, route)
    return route


def _post_mix(t, x, g, b, wr_hi, wr_lo, br, h_ref, hb_ref, route_ref):
    h = _layernorm(ALPHA * x + t, g, b)
    h_ref[...] = h
    hb = h.astype(BF16)
    hb_ref[...] = hb
    h_lo = (h - hb.astype(F32)).astype(BF16)
    lg = _dot(hb, wr_hi) + (_dot(hb, wr_lo) + _dot(h_lo, wr_hi)) + br
    route_ref[...] = _route(lg)


def _attn_out_kernel(op_ref, os_ref, x_ref, wo_ref, g_ref, b_ref, wrh_ref, wrl_ref, br_ref, h_ref, hb_ref,
                     route_ref, *, prompt_tiles):
    o = jnp.where(pl.program_id(0) < prompt_tiles, op_ref[...], os_ref[...])
    t = _dot(o, wo_ref[...])
    _post_mix(t, x_ref[...], g_ref[...], b_ref[...], wrh_ref[...], wrl_ref[...], br_ref[...],
              h_ref, hb_ref, route_ref)


def _post_out(n, d):
    specs = [_rows(d), _rows(d), _rows(ROUTER_LANES)]
    shapes = [jax.ShapeDtypeStruct((n, d), F32), jax.ShapeDtypeStruct((n, d), BF16),
              jax.ShapeDtypeStruct((n, ROUTER_LANES), F32)]
    return specs, shapes


def _attn_out(o_prompt, o_sample, x, wo, lw):
    n, d = x.shape
    wide = o_prompt.shape[1]
    pt = o_prompt.shape[0] // ROW_TILE
    specs, shapes = _post_out(n, d)
    return pl.pallas_call(
        functools.partial(_attn_out_kernel, prompt_tiles=pt),
        grid=(n // ROW_TILE,),
        in_specs=[pl.BlockSpec((ROW_TILE, wide), lambda i: (jnp.minimum(i, pt - 1), 0)),
                  pl.BlockSpec((ROW_TILE, wide), lambda i: (jnp.maximum(i - pt, 0), 0)),
                  _rows(d), _full(wo.shape), _full((1, d)), _full((1, d)),
                  _full(lw["wr_hi"].shape), _full(lw["wr_lo"].shape), _full((1, ROUTER_LANES))],
        out_specs=specs, out_shape=shapes,
        compiler_params=_cparams(("parallel",)),
        name="attn_out_ln_router",
    )(o_prompt, o_sample, x, wo, lw["ln1_g"], lw["ln1_b"], lw["wr_hi"], lw["wr_lo"], lw["br"])


MOE_EPS = 4


def _moe_dense_kernel(hb_ref, route_ref, wg_ref, wu_ref, wd_ref, y_ref):
    x = hb_ref[...]
    route = route_ref[...]
    e1 = _lane_col(route, ROUTE_E1)
    e2 = _lane_col(route, ROUTE_E2)
    w1 = _lane_col(route, ROUTE_W1)
    w2 = _lane_col(route, ROUTE_W2)
    y = None
    for k in range(MOE_EPS):
        e = (pl.program_id(1) * MOE_EPS + k).astype(F32)
        g = _dot(x, wg_ref[k])
        u = _dot(x, wu_ref[k])
        gate = jnp.where(e1 == e, w1, 0.0) + jnp.where(e2 == e, w2, 0.0)
        hdn = (g * jax.nn.sigmoid(g)) * u * gate
        yk = _dot(hdn.astype(BF16), wd_ref[k])
        y = yk if y is None else y + yk

    @pl.when(pl.program_id(1) == 0)
    def _():
        y_ref[...] = y

    @pl.when(pl.program_id(1) > 0)
    def _():
        y_ref[...] += y


def _moe_tile(n):
    return max(t for t in range(LANES, 2048 + 1, LANES) if n % t == 0)


def _moe_dense(hb, route, wg, wu, wd):
    n, d = hb.shape
    ff = wg.shape[-1]
    tile = _moe_tile(n)
    return pl.pallas_call(
        _moe_dense_kernel,
        grid=(n // tile, N_EXPERTS // MOE_EPS),
        in_specs=[pl.BlockSpec((tile, d), lambda i, e: (i, 0)),
                  pl.BlockSpec((tile, ROUTER_LANES), lambda i, e: (i, 0)),
                  pl.BlockSpec((MOE_EPS, d, ff), lambda i, e: (e, 0, 0)),
                  pl.BlockSpec((MOE_EPS, d, ff), lambda i, e: (e, 0, 0)),
                  pl.BlockSpec((MOE_EPS, ff, d), lambda i, e: (e, 0, 0))],
        out_specs=pl.BlockSpec((tile, d), lambda i, e: (i, 0)),
        out_shape=jax.ShapeDtypeStruct((n, d), F32),
        compiler_params=_cparams(("parallel", "arbitrary")),
        name="moe_dense",
    )(hb, route, wg, wu, wd)


def _ln2_ple_kernel(h_ref, y_ref, p_ref, g_ref, b_ref, wp_ref, wpg_ref, x_ref):
    h2 = _layernorm(ALPHA * h_ref[...] + y_ref[...], g_ref[...], b_ref[...])
    proj = _dot(p_ref[...].astype(BF16), wp_ref[...])
    gate = jax.nn.sigmoid(_dot(h2.astype(BF16), wpg_ref[...]))
    x_ref[...] = h2 + proj * gate


def _ln2_ple(h, y, p, lw):
    n, d = h.shape
    return pl.pallas_call(
        _ln2_ple_kernel,
        grid=(n // ROW_TILE,),
        in_specs=[_rows(d), _rows(d), _rows(p.shape[1]), _full((1, d)), _full((1, d)),
                  _full(lw["w_proj"].shape), _full(lw["w_pg"].shape)],
        out_specs=_rows(d),
        out_shape=jax.ShapeDtypeStruct((n, d), F32),
        compiler_params=_cparams(("parallel",)),
        name="ln2_ple",
    )(h, y, p, lw["ln2_g"], lw["ln2_b"], lw["w_proj"], lw["w_pg"])


def _ffn(h, hb, route, p, lw):
    y = _moe_dense(hb, route, lw["wg"], lw["wu"], lw["wd"])
    return _ln2_ple(h, y, p, lw)


def _ssm_in_kernel(x_ref, w_ref, u_ref):
    u_ref[...] = _dot(x_ref[...].astype(BF16), w_ref[...])


def _ssm_in(x, w):
    n, d = x.shape
    return pl.pallas_call(
        _ssm_in_kernel,
        grid=(n // ROW_TILE,),
        in_specs=[_rows(d), _full(w.shape)],
        out_specs=_rows(w.shape[1]),
        out_shape=jax.ShapeDtypeStruct((n, w.shape[1]), F32),
        compiler_params=_cparams(("parallel",)),
        name="ssm_in",
    )(x, w)


SSM_GB = LANES // SSM_GROUP_CH
SSM_PAIR = 2 * LANES


def _ssm_core_kernel(u_ref, bd_ref, bst_ref, cst_ref, apow_ref, h0_ref, y_ref, hp_ref, hs_ref, sre_ref, sim_ref,
                     *, n_chunks, levels):
    t = SSM_T
    rows = u_ref.shape[0] // t
    pad = sre_ref.shape[0] - n_chunks
    npair = SSM_GB // 2
    ut = [u_ref[pl.ds(tl, rows, stride=t), :].astype(BF16) for tl in range(t)]
    u2 = [jnp.concatenate([ut[2 * k], ut[2 * k + 1]], axis=1) for k in range(t // 2)]
    bb = None
    for k in range(t // 2):
        part = _dot(u2[k], jnp.concatenate([bst_ref[0, 2 * k], bst_ref[0, 2 * k + 1]], axis=0))
        bb = part if bb is None else bb + part
    zeros = jnp.zeros((pad, LANES), F32)
    sre_ref[:pad, :] = zeros
    sim_ref[:pad, :] = zeros

    def shifted(d):
        return sre_ref[pl.ds(pad - d, n_chunks), :], sim_ref[pl.ds(pad - d, n_chunks), :]

    h0 = h0_ref[0]
    hprev, h_last, h_samp = [], [], []
    for j in range(npair):
        lo = j * SSM_PAIR
        re = bb[:n_chunks, lo:lo + LANES]
        im = bb[:n_chunks, lo + LANES:lo + SSM_PAIR]
        for lv in range(levels):
            sre_ref[pad:, :] = re
            sim_ref[pad:, :] = im
            pr, pi = shifted(1 << lv)
            ar = apow_ref[0, lv:lv + 1, lo:lo + LANES]
            ai = apow_ref[0, lv:lv + 1, lo + LANES:lo + SSM_PAIR]
            re, im = re + ar * pr - ai * pi, im + ar * pi + ai * pr
        sre_ref[pad:, :] = re
        sim_ref[pad:, :] = im
        pr, pi = shifted(1)
        h0j = h0[:, lo:lo + SSM_PAIR]
        hprev.append(jnp.concatenate([jnp.concatenate([pr, pi], axis=1), h0j], axis=0).astype(BF16))
        h_last.append(jnp.concatenate([re[n_chunks - 1:, :], im[n_chunks - 1:, :]], axis=1))
        ar = apow_ref[0, 0:1, lo:lo + LANES]
        ai = apow_ref[0, 0:1, lo + LANES:lo + SSM_PAIR]
        h0r = h0j[:, :LANES]
        h0i = h0j[:, LANES:]
        h_samp.append(jnp.concatenate([ar * h0r - ai * h0i + bb[n_chunks:, lo:lo + LANES],
                                       ar * h0i + ai * h0r + bb[n_chunks:, lo + LANES:lo + SSM_PAIR]], axis=1))
    hp_ref[0] = jnp.concatenate(h_last, axis=1)
    hs_ref[0] = jnp.concatenate(h_samp, axis=1)
    hprev = jnp.concatenate(hprev, axis=1)
    zero_tap = jnp.zeros((LANES, LANES), BF16)

    def tap(lag):
        return bd_ref[0, lag] if lag >= 0 else zero_tap

    for ko in range(t // 2):
        acc = _dot_nt(hprev, jnp.concatenate([cst_ref[0, 2 * ko], cst_ref[0, 2 * ko + 1]], axis=0))
        for ki in range(ko + 1):
            lag = 2 * (ko - ki)
            w = jnp.concatenate([jnp.concatenate([tap(lag), tap(lag + 1)], axis=1),
                                 jnp.concatenate([tap(lag - 1), tap(lag)], axis=1)], axis=0)
            acc = acc + _dot(u2[ki], w)
        y_ref[pl.ds(2 * ko, rows, stride=t), :] = acc[:, :LANES]
        y_ref[pl.ds(2 * ko + 1, rows, stride=t), :] = acc[:, LANES:]


def _ssm_core(u, tabs, h0, n_chunks):
    n, d = u.shape
    nblk = d // LANES
    ns = n // SSM_T - n_chunks
    levels = max(1, (n_chunks - 1).bit_length())
    pad = max(1 << (levels - 1), 8)
    wide = (SSM_GB // 2) * SSM_PAIR
    kern = functools.partial(_ssm_core_kernel, n_chunks=n_chunks, levels=levels)
    once = dict(pipeline_mode=pl.Buffered(1))

    def blk(*shape):
        return pl.BlockSpec((1,) + shape, lambda i: (i,) + (0,) * len(shape), **once)

    return pl.pallas_call(
        kern,
        grid=(nblk,),
        in_specs=[pl.BlockSpec((n, LANES), lambda i: (0, i), **once), blk(SSM_T, LANES, LANES),
                  blk(SSM_T, LANES, wide), blk(SSM_T, LANES, wide), blk(tabs["apow"].shape[1], wide), blk(ns, wide)],
        out_specs=[pl.BlockSpec((n, LANES), lambda i: (0, i)),
                   pl.BlockSpec((1, 1, wide), lambda i: (i, 0, 0)), pl.BlockSpec((1, ns, wide), lambda i: (i, 0, 0))],
        out_shape=[jax.ShapeDtypeStruct((n, d), F32), jax.ShapeDtypeStruct((nblk, 1, wide), F32),
                   jax.ShapeDtypeStruct((nblk, ns, wide), F32)],
        scratch_shapes=[pltpu.VMEM((pad + n_chunks, LANES), F32), pltpu.VMEM((pad + n_chunks, LANES), F32)],
        compiler_params=_cparams(("parallel",)),
        name="ssm_core",
    )(u, tabs["bd"], tabs["bst"], tabs["cst"], tabs["apow"], h0)


def _gelu_tanh(y):
    c = math.sqrt(2.0 / math.pi)
    return 0.5 * y * (1.0 + jnp.tanh(c * (y + 0.044715 * (y * y * y))))


def _ssm_out_kernel(y_ref, u_ref, x_ref, d_ref, wglu_ref, g_ref, b_ref, wrh_ref, wrl_ref, br_ref,
                    h_ref, hb_ref, route_ref):
    y = y_ref[...] + d_ref[...] * u_ref[...]
    z = _dot(_gelu_tanh(y).astype(BF16), wglu_ref[...])
    dm = z.shape[1] // 2
    t = z[:, :dm] * jax.nn.sigmoid(z[:, dm:])
    _post_mix(t, x_ref[...], g_ref[...], b_ref[...], wrh_ref[...], wrl_ref[...], br_ref[...],
              h_ref, hb_ref, route_ref)


def _ssm_out(y, u, x, d_skip, wglu, lw):
    n, d = x.shape
    specs, shapes = _post_out(n, d)
    return pl.pallas_call(
        _ssm_out_kernel,
        grid=(n // ROW_TILE,),
        in_specs=[_rows(d), _rows(d), _rows(d), _full((1, d)), _full(wglu.shape), _full((1, d)), _full((1, d)),
                  _full(lw["wr_hi"].shape), _full(lw["wr_lo"].shape), _full((1, ROUTER_LANES))],
        out_specs=specs, out_shape=shapes,
        compiler_params=_cparams(("parallel",)),
        name="ssm_out_ln_router",
    )(y, u, x, d_skip, wglu, lw["ln1_g"], lw["ln1_b"], lw["wr_hi"], lw["wr_lo"], lw["br"])


def _rope_tables(pos):
    half = QK_ROPE // 2
    inv = ROPE_BASE ** (-jnp.arange(half, dtype=F32) / half)
    ang = pos.astype(F32)[:, None] * inv[None, :]
    cos, sin = jnp.cos(ang), jnp.sin(ang)
    n = pos.shape[0]
    ones = jnp.ones((n, QK_NOPE), F32)
    zeros = jnp.zeros((n, SLOT - QK_NOPE - QK_ROPE), F32)
    ct = jnp.concatenate([ones, cos, cos, zeros], axis=1)
    st = jnp.concatenate([jnp.zeros((n, QK_NOPE), F32), -sin, sin, zeros], axis=1)
    return ct, st


def _slot_cols(w, width):
    k = w.shape[0]
    return jnp.pad(w, ((0, 0), (0, 0), (0, SLOT - width))).reshape(k, N_HEADS * SLOT)


def _mla_tables(w_in, q_norm, kv_norm, w_uq, w_uk, w_uv, w_o):
    half = QK_ROPE // 2
    o = Q_LORA + KV_LORA
    d = w_in.shape[0]
    kpe_w = w_in[:, o:]
    kpe_sw = jnp.concatenate([kpe_w[:, half:], kpe_w[:, :half]], axis=1)
    zl = jnp.zeros((d, QK_NOPE), F32)
    zr = jnp.zeros((d, SLOT - QK_NOPE - QK_ROPE), F32)
    w_in_e = jnp.concatenate([w_in[:, :o], zl, kpe_w, zr, zl, kpe_sw, zr], axis=1)
    wq = w_uq.reshape(Q_LORA, N_HEADS, QK_NOPE + QK_ROPE)
    pe = wq[:, :, QK_NOPE:]
    pe_sw = jnp.concatenate([pe[:, :, half:], pe[:, :, :half]], axis=2)
    wqb = jnp.concatenate([jnp.zeros_like(wq[:, :, :QK_NOPE]), pe_sw], axis=2)
    vone = jnp.zeros((N_HEADS, SLOT), F32).at[:, V_DIM].set(1.0).reshape(1, N_HEADS * SLOT)
    wabs = jnp.pad(jnp.transpose(w_uk, (1, 2, 0)), ((0, 0), (0, SLOT - QK_NOPE), (0, 0)))
    wuv = jnp.pad(jnp.transpose(w_uv, (1, 0, 2)), ((0, 0), (0, 0), (0, SLOT - V_DIM)))
    wo = jnp.pad(w_o.reshape(N_HEADS, V_DIM, -1), ((0, 0), (0, SLOT - V_DIM), (0, 0)))
    return dict(
        w_in=w_in_e.astype(BF16), qn=q_norm.reshape(1, -1), kvn=kv_norm.reshape(1, -1),
        wqa=_slot_cols(wq, QK_NOPE + QK_ROPE).astype(BF16), wqb=_slot_cols(wqb, QK_NOPE + QK_ROPE).astype(BF16),
        wk=_slot_cols(w_uk, QK_NOPE).astype(BF16), wv=_slot_cols(w_uv, V_DIM).astype(BF16), vone=vone,
        wabs=wabs.astype(BF16), wuv=wuv.astype(BF16), wo=wo.reshape(N_HEADS * SLOT, -1).astype(BF16))


def _cmul(a, b):
    return a[0] * b[0] - a[1] * b[1], a[0] * b[1] + a[1] * b[0]


def _ssm_tables(a_re, a_im, log_dt, b_re, b_im, c_re, c_im, levels):
    t = SSM_T
    g, p = a_re.shape
    c = SSM_GROUP_CH
    hi = lax.Precision.HIGHEST
    dt = jnp.exp(log_dt)[:, None]
    mag = jnp.exp(a_re * dt)
    lam_bar = (mag * jnp.cos(a_im * dt), mag * jnp.sin(a_im * dt))
    den = a_re * a_re + a_im * a_im
    quo = (((lam_bar[0] - 1.0) * a_re + lam_bar[1] * a_im) / den,
           (lam_bar[1] * a_re - (lam_bar[0] - 1.0) * a_im) / den)
    b_bar = _cmul((quo[0][:, :, None], quo[1][:, :, None]), (b_re, b_im))
    pw = [(jnp.ones_like(a_re), jnp.zeros_like(a_re))]
    for _ in range(t):
        pw.append(_cmul(pw[-1], lam_bar))
    pw_r = jnp.stack([x[0] for x in pw])
    pw_i = jnp.stack([x[1] for x in pw])
    cp = _cmul((c_re[None], c_im[None]), (pw_r[:t, :, None, :], pw_i[:t, :, None, :]))
    taps = (jnp.einsum("tgop,gpi->tgoi", cp[0], b_bar[0], precision=hi)
            - jnp.einsum("tgop,gpi->tgoi", cp[1], b_bar[1], precision=hi))
    nblk = g // SSM_GB
    width = SSM_GB * c
    tg = jnp.transpose(taps.reshape(t, nblk, SSM_GB, c, c), (1, 0, 2, 4, 3))
    own = np.zeros((SSM_GB, c, SSM_GB, c), np.float32)
    for g8 in range(SSM_GB):
        own[g8, np.arange(c), g8, np.arange(c)] = 1.0
    own = jnp.asarray(own.reshape(SSM_GB, c, width), BF16)
    bd = jnp.einsum("btgio,gol->btgil", tg.astype(BF16), own, preferred_element_type=BF16)
    bd = bd.reshape(nblk, t, width, width)
    lanes = (SSM_GB // 2) * SSM_PAIR
    place = np.zeros((SSM_GB, 2, p, SSM_GB // 2, 2, 2, p), np.float32)
    for g8 in range(SSM_GB):
        for ri in range(2):
            place[g8, ri, np.arange(p), g8 // 2, ri, g8 % 2, np.arange(p)] = 1.0
    place = jnp.asarray(place.reshape(SSM_GB, 2 * p, lanes), BF16)

    def spread(v):
        v = jnp.transpose(v, (2, 1, 3, 5, 0, 4)).reshape(nblk, t, SSM_GB, c, 2 * p)
        out = jnp.einsum("btgck,gkl->btgcl", v.astype(BF16), place, preferred_element_type=BF16)
        return out.reshape(nblk, t, width, lanes)

    rev = t - 1 - jnp.arange(t)
    bfl = _cmul((pw_r[rev][:, :, :, None], pw_i[rev][:, :, :, None]), (b_bar[0][None], b_bar[1][None]))
    bst = spread(jnp.stack(bfl).reshape(2, t, nblk, SSM_GB, p, c))
    cfl = _cmul((c_re[None], c_im[None]), (pw_r[1:t + 1, :, None, :], pw_i[1:t + 1, :, None, :]))
    cv = jnp.stack([cfl[0], -cfl[1]]).reshape(2, t, nblk, SSM_GB, c, p)
    cst = spread(jnp.swapaxes(cv, 4, 5))
    ap = [pw[t]]
    for _ in range(levels - 1):
        ap.append(_cmul(ap[-1], ap[-1]))

    def pair_lanes(xs):
        v = jnp.stack(xs, axis=1).reshape(g // 2, 2, levels, p)
        return jnp.transpose(v, (0, 2, 1, 3)).reshape(g // 2, levels, 2 * p)

    apow = jnp.concatenate([pair_lanes([x[0] for x in ap]), pair_lanes([x[1] for x in ap])], axis=2)
    rows = -(-levels // 8) * 8
    apow = jnp.pad(apow, ((0, 0), (0, rows - levels), (0, 0)))
    return dict(bd=bd.astype(BF16), bst=bst.astype(BF16), cst=cst.astype(BF16), apow=_block_pairs(apow))


def _block_pairs(v):
    pairs, r, w = v.shape
    per = SSM_GB // 2
    return jnp.transpose(v.reshape(pairs // per, per, r, w), (0, 2, 1, 3)).reshape(pairs // per, r, per * w)


def _unblock_pairs(v):
    nblk, r, w = v.shape
    per = SSM_GB // 2
    return jnp.transpose(v.reshape(nblk, r, per, w // per), (0, 2, 1, 3)).reshape(nblk * per, r, w // per)


def _pair_states(re, im):
    b, g, p = re.shape
    r = jnp.transpose(re.reshape(b, g // 2, 2 * p), (1, 0, 2))
    i = jnp.transpose(im.reshape(b, g // 2, 2 * p), (1, 0, 2))
    return jnp.concatenate([r, i], axis=2)


def _unpair_states(h):
    pairs, b, w = h.shape
    p = w // 4
    r = jnp.transpose(h[:, :, :2 * p], (1, 0, 2)).reshape(b, pairs * 2, p)
    i = jnp.transpose(h[:, :, 2 * p:], (1, 0, 2)).reshape(b, pairs * 2, p)
    return r, i


def _layer_tables(i, ln1_g, ln1_b, ln2_g, ln2_b, w_rg, b_rg, w_re, b_re, w_gate, w_up, w_down, w_proj, w_pg):
    d = w_rg.shape[1]
    wr = jnp.concatenate([w_rg[i], jnp.transpose(w_re[i], (1, 0, 2)).reshape(d, N_EXPERTS)], axis=1)
    wr = jnp.pad(wr, ((0, 0), (0, ROUTER_LANES - wr.shape[1])))
    wr_hi = wr.astype(BF16)
    wr_lo = (wr - wr_hi.astype(F32)).astype(BF16)
    br = jnp.pad(jnp.concatenate([b_rg[i], b_re[i].reshape(-1)]), (0, ROUTER_LANES - MOE_GROUPS - N_EXPERTS))
    ff = w_gate.shape[-1]
    return dict(
        ln1_g=ln1_g[i].reshape(1, d), ln1_b=ln1_b[i].reshape(1, d), ln2_g=ln2_g[i].reshape(1, d),
        ln2_b=ln2_b[i].reshape(1, d), wr_hi=wr_hi, wr_lo=wr_lo, br=br.reshape(1, ROUTER_LANES),
        wg=w_gate[i].reshape(N_EXPERTS, d, ff).astype(BF16), wu=w_up[i].reshape(N_EXPERTS, d, ff).astype(BF16),
        wd=w_down[i].reshape(N_EXPERTS, ff, d).astype(BF16),
        w_proj=w_proj[i].astype(BF16), w_pg=w_pg[i].astype(BF16))


def kernel(x_prompt, x_sample, p_prompt, p_sample, cache_mla_ckv, cache_mla_kpe, state_ssm_re, state_ssm_im, mla_w_in, mla_q_norm, mla_kv_norm, mla_w_uq, mla_w_uk, mla_w_uv, mla_w_o, ssm_w_in, ssm_a_re, ssm_a_im, ssm_log_dt, ssm_b_re, ssm_b_im, ssm_c_re, ssm_c_im, ssm_d, ssm_w_glu, ln1_g, ln1_b, ln2_g, ln2_b, moe_w_rg, moe_b_rg, moe_w_re, moe_b_re, moe_w_gate, moe_w_up, moe_w_down, ple_w_proj, ple_w_gate):
    bp, n_prompt, d = x_prompt.shape
    nb, seq, _ = x_sample.shape
    past = cache_mla_ckv.shape[2]
    assert bp == 1 and seq == SSM_T and n_prompt % ROW_TILE == 0 and (nb * seq) % ROW_TILE == 0
    assert n_prompt % CHUNK == 0 and past % CHUNK == 0 and seq <= CHUNK
    n_samp = nb * seq
    n = n_prompt + n_samp
    x = jnp.concatenate([x_prompt.reshape(n_prompt, d), x_sample.reshape(n_samp, d)], axis=0)
    p_all = jnp.concatenate([p_prompt.reshape(DEPTH, n_prompt, -1), p_sample.reshape(DEPTH, n_samp, -1)], axis=1)
    layer_args = (ln1_g, ln1_b, ln2_g, ln2_b, moe_w_rg, moe_b_rg, moe_w_re, moe_b_re, moe_w_gate, moe_w_up,
                  moe_w_down, ple_w_proj, ple_w_gate)
    lw = _layer_tables(0, *layer_args)
    mw = _mla_tables(mla_w_in[0], mla_q_norm[0], mla_kv_norm[0], mla_w_uq[0], mla_w_uk[0], mla_w_uv[0], mla_w_o[0])
    pos = jnp.concatenate([jnp.arange(n_prompt, dtype=jnp.int32),
                           past + jnp.tile(jnp.arange(seq, dtype=jnp.int32), nb)])
    ct, st = _rope_tables(pos)
    q, k, v, ckv, kpe_slot = _mla_proj(x, ct, st, mw)
    o_prompt = _prompt_attention(q, k, v, n_prompt)
    o_sample = _sample_attention(q, cache_mla_ckv[0], cache_mla_kpe[0], ckv, kpe_slot, mw["wabs"], mw["wuv"],
                                 n_prompt)
    h, hb, route = _attn_out(o_prompt, o_sample, x, mw["wo"], lw)
    x = _ffn(h, hb, route, p_all[0], lw)
    kpe = kpe_slot[:, QK_NOPE:QK_NOPE + QK_ROPE]

    lw = _layer_tables(1, *layer_args)
    n_chunks = n_prompt // SSM_T
    levels = max(1, (n_chunks - 1).bit_length())
    tabs = _ssm_tables(ssm_a_re[0], ssm_a_im[0], ssm_log_dt[0], ssm_b_re[0], ssm_b_im[0], ssm_c_re[0],
                       ssm_c_im[0], levels)
    u = _ssm_in(x, ssm_w_in[0].astype(BF16))
    h0 = _block_pairs(_pair_states(state_ssm_re[0], state_ssm_im[0]))
    y, hp, hs = _ssm_core(u, tabs, h0, n_chunks)
    h, hb, route = _ssm_out(y, u, x, ssm_d[0].reshape(1, d), ssm_w_glu[0].astype(BF16), lw)
    x = _ffn(h, hb, route, p_all[1], lw)
    re_p, im_p = _unpair_states(_unblock_pairs(hp))
    re_s, im_s = _unpair_states(_unblock_pairs(hs))

    return (x[:n_prompt].reshape(1, n_prompt, d), x[n_prompt:].reshape(nb, seq, d),
            ckv[:n_prompt].reshape(1, 1, n_prompt, KV_LORA), kpe[:n_prompt].reshape(1, 1, n_prompt, QK_ROPE),
            re_p[None], im_p[None],
            ckv[n_prompt:].reshape(1, nb, seq, KV_LORA), kpe[n_prompt:].reshape(1, nb, seq, QK_ROPE),
            re_s[None], im_s[None])
```

```python
import functools
import math

import jax
import jax.numpy as jnp
import numpy as np
from jax import lax
from jax.experimental import pallas as pl
from jax.experimental.pallas import tpu as pltpu

F32 = jnp.float32
BF16 = jnp.bfloat16

N_HEADS = 16
Q_LORA = 256
KV_LORA = 128
QK_NOPE = 64
QK_ROPE = 32
V_DIM = 64
ROPE_BASE = 10000.0
ATTN_SCALE = (QK_NOPE + QK_ROPE) ** -0.5
CHUNK = 64
SSM_GROUP_CH = 16
SSM_STATE = 64
MOE_GROUPS = 4
MOE_EPG = 8
N_EXPERTS = MOE_GROUPS * MOE_EPG
DEPTH = 2
ALPHA = (2.0 * DEPTH) ** 0.25
LN_EPS = 1e-5
RMS_EPS = 1e-6
NEG = -1e30

LANES = 128
SLOT = LANES
ROW_TILE = 256
SSM_T = 16
VMEM_LIMIT = 56 * 1024 * 1024
ROUTER_LANES = LANES
EXPERT_LANE0 = MOE_GROUPS


def _cparams(sem):
    return pltpu.CompilerParams(dimension_semantics=sem, vmem_limit_bytes=VMEM_LIMIT)


def _dot(a, b):
    return jnp.dot(a, b, preferred_element_type=F32)


def _dot_nt(a, b):
    return lax.dot_general(a, b, (((1,), (1,)), ((), ())), preferred_element_type=F32)


def _full(shape):
    nd = len(shape)
    return pl.BlockSpec(shape, lambda *_: (0,) * nd)


def _rows(width, rows=ROW_TILE):
    return pl.BlockSpec((rows, width), lambda i: (i, 0))


def _split_rows(width, prompt_tiles):
    return [pl.BlockSpec((ROW_TILE, width), lambda i: (jnp.minimum(i, prompt_tiles - 1), 0)),
            pl.BlockSpec((ROW_TILE, width), lambda i: (jnp.maximum(i - prompt_tiles, 0), 0))]


def _pick_rows(prompt_ref, sample_ref, prompt_tiles):
    return jnp.where(pl.program_id(0) < prompt_tiles, prompt_ref[...], sample_ref[...])


def _layernorm(t, g, b):
    mu = jnp.mean(t, axis=-1, keepdims=True)
    d = t - mu
    var = jnp.mean(d * d, axis=-1, keepdims=True)
    return d * lax.rsqrt(var + LN_EPS) * g + b


def _rmsnorm(t, g):
    return t * lax.rsqrt(jnp.mean(t * t, axis=-1, keepdims=True) + RMS_EPS) * g


def _mla_proj_kernel(xp_ref, xs_ref, ct_ref, st_ref, w_in_ref, qn_ref, kvn_ref, wqa_ref, wqb_ref, wk_ref, wv_ref,
                     vone_ref, q_ref, k_ref, v_ref, ckv_ref, kpe_ref, *, prompt_tiles):
    x = _pick_rows(xp_ref, xs_ref, prompt_tiles).astype(BF16)
    z = _dot(x, w_in_ref[...])
    cq = _rmsnorm(z[:, :Q_LORA], qn_ref[...])
    ckv = _rmsnorm(z[:, Q_LORA:Q_LORA + KV_LORA], kvn_ref[...])
    ct = ct_ref[...]
    st = st_ref[...]
    o = Q_LORA + KV_LORA
    kpe = z[:, o:o + SLOT] * ct + z[:, o + SLOT:o + 2 * SLOT] * st
    ckv_ref[...] = ckv
    kpe_ref[...] = kpe
    cqb = cq.astype(BF16)
    ckvb = ckv.astype(BF16)
    qa = _dot(cqb, wqa_ref[...])
    qb = _dot(cqb, wqb_ref[...])
    kn = _dot(ckvb, wk_ref[...])
    vv = _dot(ckvb, wv_ref[...]) + vone_ref[...]
    scale = ATTN_SCALE * math.log2(math.e)
    for h in range(N_HEADS):
        sl = slice(h * SLOT, (h + 1) * SLOT)
        q_ref[:, sl] = ((qa[:, sl] * ct + qb[:, sl] * st) * scale).astype(BF16)
        k_ref[:, sl] = (kn[:, sl] + kpe).astype(BF16)
    v_ref[...] = vv.astype(BF16)


def _mla_proj(xp, xs, ct, st, w):
    n = xp.shape[0] + xs.shape[0]
    pt = xp.shape[0] // ROW_TILE
    wide = N_HEADS * SLOT
    return pl.pallas_call(
        functools.partial(_mla_proj_kernel, prompt_tiles=pt),
        grid=(n // ROW_TILE,),
        in_specs=_split_rows(xp.shape[1], pt) + [_rows(SLOT), _rows(SLOT), _full(w["w_in"].shape), _full((1, Q_LORA)),
                  _full((1, KV_LORA)), _full(w["wqa"].shape), _full(w["wqb"].shape), _full(w["wk"].shape),
                  _full(w["wv"].shape), _full((1, wide))],
        out_specs=[_rows(wide), _rows(wide), _rows(wide), _rows(KV_LORA), _rows(SLOT)],
        out_shape=[jax.ShapeDtypeStruct((n, wide), BF16)] * 3
        + [jax.ShapeDtypeStruct((n, KV_LORA), F32), jax.ShapeDtypeStruct((n, SLOT), F32)],
        compiler_params=_cparams(("parallel",)),
        name="mla_proj",
    )(xp, xs, ct, st, w["w_in"], w["qn"], w["kvn"], w["wqa"], w["wqb"], w["wk"], w["wv"], w["vone"])


ATT_TQ = 1024
ATT_TK = 2048
ATT_G = 2


def _attn_kernel(q_ref, k_ref, v_ref, o_ref):
    qi = pl.program_id(1)
    slots = [slice(g * SLOT, (g + 1) * SLOT) for g in range(ATT_G)]
    ratio = ATT_TK // ATT_TQ

    def update(off, tk, state, mask):
        out = []
        for sl, (m, acc) in zip(slots, state):
            s = _dot_nt(q_ref[:, sl], k_ref[pl.ds(off, tk), sl])
            if mask is not None:
                s = jnp.where(mask, s, NEG)
            m_new = jnp.maximum(m, jnp.max(s, axis=-1, keepdims=True))
            p = jnp.exp2(s - m_new)
            pv = _dot(p.astype(BF16), v_ref[pl.ds(off, tk), sl])
            out.append((m_new, acc * jnp.exp2(m - m_new) + pv))
        return tuple(out)

    def body(j, state):
        return update(pl.multiple_of(j * ATT_TK, ATT_TK), ATT_TK, state, None)

    init = tuple((jnp.full((ATT_TQ, 1), NEG, F32), jnp.zeros((ATT_TQ, SLOT), F32)) for _ in range(ATT_G))
    state = lax.fori_loop(0, qi // ratio, body, init)

    def tail(rem):
        tk = (rem + 1) * ATT_TQ
        rc = lax.broadcasted_iota(jnp.int32, (ATT_TQ, tk), 0) // CHUNK
        cc = lax.broadcasted_iota(jnp.int32, (ATT_TQ, tk), 1) // CHUNK - rem * (ATT_TQ // CHUNK)
        return lambda st: update(pl.multiple_of((qi - rem) * ATT_TQ, ATT_TQ), tk, st, cc <= rc)

    state = lax.switch(qi % ratio, [tail(rem) for rem in range(ratio)], state)
    lane = lax.broadcasted_iota(jnp.int32, (ATT_TQ, SLOT), 1)
    for g in range(ATT_G):
        acc = state[g][1]
        l = jnp.sum(jnp.where(lane == V_DIM, acc, 0.0), axis=-1, keepdims=True)
        o_ref[:, g * SLOT:(g + 1) * SLOT] = (acc / l).astype(BF16)


def _prompt_attention(q, k, v, n_prompt):
    n = q.shape[0]
    wide = ATT_G * SLOT
    assert ATT_TK % ATT_TQ == 0 and ATT_TQ % CHUNK == 0 and n_prompt % ATT_TQ == 0 and N_HEADS % ATT_G == 0
    return pl.pallas_call(
        _attn_kernel,
        grid=(N_HEADS // ATT_G, n_prompt // ATT_TQ),
        in_specs=[pl.BlockSpec((ATT_TQ, wide), lambda h, i: (i, h)),
                  pl.BlockSpec((n_prompt, wide), lambda h, i: (0, h)),
                  pl.BlockSpec((n_prompt, wide), lambda h, i: (0, h))],
        out_specs=pl.BlockSpec((ATT_TQ, wide), lambda h, i: (i, h)),
        out_shape=jax.ShapeDtypeStruct((n_prompt, N_HEADS * SLOT), BF16),
        compiler_params=_cparams(("parallel", "arbitrary")),
        name="prompt_attention",
    )(q, k, v)


def _sample_attn_kernel(q_ref, ckvc_ref, kpec_ref, ckvn_ref, kpen_ref, wabs_ref, wuv_ref, o_ref):
    q = q_ref[...]
    seq = q.shape[0]
    qa, qp = [], []
    for h in range(N_HEADS):
        qh = q[:, h * SLOT:(h + 1) * SLOT]
        qa.append(_dot(qh, wabs_ref[h]))
        qp.append(qh[:, QK_NOPE:QK_NOPE + QK_ROPE])
    qa = jnp.concatenate(qa, axis=0).astype(BF16)
    qp = jnp.concatenate(qp, axis=0)
    ckvc = ckvc_ref[0].astype(BF16)
    kpec = kpec_ref[0].astype(BF16)
    ckvn = ckvn_ref[...].astype(BF16)
    kpen = kpen_ref[...][:, QK_NOPE:QK_NOPE + QK_ROPE].astype(BF16)
    s_c = _dot_nt(qa, ckvc) + _dot_nt(qp, kpec)
    s_n = _dot_nt(qa, ckvn) + _dot_nt(qp, kpen)
    m = jnp.maximum(jnp.max(s_c, axis=-1, keepdims=True), jnp.max(s_n, axis=-1, keepdims=True))
    p_c = jnp.exp2(s_c - m)
    p_n = jnp.exp2(s_n - m)
    l = jnp.sum(p_c, axis=-1, keepdims=True) + jnp.sum(p_n, axis=-1, keepdims=True)
    ol = (_dot(p_c.astype(BF16), ckvc) + _dot(p_n.astype(BF16), ckvn)) / l
    olb = ol.astype(BF16)
    for h in range(N_HEADS):
        o_ref[:, h * SLOT:(h + 1) * SLOT] = _dot(olb[h * seq:(h + 1) * seq], wuv_ref[h]).astype(BF16)


def _sample_attention(q, cache_ckv, cache_kpe, ckv_new, kpe_new, wabs, wuv, n_prompt):
    nb, past, _ = cache_ckv.shape
    seq = (q.shape[0] - n_prompt) // nb
    base = n_prompt // seq
    wide = N_HEADS * SLOT
    return pl.pallas_call(
        _sample_attn_kernel,
        grid=(nb,),
        in_specs=[pl.BlockSpec((seq, wide), lambda b: (base + b, 0)),
                  pl.BlockSpec((1, past, KV_LORA), lambda b: (b, 0, 0)),
                  pl.BlockSpec((1, past, QK_ROPE), lambda b: (b, 0, 0)),
                  pl.BlockSpec((seq, KV_LORA), lambda b: (base + b, 0)),
                  pl.BlockSpec((seq, SLOT), lambda b: (base + b, 0)),
                  _full(wabs.shape), _full(wuv.shape)],
        out_specs=pl.BlockSpec((seq, wide), lambda b: (b, 0)),
        out_shape=jax.ShapeDtypeStruct((nb * seq, wide), BF16),
        compiler_params=_cparams(("parallel",)),
        name="sample_attention",
    )(q, cache_ckv, cache_kpe, ckv_new, kpe_new, wabs, wuv)


ROUTE_E1, ROUTE_E2, ROUTE_W1, ROUTE_W2 = range(4)


def _lane_col(tile, k):
    lane = lax.broadcasted_iota(jnp.int32, tile.shape, 1)
    return jnp.sum(jnp.where(lane == k, tile, 0.0), axis=-1, keepdims=True)


def _route(lg):
    lane = lax.broadcasted_iota(jnp.int32, lg.shape, 1)
    lanef = lane.astype(F32)
    big = float(ROUTER_LANES)
    is_g = lane < MOE_GROUPS
    gl = jnp.where(is_g, lg, NEG)
    gmax = jnp.max(gl, axis=-1, keepdims=True)
    gsel = jnp.min(jnp.where(gl == gmax, lanef, big), axis=-1, keepdims=True)
    gate_g = 1.0 / jnp.sum(jnp.where(is_g, jnp.exp(gl - gmax), 0.0), axis=-1, keepdims=True)
    lo = EXPERT_LANE0 + MOE_EPG * gsel
    el = jnp.where(jnp.logical_and(lanef >= lo, lanef < lo + MOE_EPG), lg, NEG)
    m1 = jnp.max(el, axis=-1, keepdims=True)
    i1 = jnp.min(jnp.where(el == m1, lanef, big), axis=-1, keepdims=True)
    el2 = jnp.where(lanef == i1, NEG, el)
    m2 = jnp.max(el2, axis=-1, keepdims=True)
    i2 = jnp.min(jnp.where(el2 == m2, lanef, big), axis=-1, keepdims=True)
    r = jnp.exp(m2 - m1)
    w1 = 1.0 / (1.0 + r)
    w2 = r / (1.0 + r)
    cols = {ROUTE_E1: i1 - EXPERT_LANE0, ROUTE_E2: i2 - EXPERT_LANE0, ROUTE_W1: gate_g * w1, ROUTE_W2: gate_g * w2}
    route = jnp.zeros(lg.shape, F32)
    for k, c in cols.items():
        route = jnp.where(lane == k, c, route)
    return route


def _post_mix(t, x, g, b, wr_hi, wr_lo, br, h_ref, hb_ref, route_ref):
    h = _layernorm(ALPHA * x + t, g, b)
    h_ref[...] = h
    hb = h.astype(BF16)
    hb_ref[...] = hb
    h_lo = (h - hb.astype(F32)).astype(BF16)
    lg = _dot(hb, wr_hi) + (_dot(hb, wr_lo) + _dot(h_lo, wr_hi)) + br
    route_ref[...] = _route(lg)


def _attn_out_kernel(op_ref, os_ref, xp_ref, xs_ref, wo_ref, g_ref, b_ref, wrh_ref, wrl_ref, br_ref, h_ref, hb_ref,
                     route_ref, *, prompt_tiles):
    t = _dot(_pick_rows(op_ref, os_ref, prompt_tiles), wo_ref[...])
    _post_mix(t, _pick_rows(xp_ref, xs_ref, prompt_tiles), g_ref[...], b_ref[...], wrh_ref[...], wrl_ref[...],
              br_ref[...], h_ref, hb_ref, route_ref)


def _post_out(n, d):
    specs = [_rows(d), _rows(d), _rows(ROUTER_LANES)]
    shapes = [jax.ShapeDtypeStruct((n, d), F32), jax.ShapeDtypeStruct((n, d), BF16),
              jax.ShapeDtypeStruct((n, ROUTER_LANES), F32)]
    return specs, shapes


def _attn_out(o_prompt, o_sample, xp, xs, wo, lw):
    d = xp.shape[1]
    n = xp.shape[0] + xs.shape[0]
    pt = xp.shape[0] // ROW_TILE
    specs, shapes = _post_out(n, d)
    return pl.pallas_call(
        functools.partial(_attn_out_kernel, prompt_tiles=pt),
        grid=(n // ROW_TILE,),
        in_specs=_split_rows(o_prompt.shape[1], pt) + _split_rows(d, pt)
        + [_full(wo.shape), _full((1, d)), _full((1, d)),
           _full(lw["wr_hi"].shape), _full(lw["wr_lo"].shape), _full((1, ROUTER_LANES))],
        out_specs=specs, out_shape=shapes,
        compiler_params=_cparams(("parallel",)),
        name="attn_out_ln_router",
    )(o_prompt, o_sample, xp, xs, wo, lw["ln1_g"], lw["ln1_b"], lw["wr_hi"], lw["wr_lo"], lw["br"])


MOE_EPS = 8
MOE_TILE_MAX = 896


def _moe_dense_kernel(hb_ref, route_ref, wg_ref, wu_ref, wd_ref, y_ref):
    x = hb_ref[...]
    route = route_ref[...]
    e1 = _lane_col(route, ROUTE_E1)
    e2 = _lane_col(route, ROUTE_E2)
    w1 = _lane_col(route, ROUTE_W1)
    w2 = _lane_col(route, ROUTE_W2)
    y = None
    for k in range(MOE_EPS):
        e = (pl.program_id(1) * MOE_EPS + k).astype(F32)
        g = _dot(x, wg_ref[k])
        u = _dot(x, wu_ref[k])
        gate = jnp.where(e1 == e, w1, 0.0) + jnp.where(e2 == e, w2, 0.0)
        hdn = (g * jax.nn.sigmoid(g)) * u * gate
        yk = _dot(hdn.astype(BF16), wd_ref[k])
        y = yk if y is None else y + yk

    @pl.when(pl.program_id(1) == 0)
    def _():
        y_ref[...] = y

    @pl.when(pl.program_id(1) > 0)
    def _():
        y_ref[...] += y


def _moe_tile(n):
    return max(t for t in range(16, MOE_TILE_MAX + 1, 16) if n % t == 0)


def _moe_dense(hb, route, wg, wu, wd):
    n, d = hb.shape
    ff = wg.shape[-1]
    tile = _moe_tile(n)
    return pl.pallas_call(
        _moe_dense_kernel,
        grid=(n // tile, N_EXPERTS // MOE_EPS),
        in_specs=[pl.BlockSpec((tile, d), lambda i, e: (i, 0)),
                  pl.BlockSpec((tile, ROUTER_LANES), lambda i, e: (i, 0)),
                  pl.BlockSpec((MOE_EPS, d, ff), lambda i, e: (e, 0, 0)),
                  pl.BlockSpec((MOE_EPS, d, ff), lambda i, e: (e, 0, 0)),
                  pl.BlockSpec((MOE_EPS, ff, d), lambda i, e: (e, 0, 0))],
        out_specs=pl.BlockSpec((tile, d), lambda i, e: (i, 0)),
        out_shape=jax.ShapeDtypeStruct((n, d), F32),
        compiler_params=_cparams(("parallel", "arbitrary")),
        name="moe_dense",
    )(hb, route, wg, wu, wd)


def _ln2_ple_kernel(h_ref, y_ref, pp_ref, ps_ref, g_ref, b_ref, wp_ref, wpg_ref, x_ref, *, prompt_tiles):
    h2 = _layernorm(ALPHA * h_ref[...] + y_ref[...], g_ref[...], b_ref[...])
    proj = _dot(_pick_rows(pp_ref, ps_ref, prompt_tiles).astype(BF16), wp_ref[...])
    gate = jax.nn.sigmoid(_dot(h2.astype(BF16), wpg_ref[...]))
    x_ref[...] = h2 + proj * gate


def _ln2_ple(h, y, pp, ps, lw):
    n, d = h.shape
    pt = pp.shape[0] // ROW_TILE
    return pl.pallas_call(
        functools.partial(_ln2_ple_kernel, prompt_tiles=pt),
        grid=(n // ROW_TILE,),
        in_specs=[_rows(d), _rows(d)] + _split_rows(pp.shape[1], pt)
        + [_full((1, d)), _full((1, d)), _full(lw["w_proj"].shape), _full(lw["w_pg"].shape)],
        out_specs=_rows(d),
        out_shape=jax.ShapeDtypeStruct((n, d), F32),
        compiler_params=_cparams(("parallel",)),
        name="ln2_ple",
    )(h, y, pp, ps, lw["ln2_g"], lw["ln2_b"], lw["w_proj"], lw["w_pg"])


def _ffn(h, hb, route, pp, ps, lw):
    y = _moe_dense(hb, route, lw["wg"], lw["wu"], lw["wd"])
    return _ln2_ple(h, y, pp, ps, lw)


def _ssm_in_kernel(x_ref, w_ref, u_ref):
    u_ref[...] = _dot(x_ref[...].astype(BF16), w_ref[...])


def _ssm_in(x, w):
    n, d = x.shape
    return pl.pallas_call(
        _ssm_in_kernel,
        grid=(n // ROW_TILE,),
        in_specs=[_rows(d), _full(w.shape)],
        out_specs=_rows(w.shape[1]),
        out_shape=jax.ShapeDtypeStruct((n, w.shape[1]), F32),
        compiler_params=_cparams(("parallel",)),
        name="ssm_in",
    )(x, w)


SSM_GB = LANES // SSM_GROUP_CH
SSM_PAIR = 2 * LANES


def _ssm_core_kernel(u_ref, bd_ref, bst_ref, cst_ref, apow_ref, h0_ref, y_ref, hp_ref, hs_ref, sre_ref, sim_ref,
                     *, n_chunks, levels):
    t = SSM_T
    rows = u_ref.shape[0] // t
    pad = sre_ref.shape[0] - n_chunks
    npair = SSM_GB // 2
    ut = [u_ref[pl.ds(tl, rows, stride=t), :].astype(BF16) for tl in range(t)]
    u2 = [jnp.concatenate([ut[2 * k], ut[2 * k + 1]], axis=1) for k in range(t // 2)]
    bb = None
    for k in range(t // 2):
        part = _dot(u2[k], jnp.concatenate([bst_ref[0, 2 * k], bst_ref[0, 2 * k + 1]], axis=0))
        bb = part if bb is None else bb + part
    zeros = jnp.zeros((pad, LANES), F32)
    sre_ref[:pad, :] = zeros
    sim_ref[:pad, :] = zeros

    def shifted(d):
        return sre_ref[pl.ds(pad - d, n_chunks), :], sim_ref[pl.ds(pad - d, n_chunks), :]

    h0 = h0_ref[0]
    hprev, h_last, h_samp = [], [], []
    for j in range(npair):
        lo = j * SSM_PAIR
        re = bb[:n_chunks, lo:lo + LANES]
        im = bb[:n_chunks, lo + LANES:lo + SSM_PAIR]
        for lv in range(levels):
            sre_ref[pad:, :] = re
            sim_ref[pad:, :] = im
            pr, pi = shifted(1 << lv)
            ar = apow_ref[0, lv:lv + 1, lo:lo + LANES]
            ai = apow_ref[0, lv:lv + 1, lo + LANES:lo + SSM_PAIR]
            re, im = re + ar * pr - ai * pi, im + ar * pi + ai * pr
        sre_ref[pad:, :] = re
        sim_ref[pad:, :] = im
        pr, pi = shifted(1)
        h0j = h0[:, lo:lo + SSM_PAIR]
        hprev.append(jnp.concatenate([jnp.concatenate([pr, pi], axis=1), h0j], axis=0).astype(BF16))
        h_last.append(jnp.concatenate([re[n_chunks - 1:, :], im[n_chunks - 1:, :]], axis=1))
        ar = apow_ref[0, 0:1, lo:lo + LANES]
        ai = apow_ref[0, 0:1, lo + LANES:lo + SSM_PAIR]
        h0r = h0j[:, :LANES]
        h0i = h0j[:, LANES:]
        h_samp.append(jnp.concatenate([ar * h0r - ai * h0i + bb[n_chunks:, lo:lo + LANES],
                                       ar * h0i + ai * h0r + bb[n_chunks:, lo + LANES:lo + SSM_PAIR]], axis=1))
    hp_ref[0] = jnp.concatenate(h_last, axis=1)
    hs_ref[0] = jnp.concatenate(h_samp, axis=1)
    hprev = jnp.concatenate(hprev, axis=1)
    zero_tap = jnp.zeros((LANES, LANES), BF16)

    def tap(lag):
        return bd_ref[0, lag] if lag >= 0 else zero_tap

    for ko in range(t // 2):
        acc = _dot_nt(hprev, jnp.concatenate([cst_ref[0, 2 * ko], cst_ref[0, 2 * ko + 1]], axis=0))
        for ki in range(ko + 1):
            lag = 2 * (ko - ki)
            w = jnp.concatenate([jnp.concatenate([tap(lag), tap(lag + 1)], axis=1),
                                 jnp.concatenate([tap(lag - 1), tap(lag)], axis=1)], axis=0)
            acc = acc + _dot(u2[ki], w)
        y_ref[pl.ds(2 * ko, rows, stride=t), :] = acc[:, :LANES]
        y_ref[pl.ds(2 * ko + 1, rows, stride=t), :] = acc[:, LANES:]


def _ssm_core(u, tabs, h0, n_chunks):
    n, d = u.shape
    nblk = d // LANES
    ns = n // SSM_T - n_chunks
    levels = max(1, (n_chunks - 1).bit_length())
    pad = max(1 << (levels - 1), 8)
    wide = (SSM_GB // 2) * SSM_PAIR
    kern = functools.partial(_ssm_core_kernel, n_chunks=n_chunks, levels=levels)
    once = dict(pipeline_mode=pl.Buffered(1))

    def blk(*shape):
        return pl.BlockSpec((1,) + shape, lambda i: (i,) + (0,) * len(shape), **once)

    return pl.pallas_call(
        kern,
        grid=(nblk,),
        in_specs=[pl.BlockSpec((n, LANES), lambda i: (0, i), **once), blk(SSM_T, LANES, LANES),
                  blk(SSM_T, LANES, wide), blk(SSM_T, LANES, wide), blk(tabs["apow"].shape[1], wide), blk(ns, wide)],
        out_specs=[pl.BlockSpec((n, LANES), lambda i: (0, i)),
                   pl.BlockSpec((1, 1, wide), lambda i: (i, 0, 0)), pl.BlockSpec((1, ns, wide), lambda i: (i, 0, 0))],
        out_shape=[jax.ShapeDtypeStruct((n, d), F32), jax.ShapeDtypeStruct((nblk, 1, wide), F32),
                   jax.ShapeDtypeStruct((nblk, ns, wide), F32)],
        scratch_shapes=[pltpu.VMEM((pad + n_chunks, LANES), F32), pltpu.VMEM((pad + n_chunks, LANES), F32)],
        compiler_params=_cparams(("parallel",)),
        name="ssm_core",
    )(u, tabs["bd"], tabs["bst"], tabs["cst"], tabs["apow"], h0)


def _gelu_tanh(y):
    c = math.sqrt(2.0 / math.pi)
    return 0.5 * y * (1.0 + jnp.tanh(c * (y + 0.044715 * (y * y * y))))


def _ssm_out_kernel(y_ref, u_ref, x_ref, d_ref, wglu_ref, g_ref, b_ref, wrh_ref, wrl_ref, br_ref,
                    h_ref, hb_ref, route_ref):
    y = y_ref[...] + d_ref[...] * u_ref[...]
    z = _dot(_gelu_tanh(y).astype(BF16), wglu_ref[...])
    dm = z.shape[1] // 2
    t = z[:, :dm] * jax.nn.sigmoid(z[:, dm:])
    _post_mix(t, x_ref[...], g_ref[...], b_ref[...], wrh_ref[...], wrl_ref[...], br_ref[...],
              h_ref, hb_ref, route_ref)


def _ssm_out(y, u, x, d_skip, wglu, lw):
    n, d = x.shape
    specs, shapes = _post_out(n, d)
    return pl.pallas_call(
        _ssm_out_kernel,
        grid=(n // ROW_TILE,),
        in_specs=[_rows(d), _rows(d), _rows(d), _full((1, d)), _full(wglu.shape), _full((1, d)), _full((1, d)),
                  _full(lw["wr_hi"].shape), _full(lw["wr_lo"].shape), _full((1, ROUTER_LANES))],
        out_specs=specs, out_shape=shapes,
        compiler_params=_cparams(("parallel",)),
        name="ssm_out_ln_router",
    )(y, u, x, d_skip, wglu, lw["ln1_g"], lw["ln1_b"], lw["wr_hi"], lw["wr_lo"], lw["br"])


def _rope_tables(pos):
    half = QK_ROPE // 2
    inv = ROPE_BASE ** (-jnp.arange(half, dtype=F32) / half)
    ang = pos.astype(F32)[:, None] * inv[None, :]
    cos, sin = jnp.cos(ang), jnp.sin(ang)
    n = pos.shape[0]
    ones = jnp.ones((n, QK_NOPE), F32)
    zeros = jnp.zeros((n, SLOT - QK_NOPE - QK_ROPE), F32)
    ct = jnp.concatenate([ones, cos, cos, zeros], axis=1)
    st = jnp.concatenate([jnp.zeros((n, QK_NOPE), F32), -sin, sin, zeros], axis=1)
    return ct, st


def _slot_cols(w, width):
    k = w.shape[0]
    return jnp.pad(w, ((0, 0), (0, 0), (0, SLOT - width))).reshape(k, N_HEADS * SLOT)


def _mla_tables(w_in, q_norm, kv_norm, w_uq, w_uk, w_uv, w_o):
    half = QK_ROPE // 2
    o = Q_LORA + KV_LORA
    d = w_in.shape[0]
    kpe_w = w_in[:, o:]
    kpe_sw = jnp.concatenate([kpe_w[:, half:], kpe_w[:, :half]], axis=1)
    zl = jnp.zeros((d, QK_NOPE), F32)
    zr = jnp.zeros((d, SLOT - QK_NOPE - QK_ROPE), F32)
    w_in_e = jnp.concatenate([w_in[:, :o], zl, kpe_w, zr, zl, kpe_sw, zr], axis=1)
    wq = w_uq.reshape(Q_LORA, N_HEADS, QK_NOPE + QK_ROPE)
    pe = wq[:, :, QK_NOPE:]
    pe_sw = jnp.concatenate([pe[:, :, half:], pe[:, :, :half]], axis=2)
    wqb = jnp.concatenate([jnp.zeros_like(wq[:, :, :QK_NOPE]), pe_sw], axis=2)
    vone = jnp.zeros((N_HEADS, SLOT), F32).at[:, V_DIM].set(1.0).reshape(1, N_HEADS * SLOT)
    wabs = jnp.pad(jnp.transpose(w_uk, (1, 2, 0)), ((0, 0), (0, SLOT - QK_NOPE), (0, 0)))
    wuv = jnp.pad(jnp.transpose(w_uv, (1, 0, 2)), ((0, 0), (0, 0), (0, SLOT - V_DIM)))
    wo = jnp.pad(w_o.reshape(N_HEADS, V_DIM, -1), ((0, 0), (0, SLOT - V_DIM), (0, 0)))
    return dict(
        w_in=w_in_e.astype(BF16), qn=q_norm.reshape(1, -1), kvn=kv_norm.reshape(1, -1),
        wqa=_slot_cols(wq, QK_NOPE + QK_ROPE).astype(BF16), wqb=_slot_cols(wqb, QK_NOPE + QK_ROPE).astype(BF16),
        wk=_slot_cols(w_uk, QK_NOPE).astype(BF16), wv=_slot_cols(w_uv, V_DIM).astype(BF16), vone=vone,
        wabs=wabs.astype(BF16), wuv=wuv.astype(BF16), wo=wo.reshape(N_HEADS * SLOT, -1).astype(BF16))


def _cmul(a, b):
    return a[0] * b[0] - a[1] * b[1], a[0] * b[1] + a[1] * b[0]


def _ssm_tables(a_re, a_im, log_dt, b_re, b_im, c_re, c_im, levels):
    t = SSM_T
    g, p = a_re.shape
    c = SSM_GROUP_CH
    hi = lax.Precision.HIGHEST
    dt = jnp.exp(log_dt)[:, None]
    mag = jnp.exp(a_re * dt)
    lam_bar = (mag * jnp.cos(a_im * dt), mag * jnp.sin(a_im * dt))
    den = a_re * a_re + a_im * a_im
    quo = (((lam_bar[0] - 1.0) * a_re + lam_bar[1] * a_im) / den,
           (lam_bar[1] * a_re - (lam_bar[0] - 1.0) * a_im) / den)
    b_bar = _cmul((quo[0][:, :, None], quo[1][:, :, None]), (b_re, b_im))
    pw = [(jnp.ones_like(a_re), jnp.zeros_like(a_re))]
    for _ in range(t):
        pw.append(_cmul(pw[-1], lam_bar))
    pw_r = jnp.stack([x[0] for x in pw])
    pw_i = jnp.stack([x[1] for x in pw])
    cp = _cmul((c_re[None], c_im[None]), (pw_r[:t, :, None, :], pw_i[:t, :, None, :]))
    taps = (jnp.einsum("tgop,gpi->tgoi", cp[0], b_bar[0], precision=hi)
            - jnp.einsum("tgop,gpi->tgoi", cp[1], b_bar[1], precision=hi))
    nblk = g // SSM_GB
    width = SSM_GB * c
    tg = jnp.transpose(taps.reshape(t, nblk, SSM_GB, c, c), (1, 0, 2, 4, 3))
    own = np.zeros((SSM_GB, c, SSM_GB, c), np.float32)
    for g8 in range(SSM_GB):
        own[g8, np.arange(c), g8, np.arange(c)] = 1.0
    own = jnp.asarray(own.reshape(SSM_GB, c, width), BF16)
    bd = jnp.einsum("btgio,gol->btgil", tg.astype(BF16), own, preferred_element_type=BF16)
    bd = bd.reshape(nblk, t, width, width)
    lanes = (SSM_GB // 2) * SSM_PAIR
    place = np.zeros((SSM_GB, 2, p, SSM_GB // 2, 2, 2, p), np.float32)
    for g8 in range(SSM_GB):
        for ri in range(2):
            place[g8, ri, np.arange(p), g8 // 2, ri, g8 % 2, np.arange(p)] = 1.0
    place = jnp.asarray(place.reshape(SSM_GB, 2 * p, lanes), BF16)

    def spread(v):
        v = jnp.transpose(v, (2, 1, 3, 5, 0, 4)).reshape(nblk, t, SSM_GB, c, 2 * p)
        out = jnp.einsum("btgck,gkl->btgcl", v.astype(BF16), place, preferred_element_type=BF16)
        return out.reshape(nblk, t, width, lanes)

    rev = t - 1 - jnp.arange(t)
    bfl = _cmul((pw_r[rev][:, :, :, None], pw_i[rev][:, :, :, None]), (b_bar[0][None], b_bar[1][None]))
    bst = spread(jnp.stack(bfl).reshape(2, t, nblk, SSM_GB, p, c))
    cfl = _cmul((c_re[None], c_im[None]), (pw_r[1:t + 1, :, None, :], pw_i[1:t + 1, :, None, :]))
    cv = jnp.stack([cfl[0], -cfl[1]]).reshape(2, t, nblk, SSM_GB, c, p)
    cst = spread(jnp.swapaxes(cv, 4, 5))
    ap = [pw[t]]
    for _ in range(levels - 1):
        ap.append(_cmul(ap[-1], ap[-1]))

    def pair_lanes(xs):
        v = jnp.stack(xs, axis=1).reshape(g // 2, 2, levels, p)
        return jnp.transpose(v, (0, 2, 1, 3)).reshape(g // 2, levels, 2 * p)

    apow = jnp.concatenate([pair_lanes([x[0] for x in ap]), pair_lanes([x[1] for x in ap])], axis=2)
    rows = -(-levels // 8) * 8
    apow = jnp.pad(apow, ((0, 0), (0, rows - levels), (0, 0)))
    return dict(bd=bd.astype(BF16), bst=bst.astype(BF16), cst=cst.astype(BF16), apow=_block_pairs(apow))


def _block_pairs(v):
    pairs, r, w = v.shape
    per = SSM_GB // 2
    return jnp.transpose(v.reshape(pairs // per, per, r, w), (0, 2, 1, 3)).reshape(pairs // per, r, per * w)


def _unblock_pairs(v):
    nblk, r, w = v.shape
    per = SSM_GB // 2
    return jnp.transpose(v.reshape(nblk, r, per, w // per), (0, 2, 1, 3)).reshape(nblk * per, r, w // per)


def _pair_states(re, im):
    b, g, p = re.shape
    r = jnp.transpose(re.reshape(b, g // 2, 2 * p), (1, 0, 2))
    i = jnp.transpose(im.reshape(b, g // 2, 2 * p), (1, 0, 2))
    return jnp.concatenate([r, i], axis=2)


def _unpair_states(h):
    pairs, b, w = h.shape
    p = w // 4
    r = jnp.transpose(h[:, :, :2 * p], (1, 0, 2)).reshape(b, pairs * 2, p)
    i = jnp.transpose(h[:, :, 2 * p:], (1, 0, 2)).reshape(b, pairs * 2, p)
    return r, i


def _layer_tables(i, ln1_g, ln1_b, ln2_g, ln2_b, w_rg, b_rg, w_re, b_re, w_gate, w_up, w_down, w_proj, w_pg):
    d = w_rg.shape[1]
    wr = jnp.concatenate([w_rg[i], jnp.transpose(w_re[i], (1, 0, 2)).reshape(d, N_EXPERTS)], axis=1)
    wr = jnp.pad(wr, ((0, 0), (0, ROUTER_LANES - wr.shape[1])))
    wr_hi = wr.astype(BF16)
    wr_lo = (wr - wr_hi.astype(F32)).astype(BF16)
    br = jnp.pad(jnp.concatenate([b_rg[i], b_re[i].reshape(-1)]), (0, ROUTER_LANES - MOE_GROUPS - N_EXPERTS))
    ff = w_gate.shape[-1]
    return dict(
        ln1_g=ln1_g[i].reshape(1, d), ln1_b=ln1_b[i].reshape(1, d), ln2_g=ln2_g[i].reshape(1, d),
        ln2_b=ln2_b[i].reshape(1, d), wr_hi=wr_hi, wr_lo=wr_lo, br=br.reshape(1, ROUTER_LANES),
        wg=w_gate[i].reshape(N_EXPERTS, d, ff).astype(BF16), wu=w_up[i].reshape(N_EXPERTS, d, ff).astype(BF16),
        wd=w_down[i].reshape(N_EXPERTS, ff, d).astype(BF16),
        w_proj=w_proj[i].astype(BF16), w_pg=w_pg[i].astype(BF16))


def kernel(x_prompt, x_sample, p_prompt, p_sample, cache_mla_ckv, cache_mla_kpe, state_ssm_re, state_ssm_im, mla_w_in, mla_q_norm, mla_kv_norm, mla_w_uq, mla_w_uk, mla_w_uv, mla_w_o, ssm_w_in, ssm_a_re, ssm_a_im, ssm_log_dt, ssm_b_re, ssm_b_im, ssm_c_re, ssm_c_im, ssm_d, ssm_w_glu, ln1_g, ln1_b, ln2_g, ln2_b, moe_w_rg, moe_b_rg, moe_w_re, moe_b_re, moe_w_gate, moe_w_up, moe_w_down, ple_w_proj, ple_w_gate):
    bp, n_prompt, d = x_prompt.shape
    nb, seq, _ = x_sample.shape
    past = cache_mla_ckv.shape[2]
    assert bp == 1 and seq == SSM_T and n_prompt % ROW_TILE == 0 and (nb * seq) % ROW_TILE == 0
    assert n_prompt % CHUNK == 0 and past % CHUNK == 0 and seq <= CHUNK
    n_samp = nb * seq
    n = n_prompt + n_samp
    xp = x_prompt.reshape(n_prompt, d)
    xs = x_sample.reshape(n_samp, d)
    pp = p_prompt.reshape(DEPTH, n_prompt, -1)
    ps = p_sample.reshape(DEPTH, n_samp, -1)
    layer_args = (ln1_g, ln1_b, ln2_g, ln2_b, moe_w_rg, moe_b_rg, moe_w_re, moe_b_re, moe_w_gate, moe_w_up,
                  moe_w_down, ple_w_proj, ple_w_gate)
    lw = _layer_tables(0, *layer_args)
    mw = _mla_tables(mla_w_in[0], mla_q_norm[0], mla_kv_norm[0], mla_w_uq[0], mla_w_uk[0], mla_w_uv[0], mla_w_o[0])
    pos = jnp.concatenate([jnp.arange(n_prompt, dtype=jnp.int32),
                           past + jnp.tile(jnp.arange(seq, dtype=jnp.int32), nb)])
    ct, st = _rope_tables(pos)
    q, k, v, ckv, kpe_slot = _mla_proj(xp, xs, ct, st, mw)
    o_prompt = _prompt_attention(q, k, v, n_prompt)
    o_sample = _sample_attention(q, cache_mla_ckv[0], cache_mla_kpe[0], ckv, kpe_slot, mw["wabs"], mw["wuv"],
                                 n_prompt)
    h, hb, route = _attn_out(o_prompt, o_sample, xp, xs, mw["wo"], lw)
    x = _ffn(h, hb, route, pp[0], ps[0], lw)
    kpe = kpe_slot[:, QK_NOPE:QK_NOPE + QK_ROPE]

    lw = _layer_tables(1, *layer_args)
    n_chunks = n_prompt // SSM_T
    levels = max(1, (n_chunks - 1).bit_length())
    tabs = _ssm_tables(ssm_a_re[0], ssm_a_im[0], ssm_log_dt[0], ssm_b_re[0], ssm_b_im[0], ssm_c_re[0],
                       ssm_c_im[0], levels)
    u = _ssm_in(x, ssm_w_in[0].astype(BF16))
    h0 = _block_pairs(_pair_states(state_ssm_re[0], state_ssm_im[0]))
    y, hp, hs = _ssm_core(u, tabs, h0, n_chunks)
    h, hb, route = _ssm_out(y, u, x, ssm_d[0].reshape(1, d), ssm_w_glu[0].astype(BF16), lw)
    x = _ffn(h, hb, route, pp[1], ps[1], lw)
    re_p, im_p = _unpair_states(_unblock_pairs(hp))
    re_s, im_s = _unpair_states(_unblock_pairs(hs))

    return (x[:n_prompt].reshape(1, n_prompt, d), x[n_prompt:].reshape(nb, seq, d),
            ckv[:n_prompt].reshape(1, 1, n_prompt, KV_LORA), kpe[:n_prompt].reshape(1, 1, n_prompt, QK_ROPE),
            re_p[None], im_p[None],
            ckv[n_prompt:].reshape(1, nb, seq, KV_LORA), kpe[n_prompt:].reshape(1, nb, seq, QK_ROPE),
            re_s[None], im_s[None])
```

```python
import functools
import math

import jax
import jax.numpy as jnp
import numpy as np
from jax import lax
from jax.experimental import pallas as pl
from jax.experimental.pallas import tpu as pltpu

F32 = jnp.float32
BF16 = jnp.bfloat16

N_HEADS = 16
Q_LORA = 256
KV_LORA = 128
QK_NOPE = 64
QK_ROPE = 32
V_DIM = 64
ROPE_BASE = 10000.0
ATTN_SCALE = (QK_NOPE + QK_ROPE) ** -0.5
CHUNK = 64
SSM_GROUP_CH = 16
SSM_STATE = 64
MOE_GROUPS = 4
MOE_EPG = 8
N_EXPERTS = MOE_GROUPS * MOE_EPG
DEPTH = 2
ALPHA = (2.0 * DEPTH) ** 0.25
LN_EPS = 1e-5
RMS_EPS = 1e-6
NEG = -1e30

LANES = 128
SLOT = LANES
ROW_TILE = 256
SSM_T = 16
VMEM_LIMIT = 56 * 1024 * 1024
ROUTER_LANES = LANES
EXPERT_LANE0 = MOE_GROUPS


def _cparams(sem):
    return pltpu.CompilerParams(dimension_semantics=sem, vmem_limit_bytes=VMEM_LIMIT)


def _dot(a, b):
    return jnp.dot(a, b, preferred_element_type=F32)


def _dot_nt(a, b):
    return lax.dot_general(a, b, (((1,), (1,)), ((), ())), preferred_element_type=F32)


def _full(shape):
    nd = len(shape)
    return pl.BlockSpec(shape, lambda *_: (0,) * nd)


def _rows(width, rows=ROW_TILE):
    return pl.BlockSpec((rows, width), lambda i: (i, 0))


def _split_rows(width, prompt_tiles):
    return [pl.BlockSpec((ROW_TILE, width), lambda i: (jnp.minimum(i, prompt_tiles - 1), 0)),
            pl.BlockSpec((ROW_TILE, width), lambda i: (jnp.maximum(i - prompt_tiles, 0), 0))]


def _pick_rows(prompt_ref, sample_ref, prompt_tiles):
    return jnp.where(pl.program_id(0) < prompt_tiles, prompt_ref[...], sample_ref[...])


def _layernorm(t, g, b):
    mu = jnp.mean(t, axis=-1, keepdims=True)
    d = t - mu
    var = jnp.mean(d * d, axis=-1, keepdims=True)
    return d * lax.rsqrt(var + LN_EPS) * g + b


def _rmsnorm(t, g):
    return t * lax.rsqrt(jnp.mean(t * t, axis=-1, keepdims=True) + RMS_EPS) * g


def _mla_proj_kernel(xp_ref, xs_ref, ct_ref, st_ref, w_in_ref, qn_ref, kvn_ref, wqa_ref, wqb_ref, wk_ref, wv_ref,
                     vone_ref, q_ref, k_ref, v_ref, ckv_ref, kpe_ref, *, prompt_tiles):
    x = _pick_rows(xp_ref, xs_ref, prompt_tiles).astype(BF16)
    z = _dot(x, w_in_ref[...])
    cq = _rmsnorm(z[:, :Q_LORA], qn_ref[...])
    ckv = _rmsnorm(z[:, Q_LORA:Q_LORA + KV_LORA], kvn_ref[...])
    ct = ct_ref[...]
    st = st_ref[...]
    o = Q_LORA + KV_LORA
    kpe = z[:, o:o + SLOT] * ct + z[:, o + SLOT:o + 2 * SLOT] * st
    ckv_ref[...] = ckv
    kpe_ref[...] = kpe
    cqb = cq.astype(BF16)
    ckvb = ckv.astype(BF16)
    qa = _dot(cqb, wqa_ref[...])
    qb = _dot(cqb, wqb_ref[...])
    kn = _dot(ckvb, wk_ref[...])
    vv = _dot(ckvb, wv_ref[...]) + vone_ref[...]
    scale = ATTN_SCALE * math.log2(math.e)
    for h in range(N_HEADS):
        sl = slice(h * SLOT, (h + 1) * SLOT)
        q_ref[:, sl] = ((qa[:, sl] * ct + qb[:, sl] * st) * scale).astype(BF16)
        k_ref[:, sl] = (kn[:, sl] + kpe).astype(BF16)
    v_ref[...] = vv.astype(BF16)


def _mla_proj(xp, xs, ct, st, w):
    n = xp.shape[0] + xs.shape[0]
    pt = xp.shape[0] // ROW_TILE
    wide = N_HEADS * SLOT
    return pl.pallas_call(
        functools.partial(_mla_proj_kernel, prompt_tiles=pt),
        grid=(n // ROW_TILE,),
        in_specs=_split_rows(xp.shape[1], pt) + [_rows(SLOT), _rows(SLOT), _full(w["w_in"].shape), _full((1, Q_LORA)),
                  _full((1, KV_LORA)), _full(w["wqa"].shape), _full(w["wqb"].shape), _full(w["wk"].shape),
                  _full(w["wv"].shape), _full((1, wide))],
        out_specs=[_rows(wide), _rows(wide), _rows(wide), _rows(KV_LORA), _rows(SLOT)],
        out_shape=[jax.ShapeDtypeStruct((n, wide), BF16)] * 3
        + [jax.ShapeDtypeStruct((n, KV_LORA), F32), jax.ShapeDtypeStruct((n, SLOT), F32)],
        compiler_params=_cparams(("parallel",)),
        name="mla_proj",
    )(xp, xs, ct, st, w["w_in"], w["qn"], w["kvn"], w["wqa"], w["wqb"], w["wk"], w["wv"], w["vone"])


ATT_TQ = 1024
ATT_TK = 2048
ATT_G = 2


def _attn_kernel(q_ref, k_ref, v_ref, o_ref):
    qi = pl.program_id(1)
    slots = [slice(g * SLOT, (g + 1) * SLOT) for g in range(ATT_G)]
    ratio = ATT_TK // ATT_TQ

    def update(off, tk, state, mask):
        out = []
        for sl, (m, acc) in zip(slots, state):
            s = _dot_nt(q_ref[:, sl], k_ref[pl.ds(off, tk), sl])
            if mask is not None:
                s = jnp.where(mask, s, NEG)
            m_new = jnp.maximum(m, jnp.max(s, axis=-1, keepdims=True))
            p = jnp.exp2(s - m_new)
            pv = _dot(p.astype(BF16), v_ref[pl.ds(off, tk), sl])
            out.append((m_new, acc * jnp.exp2(m - m_new) + pv))
        return tuple(out)

    def body(j, state):
        return update(pl.multiple_of(j * ATT_TK, ATT_TK), ATT_TK, state, None)

    init = tuple((jnp.full((ATT_TQ, 1), NEG, F32), jnp.zeros((ATT_TQ, SLOT), F32)) for _ in range(ATT_G))
    state = lax.fori_loop(0, qi // ratio, body, init)

    def tail(rem):
        tk = (rem + 1) * ATT_TQ
        rc = lax.broadcasted_iota(jnp.int32, (ATT_TQ, tk), 0) // CHUNK
        cc = lax.broadcasted_iota(jnp.int32, (ATT_TQ, tk), 1) // CHUNK - rem * (ATT_TQ // CHUNK)
        return lambda st: update(pl.multiple_of((qi - rem) * ATT_TQ, ATT_TQ), tk, st, cc <= rc)

    state = lax.switch(qi % ratio, [tail(rem) for rem in range(ratio)], state)
    lane = lax.broadcasted_iota(jnp.int32, (ATT_TQ, SLOT), 1)
    outs = []
    for g in range(ATT_G):
        acc = state[g][1]
        l = jnp.sum(jnp.where(lane == V_DIM, acc, 0.0), axis=-1, keepdims=True)
        outs.append((acc / l)[:, :V_DIM])
    o_ref[...] = jnp.concatenate(outs, axis=1).astype(BF16)


def _prompt_attention(q, k, v, n_prompt):
    n = q.shape[0]
    wide = ATT_G * SLOT
    assert ATT_TK % ATT_TQ == 0 and ATT_TQ % CHUNK == 0 and n_prompt % ATT_TQ == 0 and N_HEADS % ATT_G == 0
    assert (ATT_G * V_DIM) % LANES == 0
    return pl.pallas_call(
        _attn_kernel,
        grid=(N_HEADS // ATT_G, n_prompt // ATT_TQ),
        in_specs=[pl.BlockSpec((ATT_TQ, wide), lambda h, i: (i, h)),
                  pl.BlockSpec((n_prompt, wide), lambda h, i: (0, h)),
                  pl.BlockSpec((n_prompt, wide), lambda h, i: (0, h))],
        out_specs=pl.BlockSpec((ATT_TQ, ATT_G * V_DIM), lambda h, i: (i, h)),
        out_shape=jax.ShapeDtypeStruct((n_prompt, N_HEADS * V_DIM), BF16),
        compiler_params=_cparams(("parallel", "arbitrary")),
        name="prompt_attention",
    )(q, k, v)


def _sample_attn_kernel(q_ref, ckvc_ref, kpec_ref, ckvn_ref, kpen_ref, wabs_ref, wuv_ref, o_ref):
    q = q_ref[...]
    seq = q.shape[0]
    qa, qp = [], []
    for h in range(N_HEADS):
        qh = q[:, h * SLOT:(h + 1) * SLOT]
        qa.append(_dot(qh, wabs_ref[h]))
        qp.append(qh[:, QK_NOPE:QK_NOPE + QK_ROPE])
    qa = jnp.concatenate(qa, axis=0).astype(BF16)
    qp = jnp.concatenate(qp, axis=0)
    ckvc = ckvc_ref[0].astype(BF16)
    kpec = kpec_ref[0].astype(BF16)
    ckvn = ckvn_ref[...].astype(BF16)
    kpen = kpen_ref[...][:, QK_NOPE:QK_NOPE + QK_ROPE].astype(BF16)
    s_c = _dot_nt(qa, ckvc) + _dot_nt(qp, kpec)
    s_n = _dot_nt(qa, ckvn) + _dot_nt(qp, kpen)
    m = jnp.maximum(jnp.max(s_c, axis=-1, keepdims=True), jnp.max(s_n, axis=-1, keepdims=True))
    p_c = jnp.exp2(s_c - m)
    p_n = jnp.exp2(s_n - m)
    l = jnp.sum(p_c, axis=-1, keepdims=True) + jnp.sum(p_n, axis=-1, keepdims=True)
    ol = (_dot(p_c.astype(BF16), ckvc) + _dot(p_n.astype(BF16), ckvn)) / l
    olb = ol.astype(BF16)
    outs = [_dot(olb[h * seq:(h + 1) * seq], wuv_ref[h]) for h in range(N_HEADS)]
    o_ref[...] = jnp.concatenate(outs, axis=1).astype(BF16)


def _sample_attention(q, cache_ckv, cache_kpe, ckv_new, kpe_new, wabs, wuv, n_prompt):
    nb, past, _ = cache_ckv.shape
    seq = (q.shape[0] - n_prompt) // nb
    base = n_prompt // seq
    wide = N_HEADS * SLOT
    return pl.pallas_call(
        _sample_attn_kernel,
        grid=(nb,),
        in_specs=[pl.BlockSpec((seq, wide), lambda b: (base + b, 0)),
                  pl.BlockSpec((1, past, KV_LORA), lambda b: (b, 0, 0)),
                  pl.BlockSpec((1, past, QK_ROPE), lambda b: (b, 0, 0)),
                  pl.BlockSpec((seq, KV_LORA), lambda b: (base + b, 0)),
                  pl.BlockSpec((seq, SLOT), lambda b: (base + b, 0)),
                  _full(wabs.shape), _full(wuv.shape)],
        out_specs=pl.BlockSpec((seq, N_HEADS * V_DIM), lambda b: (b, 0)),
        out_shape=jax.ShapeDtypeStruct((nb * seq, N_HEADS * V_DIM), BF16),
        compiler_params=_cparams(("parallel",)),
        name="sample_attention",
    )(q, cache_ckv, cache_kpe, ckv_new, kpe_new, wabs, wuv)


ROUTE_E1, ROUTE_E2, ROUTE_W1, ROUTE_W2 = range(4)


def _lane_col(tile, k):
    lane = lax.broadcasted_iota(jnp.int32, tile.shape, 1)
    return jnp.sum(jnp.where(lane == k, tile, 0.0), axis=-1, keepdims=True)


def _route(lg):
    lane = lax.broadcasted_iota(jnp.int32, lg.shape, 1)
    lanef = lane.astype(F32)
    big = float(ROUTER_LANES)
    is_g = lane < MOE_GROUPS
    gl = jnp.where(is_g, lg, NEG)
    gmax = jnp.max(gl, axis=-1, keepdims=True)
    gsel = jnp.min(jnp.where(gl == gmax, lanef, big), axis=-1, keepdims=True)
    gate_g = 1.0 / jnp.sum(jnp.where(is_g, jnp.exp(gl - gmax), 0.0), axis=-1, keepdims=True)
    lo = EXPERT_LANE0 + MOE_EPG * gsel
    el = jnp.where(jnp.logical_and(lanef >= lo, lanef < lo + MOE_EPG), lg, NEG)
    m1 = jnp.max(el, axis=-1, keepdims=True)
    i1 = jnp.min(jnp.where(el == m1, lanef, big), axis=-1, keepdims=True)
    el2 = jnp.where(lanef == i1, NEG, el)
    m2 = jnp.max(el2, axis=-1, keepdims=True)
    i2 = jnp.min(jnp.where(el2 == m2, lanef, big), axis=-1, keepdims=True)
    r = jnp.exp(m2 - m1)
    w1 = 1.0 / (1.0 + r)
    w2 = r / (1.0 + r)
    cols = {ROUTE_E1: i1 - EXPERT_LANE0, ROUTE_E2: i2 - EXPERT_LANE0, ROUTE_W1: gate_g * w1, ROUTE_W2: gate_g * w2}
    route = jnp.zeros(lg.shape, F32)
    for k, c in cols.items():
        route = jnp.where(lane == k, c, route)
    return route


def _post_mix(t, x, g, b, wr_hi, wr_lo, br, h_ref, hb_ref, route_ref):
    h = _layernorm(ALPHA * x + t, g, b)
    h_ref[...] = h
    hb = h.astype(BF16)
    hb_ref[...] = hb
    h_lo = (h - hb.astype(F32)).astype(BF16)
    lg = _dot(hb, wr_hi) + (_dot(hb, wr_lo) + _dot(h_lo, wr_hi)) + br
    route_ref[...] = _route(lg)


def _attn_out_kernel(op_ref, os_ref, xp_ref, xs_ref, wo_ref, g_ref, b_ref, wrh_ref, wrl_ref, br_ref, h_ref, hb_ref,
                     route_ref, *, prompt_tiles):
    t = _dot(_pick_rows(op_ref, os_ref, prompt_tiles), wo_ref[...])
    _post_mix(t, _pick_rows(xp_ref, xs_ref, prompt_tiles), g_ref[...], b_ref[...], wrh_ref[...], wrl_ref[...],
              br_ref[...], h_ref, hb_ref, route_ref)


def _post_out(n, d):
    specs = [_rows(d), _rows(d), _rows(ROUTER_LANES)]
    shapes = [jax.ShapeDtypeStruct((n, d), F32), jax.ShapeDtypeStruct((n, d), BF16),
              jax.ShapeDtypeStruct((n, ROUTER_LANES), F32)]
    return specs, shapes


def _attn_out(o_prompt, o_sample, xp, xs, wo, lw):
    d = xp.shape[1]
    n = xp.shape[0] + xs.shape[0]
    pt = xp.shape[0] // ROW_TILE
    specs, shapes = _post_out(n, d)
    return pl.pallas_call(
        functools.partial(_attn_out_kernel, prompt_tiles=pt),
        grid=(n // ROW_TILE,),
        in_specs=_split_rows(o_prompt.shape[1], pt) + _split_rows(d, pt)
        + [_full(wo.shape), _full((1, d)), _full((1, d)),
           _full(lw["wr_hi"].shape), _full(lw["wr_lo"].shape), _full((1, ROUTER_LANES))],
        out_specs=specs, out_shape=shapes,
        compiler_params=_cparams(("parallel",)),
        name="attn_out_ln_router",
    )(o_prompt, o_sample, xp, xs, wo, lw["ln1_g"], lw["ln1_b"], lw["wr_hi"], lw["wr_lo"], lw["br"])


MOE_EPS = 8
MOE_TILE_MAX = 896


def _moe_dense_kernel(hb_ref, route_ref, wg_ref, wu_ref, wd_ref, y_ref):
    x = hb_ref[...]
    route = route_ref[...]
    e1 = _lane_col(route, ROUTE_E1)
    e2 = _lane_col(route, ROUTE_E2)
    w1 = _lane_col(route, ROUTE_W1)
    w2 = _lane_col(route, ROUTE_W2)
    y = None
    for k in range(MOE_EPS):
        e = (pl.program_id(1) * MOE_EPS + k).astype(F32)
        g = _dot(x, wg_ref[k])
        u = _dot(x, wu_ref[k])
        gate = jnp.where(e1 == e, w1, 0.0) + jnp.where(e2 == e, w2, 0.0)
        hdn = (g * jax.nn.sigmoid(g)) * u * gate
        yk = _dot(hdn.astype(BF16), wd_ref[k])
        y = yk if y is None else y + yk

    @pl.when(pl.program_id(1) == 0)
    def _():
        y_ref[...] = y

    @pl.when(pl.program_id(1) > 0)
    def _():
        y_ref[...] += y


def _moe_tile(n):
    return max(t for t in range(16, MOE_TILE_MAX + 1, 16) if n % t == 0)


def _moe_dense(hb, route, wg, wu, wd):
    n, d = hb.shape
    ff = wg.shape[-1]
    tile = _moe_tile(n)
    return pl.pallas_call(
        _moe_dense_kernel,
        grid=(n // tile, N_EXPERTS // MOE_EPS),
        in_specs=[pl.BlockSpec((tile, d), lambda i, e: (i, 0)),
                  pl.BlockSpec((tile, ROUTER_LANES), lambda i, e: (i, 0)),
                  pl.BlockSpec((MOE_EPS, d, ff), lambda i, e: (e, 0, 0)),
                  pl.BlockSpec((MOE_EPS, d, ff), lambda i, e: (e, 0, 0)),
                  pl.BlockSpec((MOE_EPS, ff, d), lambda i, e: (e, 0, 0))],
        out_specs=pl.BlockSpec((tile, d), lambda i, e: (i, 0)),
        out_shape=jax.ShapeDtypeStruct((n, d), F32),
        compiler_params=_cparams(("parallel", "arbitrary")),
        name="moe_dense",
    )(hb, route, wg, wu, wd)


def _ln2_ple_kernel(h_ref, y_ref, pp_ref, ps_ref, g_ref, b_ref, wp_ref, wpg_ref, *out_refs, prompt_tiles):
    h2 = _layernorm(ALPHA * h_ref[...] + y_ref[...], g_ref[...], b_ref[...])
    proj = _dot(_pick_rows(pp_ref, ps_ref, prompt_tiles).astype(BF16), wp_ref[...])
    gate = jax.nn.sigmoid(_dot(h2.astype(BF16), wpg_ref[...]))
    x = h2 + proj * gate
    if len(out_refs) == 1:
        out_refs[0][...] = x
    else:
        is_prompt = pl.program_id(0) < prompt_tiles

        @pl.when(is_prompt)
        def _():
            out_refs[0][...] = x

        @pl.when(jnp.logical_not(is_prompt))
        def _():
            out_refs[1][...] = x


def _ln2_ple(h, y, pp, ps, lw, split_out):
    n, d = h.shape
    pt = pp.shape[0] // ROW_TILE
    if split_out:
        out_specs = _split_rows(d, pt)
        out_shape = [jax.ShapeDtypeStruct((pp.shape[0], d), F32), jax.ShapeDtypeStruct((ps.shape[0], d), F32)]
    else:
        out_specs, out_shape = _rows(d), jax.ShapeDtypeStruct((n, d), F32)
    return pl.pallas_call(
        functools.partial(_ln2_ple_kernel, prompt_tiles=pt),
        grid=(n // ROW_TILE,),
        in_specs=[_rows(d), _rows(d)] + _split_rows(pp.shape[1], pt)
        + [_full((1, d)), _full((1, d)), _full(lw["w_proj"].shape), _full(lw["w_pg"].shape)],
        out_specs=out_specs, out_shape=out_shape,
        compiler_params=_cparams(("arbitrary",)),
        name="ln2_ple",
    )(h, y, pp, ps, lw["ln2_g"], lw["ln2_b"], lw["w_proj"], lw["w_pg"])


def _ffn(h, hb, route, pp, ps, lw, split_out=False):
    y = _moe_dense(hb, route, lw["wg"], lw["wu"], lw["wd"])
    return _ln2_ple(h, y, pp, ps, lw, split_out)


def _ssm_in_kernel(x_ref, w_ref, u_ref):
    u_ref[...] = _dot(x_ref[...].astype(BF16), w_ref[...])


def _ssm_in(x, w):
    n, d = x.shape
    return pl.pallas_call(
        _ssm_in_kernel,
        grid=(n // ROW_TILE,),
        in_specs=[_rows(d), _full(w.shape)],
        out_specs=_rows(w.shape[1]),
        out_shape=jax.ShapeDtypeStruct((n, w.shape[1]), F32),
        compiler_params=_cparams(("parallel",)),
        name="ssm_in",
    )(x, w)


SSM_GB = LANES // SSM_GROUP_CH
SSM_PAIR = 2 * LANES


def _ssm_core_kernel(u_ref, bd_ref, bst_ref, cst_ref, apow_ref, h0_ref, y_ref, hp_ref, hs_ref, sre_ref, sim_ref,
                     *, n_chunks, levels):
    t = SSM_T
    rows = u_ref.shape[0] // t
    pad = sre_ref.shape[0] - n_chunks
    npair = SSM_GB // 2
    ut = [u_ref[pl.ds(tl, rows, stride=t), :].astype(BF16) for tl in range(t)]
    u2 = [jnp.concatenate([ut[2 * k], ut[2 * k + 1]], axis=1) for k in range(t // 2)]
    bb = None
    for k in range(t // 2):
        part = _dot(u2[k], jnp.concatenate([bst_ref[0, 2 * k], bst_ref[0, 2 * k + 1]], axis=0))
        bb = part if bb is None else bb + part
    zeros = jnp.zeros((pad, LANES), F32)
    sre_ref[:pad, :] = zeros
    sim_ref[:pad, :] = zeros

    def shifted(d):
        return sre_ref[pl.ds(pad - d, n_chunks), :], sim_ref[pl.ds(pad - d, n_chunks), :]

    h0 = h0_ref[0]
    hprev, h_last, h_samp = [], [], []
    for j in range(npair):
        lo = j * SSM_PAIR
        re = bb[:n_chunks, lo:lo + LANES]
        im = bb[:n_chunks, lo + LANES:lo + SSM_PAIR]
        for lv in range(levels):
            sre_ref[pad:, :] = re
            sim_ref[pad:, :] = im
            pr, pi = shifted(1 << lv)
            ar = apow_ref[0, lv:lv + 1, lo:lo + LANES]
            ai = apow_ref[0, lv:lv + 1, lo + LANES:lo + SSM_PAIR]
            re, im = re + ar * pr - ai * pi, im + ar * pi + ai * pr
        sre_ref[pad:, :] = re
        sim_ref[pad:, :] = im
        pr, pi = shifted(1)
        h0j = h0[:, lo:lo + SSM_PAIR]
        hprev.append(jnp.concatenate([jnp.concatenate([pr, pi], axis=1), h0j], axis=0).astype(BF16))
        h_last.append(jnp.concatenate([re[n_chunks - 1:, :], im[n_chunks - 1:, :]], axis=1))
        ar = apow_ref[0, 0:1, lo:lo + LANES]
        ai = apow_ref[0, 0:1, lo + LANES:lo + SSM_PAIR]
        h0r = h0j[:, :LANES]
        h0i = h0j[:, LANES:]
        h_samp.append(jnp.concatenate([ar * h0r - ai * h0i + bb[n_chunks:, lo:lo + LANES],
                                       ar * h0i + ai * h0r + bb[n_chunks:, lo + LANES:lo + SSM_PAIR]], axis=1))
    hp_ref[0] = jnp.concatenate(h_last, axis=1)
    hs_ref[0] = jnp.concatenate(h_samp, axis=1)
    hprev = jnp.concatenate(hprev, axis=1)
    zero_tap = jnp.zeros((LANES, LANES), BF16)

    def tap(lag):
        return bd_ref[0, lag] if lag >= 0 else zero_tap

    for ko in range(t // 2):
        acc = _dot_nt(hprev, jnp.concatenate([cst_ref[0, 2 * ko], cst_ref[0, 2 * ko + 1]], axis=0))
        for ki in range(ko + 1):
            lag = 2 * (ko - ki)
            w = jnp.concatenate([jnp.concatenate([tap(lag), tap(lag + 1)], axis=1),
                                 jnp.concatenate([tap(lag - 1), tap(lag)], axis=1)], axis=0)
            acc = acc + _dot(u2[ki], w)
        y_ref[pl.ds(2 * ko, rows, stride=t), :] = acc[:, :LANES]
        y_ref[pl.ds(2 * ko + 1, rows, stride=t), :] = acc[:, LANES:]


def _ssm_core(u, tabs, h0, n_chunks):
    n, d = u.shape
    nblk = d // LANES
    ns = n // SSM_T - n_chunks
    levels = max(1, (n_chunks - 1).bit_length())
    pad = max(1 << (levels - 1), 8)
    wide = (SSM_GB // 2) * SSM_PAIR
    kern = functools.partial(_ssm_core_kernel, n_chunks=n_chunks, levels=levels)
    once = dict(pipeline_mode=pl.Buffered(1))

    def blk(*shape):
        return pl.BlockSpec((1,) + shape, lambda i: (i,) + (0,) * len(shape), **once)

    return pl.pallas_call(
        kern,
        grid=(nblk,),
        in_specs=[pl.BlockSpec((n, LANES), lambda i: (0, i), **once), blk(SSM_T, LANES, LANES),
                  blk(SSM_T, LANES, wide), blk(SSM_T, LANES, wide), blk(tabs["apow"].shape[1], wide), blk(ns, wide)],
        out_specs=[pl.BlockSpec((n, LANES), lambda i: (0, i)),
                   pl.BlockSpec((1, 1, wide), lambda i: (i, 0, 0)), pl.BlockSpec((1, ns, wide), lambda i: (i, 0, 0))],
        out_shape=[jax.ShapeDtypeStruct((n, d), F32), jax.ShapeDtypeStruct((nblk, 1, wide), F32),
                   jax.ShapeDtypeStruct((nblk, ns, wide), F32)],
        scratch_shapes=[pltpu.VMEM((pad + n_chunks, LANES), F32), pltpu.VMEM((pad + n_chunks, LANES), F32)],
        compiler_params=_cparams(("parallel",)),
        name="ssm_core",
    )(u, tabs["bd"], tabs["bst"], tabs["cst"], tabs["apow"], h0)


def _gelu_tanh(y):
    c = math.sqrt(2.0 / math.pi)
    return 0.5 * y * (1.0 + jnp.tanh(c * (y + 0.044715 * (y * y * y))))


def _ssm_out_kernel(y_ref, u_ref, x_ref, d_ref, wglu_ref, g_ref, b_ref, wrh_ref, wrl_ref, br_ref,
                    h_ref, hb_ref, route_ref):
    y = y_ref[...] + d_ref[...] * u_ref[...]
    z = _dot(_gelu_tanh(y).astype(BF16), wglu_ref[...])
    dm = z.shape[1] // 2
    t = z[:, :dm] * jax.nn.sigmoid(z[:, dm:])
    _post_mix(t, x_ref[...], g_ref[...], b_ref[...], wrh_ref[...], wrl_ref[...], br_ref[...],
              h_ref, hb_ref, route_ref)


def _ssm_out(y, u, x, d_skip, wglu, lw):
    n, d = x.shape
    specs, shapes = _post_out(n, d)
    return pl.pallas_call(
        _ssm_out_kernel,
        grid=(n // ROW_TILE,),
        in_specs=[_rows(d), _rows(d), _rows(d), _full((1, d)), _full(wglu.shape), _full((1, d)), _full((1, d)),
                  _full(lw["wr_hi"].shape), _full(lw["wr_lo"].shape), _full((1, ROUTER_LANES))],
        out_specs=specs, out_shape=shapes,
        compiler_params=_cparams(("parallel",)),
        name="ssm_out_ln_router",
    )(y, u, x, d_skip, wglu, lw["ln1_g"], lw["ln1_b"], lw["wr_hi"], lw["wr_lo"], lw["br"])


def _rope_tables(pos):
    half = QK_ROPE // 2
    inv = ROPE_BASE ** (-jnp.arange(half, dtype=F32) / half)
    ang = pos.astype(F32)[:, None] * inv[None, :]
    cos, sin = jnp.cos(ang), jnp.sin(ang)
    n = pos.shape[0]
    ones = jnp.ones((n, QK_NOPE), F32)
    zeros = jnp.zeros((n, SLOT - QK_NOPE - QK_ROPE), F32)
    ct = jnp.concatenate([ones, cos, cos, zeros], axis=1)
    st = jnp.concatenate([jnp.zeros((n, QK_NOPE), F32), -sin, sin, zeros], axis=1)
    return ct, st


def _slot_cols(w, width):
    k = w.shape[0]
    return jnp.pad(w, ((0, 0), (0, 0), (0, SLOT - width))).reshape(k, N_HEADS * SLOT)


def _mla_tables(w_in, q_norm, kv_norm, w_uq, w_uk, w_uv, w_o):
    half = QK_ROPE // 2
    o = Q_LORA + KV_LORA
    d = w_in.shape[0]
    kpe_w = w_in[:, o:]
    kpe_sw = jnp.concatenate([kpe_w[:, half:], kpe_w[:, :half]], axis=1)
    zl = jnp.zeros((d, QK_NOPE), F32)
    zr = jnp.zeros((d, SLOT - QK_NOPE - QK_ROPE), F32)
    w_in_e = jnp.concatenate([w_in[:, :o], zl, kpe_w, zr, zl, kpe_sw, zr], axis=1)
    wq = w_uq.reshape(Q_LORA, N_HEADS, QK_NOPE + QK_ROPE)
    pe = wq[:, :, QK_NOPE:]
    pe_sw = jnp.concatenate([pe[:, :, half:], pe[:, :, :half]], axis=2)
    wqb = jnp.concatenate([jnp.zeros_like(wq[:, :, :QK_NOPE]), pe_sw], axis=2)
    vone = jnp.zeros((N_HEADS, SLOT), F32).at[:, V_DIM].set(1.0).reshape(1, N_HEADS * SLOT)
    wabs = jnp.pad(jnp.transpose(w_uk, (1, 2, 0)), ((0, 0), (0, SLOT - QK_NOPE), (0, 0)))
    wuv = jnp.transpose(w_uv, (1, 0, 2))
    return dict(
        w_in=w_in_e.astype(BF16), qn=q_norm.reshape(1, -1), kvn=kv_norm.reshape(1, -1),
        wqa=_slot_cols(wq, QK_NOPE + QK_ROPE).astype(BF16), wqb=_slot_cols(wqb, QK_NOPE + QK_ROPE).astype(BF16),
        wk=_slot_cols(w_uk, QK_NOPE).astype(BF16), wv=_slot_cols(w_uv, V_DIM).astype(BF16), vone=vone,
        wabs=wabs.astype(BF16), wuv=wuv.astype(BF16), wo=w_o.astype(BF16))


def _cmul(a, b):
    return a[0] * b[0] - a[1] * b[1], a[0] * b[1] + a[1] * b[0]


def _ssm_tables(a_re, a_im, log_dt, b_re, b_im, c_re, c_im, levels):
    t = SSM_T
    g, p = a_re.shape
    c = SSM_GROUP_CH
    hi = lax.Precision.HIGHEST
    dt = jnp.exp(log_dt)[:, None]
    mag = jnp.exp(a_re * dt)
    lam_bar = (mag * jnp.cos(a_im * dt), mag * jnp.sin(a_im * dt))
    den = a_re * a_re + a_im * a_im
    quo = (((lam_bar[0] - 1.0) * a_re + lam_bar[1] * a_im) / den,
           (lam_bar[1] * a_re - (lam_bar[0] - 1.0) * a_im) / den)
    b_bar = _cmul((quo[0][:, :, None], quo[1][:, :, None]), (b_re, b_im))
    pw = [(jnp.ones_like(a_re), jnp.zeros_like(a_re))]
    for _ in range(t):
        pw.append(_cmul(pw[-1], lam_bar))
    pw_r = jnp.stack([x[0] for x in pw])
    pw_i = jnp.stack([x[1] for x in pw])
    cp = _cmul((c_re[None], c_im[None]), (pw_r[:t, :, None, :], pw_i[:t, :, None, :]))
    taps = (jnp.einsum("tgop,gpi->tgoi", cp[0], b_bar[0], precision=hi)
            - jnp.einsum("tgop,gpi->tgoi", cp[1], b_bar[1], precision=hi))
    nblk = g // SSM_GB
    width = SSM_GB * c
    tg = jnp.transpose(taps.reshape(t, nblk, SSM_GB, c, c), (1, 0, 2, 4, 3))
    own = np.zeros((SSM_GB, c, SSM_GB, c), np.float32)
    for g8 in range(SSM_GB):
        own[g8, np.arange(c), g8, np.arange(c)] = 1.0
    own = jnp.asarray(own.reshape(SSM_GB, c, width), BF16)
    bd = jnp.einsum("btgio,gol->btgil", tg.astype(BF16), own, preferred_element_type=BF16)
    bd = bd.reshape(nblk, t, width, width)
    lanes = (SSM_GB // 2) * SSM_PAIR
    place = np.zeros((SSM_GB, 2, p, SSM_GB // 2, 2, 2, p), np.float32)
    for g8 in range(SSM_GB):
        for ri in range(2):
            place[g8, ri, np.arange(p), g8 // 2, ri, g8 % 2, np.arange(p)] = 1.0
    place = jnp.asarray(place.reshape(SSM_GB, 2 * p, lanes), BF16)

    def spread(v):
        v = jnp.transpose(v, (2, 1, 3, 5, 0, 4)).reshape(nblk, t, SSM_GB, c, 2 * p)
        out = jnp.einsum("btgck,gkl->btgcl", v.astype(BF16), place, preferred_element_type=BF16)
        return out.reshape(nblk, t, width, lanes)

    rev = t - 1 - jnp.arange(t)
    bfl = _cmul((pw_r[rev][:, :, :, None], pw_i[rev][:, :, :, None]), (b_bar[0][None], b_bar[1][None]))
    bst = spread(jnp.stack(bfl).reshape(2, t, nblk, SSM_GB, p, c))
    cfl = _cmul((c_re[None], c_im[None]), (pw_r[1:t + 1, :, None, :], pw_i[1:t + 1, :, None, :]))
    cv = jnp.stack([cfl[0], -cfl[1]]).reshape(2, t, nblk, SSM_GB, c, p)
    cst = spread(jnp.swapaxes(cv, 4, 5))
    ap = [pw[t]]
    for _ in range(levels - 1):
        ap.append(_cmul(ap[-1], ap[-1]))

    def pair_lanes(xs):
        v = jnp.stack(xs, axis=1).reshape(g // 2, 2, levels, p)
        return jnp.transpose(v, (0, 2, 1, 3)).reshape(g // 2, levels, 2 * p)

    apow = jnp.concatenate([pair_lanes([x[0] for x in ap]), pair_lanes([x[1] for x in ap])], axis=2)
    rows = -(-levels // 8) * 8
    apow = jnp.pad(apow, ((0, 0), (0, rows - levels), (0, 0)))
    return dict(bd=bd.astype(BF16), bst=bst.astype(BF16), cst=cst.astype(BF16), apow=_block_pairs(apow))


def _block_pairs(v):
    pairs, r, w = v.shape
    per = SSM_GB // 2
    return jnp.transpose(v.reshape(pairs // per, per, r, w), (0, 2, 1, 3)).reshape(pairs // per, r, per * w)


def _unblock_pairs(v):
    nblk, r, w = v.shape
    per = SSM_GB // 2
    return jnp.transpose(v.reshape(nblk, r, per, w // per), (0, 2, 1, 3)).reshape(nblk * per, r, w // per)


def _pair_states(re, im):
    b, g, p = re.shape
    r = jnp.transpose(re.reshape(b, g // 2, 2 * p), (1, 0, 2))
    i = jnp.transpose(im.reshape(b, g // 2, 2 * p), (1, 0, 2))
    return jnp.concatenate([r, i], axis=2)


def _unpair_states(h):
    pairs, b, w = h.shape
    p = w // 4
    r = jnp.transpose(h[:, :, :2 * p], (1, 0, 2)).reshape(b, pairs * 2, p)
    i = jnp.transpose(h[:, :, 2 * p:], (1, 0, 2)).reshape(b, pairs * 2, p)
    return r, i


def _layer_tables(i, ln1_g, ln1_b, ln2_g, ln2_b, w_rg, b_rg, w_re, b_re, w_gate, w_up, w_down, w_proj, w_pg):
    d = w_rg.shape[1]
    wr = jnp.concatenate([w_rg[i], jnp.transpose(w_re[i], (1, 0, 2)).reshape(d, N_EXPERTS)], axis=1)
    wr = jnp.pad(wr, ((0, 0), (0, ROUTER_LANES - wr.shape[1])))
    wr_hi = wr.astype(BF16)
    wr_lo = (wr - wr_hi.astype(F32)).astype(BF16)
    br = jnp.pad(jnp.concatenate([b_rg[i], b_re[i].reshape(-1)]), (0, ROUTER_LANES - MOE_GROUPS - N_EXPERTS))
    ff = w_gate.shape[-1]
    return dict(
        ln1_g=ln1_g[i].reshape(1, d), ln1_b=ln1_b[i].reshape(1, d), ln2_g=ln2_g[i].reshape(1, d),
        ln2_b=ln2_b[i].reshape(1, d), wr_hi=wr_hi, wr_lo=wr_lo, br=br.reshape(1, ROUTER_LANES),
        wg=w_gate[i].reshape(N_EXPERTS, d, ff).astype(BF16), wu=w_up[i].reshape(N_EXPERTS, d, ff).astype(BF16),
        wd=w_down[i].reshape(N_EXPERTS, ff, d).astype(BF16),
        w_proj=w_proj[i].astype(BF16), w_pg=w_pg[i].astype(BF16))


def kernel(x_prompt, x_sample, p_prompt, p_sample, cache_mla_ckv, cache_mla_kpe, state_ssm_re, state_ssm_im, mla_w_in, mla_q_norm, mla_kv_norm, mla_w_uq, mla_w_uk, mla_w_uv, mla_w_o, ssm_w_in, ssm_a_re, ssm_a_im, ssm_log_dt, ssm_b_re, ssm_b_im, ssm_c_re, ssm_c_im, ssm_d, ssm_w_glu, ln1_g, ln1_b, ln2_g, ln2_b, moe_w_rg, moe_b_rg, moe_w_re, moe_b_re, moe_w_gate, moe_w_up, moe_w_down, ple_w_proj, ple_w_gate):
    bp, n_prompt, d = x_prompt.shape
    nb, seq, _ = x_sample.shape
    past = cache_mla_ckv.shape[2]
    assert bp == 1 and seq == SSM_T and n_prompt % ROW_TILE == 0 and (nb * seq) % ROW_TILE == 0
    assert n_prompt % CHUNK == 0 and past % CHUNK == 0 and seq <= CHUNK
    n_samp = nb * seq
    n = n_prompt + n_samp
    xp = x_prompt.reshape(n_prompt, d)
    xs = x_sample.reshape(n_samp, d)
    pp = p_prompt.reshape(DEPTH, n_prompt, -1)
    ps = p_sample.reshape(DEPTH, n_samp, -1)
    layer_args = (ln1_g, ln1_b, ln2_g, ln2_b, moe_w_rg, moe_b_rg, moe_w_re, moe_b_re, moe_w_gate, moe_w_up,
                  moe_w_down, ple_w_proj, ple_w_gate)
    lw = _layer_tables(0, *layer_args)
    mw = _mla_tables(mla_w_in[0], mla_q_norm[0], mla_kv_norm[0], mla_w_uq[0], mla_w_uk[0], mla_w_uv[0], mla_w_o[0])
    pos = jnp.concatenate([jnp.arange(n_prompt, dtype=jnp.int32),
                           past + jnp.tile(jnp.arange(seq, dtype=jnp.int32), nb)])
    ct, st = _rope_tables(pos)
    q, k, v, ckv, kpe_slot = _mla_proj(xp, xs, ct, st, mw)
    o_prompt = _prompt_attention(q, k, v, n_prompt)
    o_sample = _sample_attention(q, cache_mla_ckv[0], cache_mla_kpe[0], ckv, kpe_slot, mw["wabs"], mw["wuv"],
                                 n_prompt)
    h, hb, route = _attn_out(o_prompt, o_sample, xp, xs, mw["wo"], lw)
    x = _ffn(h, hb, route, pp[0], ps[0], lw)
    kpe = kpe_slot[:, QK_NOPE:QK_NOPE + QK_ROPE]

    lw = _layer_tables(1, *layer_args)
    n_chunks = n_prompt // SSM_T
    levels = max(1, (n_chunks - 1).bit_length())
    tabs = _ssm_tables(ssm_a_re[0], ssm_a_im[0], ssm_log_dt[0], ssm_b_re[0], ssm_b_im[0], ssm_c_re[0],
                       ssm_c_im[0], levels)
    u = _ssm_in(x, ssm_w_in[0].astype(BF16))
    h0 = _block_pairs(_pair_states(state_ssm_re[0], state_ssm_im[0]))
    y, hp, hs = _ssm_core(u, tabs, h0, n_chunks)
    h, hb, route = _ssm_out(y, u, x, ssm_d[0].reshape(1, d), ssm_w_glu[0].astype(BF16), lw)
    y_prompt, y_sample = _ffn(h, hb, route, pp[1], ps[1], lw, split_out=True)
    re_p, im_p = _unpair_states(_unblock_pairs(hp))
    re_s, im_s = _unpair_states(_unblock_pairs(hs))

    return (y_prompt.reshape(1, n_prompt, d), y_sample.reshape(nb, seq, d),
            ckv[:n_prompt].reshape(1, 1, n_prompt, KV_LORA), kpe[:n_prompt].reshape(1, 1, n_prompt, QK_ROPE),
            re_p[None], im_p[None],
            ckv[n_prompt:].reshape(1, nb, seq, KV_LORA), kpe[n_prompt:].reshape(1, nb, seq, QK_ROPE),
            re_s[None], im_s[None])
```

```python
import functools
import math

import jax
import jax.numpy as jnp
import numpy as np
from jax import lax
from jax.experimental import pallas as pl
from jax.experimental.pallas import tpu as pltpu

F32 = jnp.float32
BF16 = jnp.bfloat16

N_HEADS = 16
Q_LORA = 256
KV_LORA = 128
QK_NOPE = 64
QK_ROPE = 32
V_DIM = 64
ROPE_BASE = 10000.0
ATTN_SCALE = (QK_NOPE + QK_ROPE) ** -0.5
CHUNK = 64
SSM_GROUP_CH = 16
SSM_STATE = 64
MOE_GROUPS = 4
MOE_EPG = 8
N_EXPERTS = MOE_GROUPS * MOE_EPG
DEPTH = 2
ALPHA = (2.0 * DEPTH) ** 0.25
LN_EPS = 1e-5
RMS_EPS = 1e-6
NEG = -1e30

LANES = 128
SLOT = LANES
ROW_TILE = 256
SSM_T = 16
VMEM_LIMIT = 56 * 1024 * 1024
ROUTER_LANES = LANES
EXPERT_LANE0 = MOE_GROUPS


def _cparams(sem):
    return pltpu.CompilerParams(dimension_semantics=sem, vmem_limit_bytes=VMEM_LIMIT)


def _dot(a, b):
    return jnp.dot(a, b, preferred_element_type=F32)


def _dot_nt(a, b):
    return lax.dot_general(a, b, (((1,), (1,)), ((), ())), preferred_element_type=F32)


def _full(shape):
    nd = len(shape)
    return pl.BlockSpec(shape, lambda *_: (0,) * nd)


def _rows(width, rows=ROW_TILE):
    return pl.BlockSpec((rows, width), lambda i: (i, 0))


def _split_rows(width, prompt_tiles):
    return [pl.BlockSpec((ROW_TILE, width), lambda i: (jnp.minimum(i, prompt_tiles - 1), 0)),
            pl.BlockSpec((ROW_TILE, width), lambda i: (jnp.maximum(i - prompt_tiles, 0), 0))]


def _pick_rows(prompt_ref, sample_ref, prompt_tiles):
    return jnp.where(pl.program_id(0) < prompt_tiles, prompt_ref[...], sample_ref[...])


def _layernorm(t, g, b):
    mu = jnp.mean(t, axis=-1, keepdims=True)
    d = t - mu
    var = jnp.mean(d * d, axis=-1, keepdims=True)
    return d * lax.rsqrt(var + LN_EPS) * g + b


def _rmsnorm(t, g):
    return t * lax.rsqrt(jnp.mean(t * t, axis=-1, keepdims=True) + RMS_EPS) * g


def _mla_proj_kernel(xp_ref, xs_ref, ct_ref, st_ref, w_in_ref, qn_ref, kvn_ref, wqa_ref, wqb_ref, wk_ref, wv_ref,
                     vone_ref, q_ref, k_ref, v_ref, ckv_ref, kpe_ref, *, prompt_tiles):
    x = _pick_rows(xp_ref, xs_ref, prompt_tiles).astype(BF16)
    z = _dot(x, w_in_ref[...])
    cq = _rmsnorm(z[:, :Q_LORA], qn_ref[...])
    ckv = _rmsnorm(z[:, Q_LORA:Q_LORA + KV_LORA], kvn_ref[...])
    ct = ct_ref[...]
    st = st_ref[...]
    o = Q_LORA + KV_LORA
    kpe = z[:, o:o + SLOT] * ct + z[:, o + SLOT:o + 2 * SLOT] * st
    ckv_ref[...] = ckv
    kpe_ref[...] = kpe
    cqb = cq.astype(BF16)
    ckvb = ckv.astype(BF16)
    qa = _dot(cqb, wqa_ref[...])
    qb = _dot(cqb, wqb_ref[...])
    kn = _dot(ckvb, wk_ref[...])
    vv = _dot(ckvb, wv_ref[...]) + vone_ref[...]
    scale = ATTN_SCALE * math.log2(math.e)
    for h in range(N_HEADS):
        sl = slice(h * SLOT, (h + 1) * SLOT)
        q_ref[:, sl] = ((qa[:, sl] * ct + qb[:, sl] * st) * scale).astype(BF16)
        k_ref[:, sl] = (kn[:, sl] + kpe).astype(BF16)
    v_ref[...] = vv.astype(BF16)


def _mla_proj(xp, xs, ct, st, w):
    n = xp.shape[0] + xs.shape[0]
    pt = xp.shape[0] // ROW_TILE
    wide = N_HEADS * SLOT
    return pl.pallas_call(
        functools.partial(_mla_proj_kernel, prompt_tiles=pt),
        grid=(n // ROW_TILE,),
        in_specs=_split_rows(xp.shape[1], pt) + [_rows(SLOT), _rows(SLOT), _full(w["w_in"].shape), _full((1, Q_LORA)),
                  _full((1, KV_LORA)), _full(w["wqa"].shape), _full(w["wqb"].shape), _full(w["wk"].shape),
                  _full(w["wv"].shape), _full((1, wide))],
        out_specs=[_rows(wide), _rows(wide), _rows(wide), _rows(KV_LORA), _rows(SLOT)],
        out_shape=[jax.ShapeDtypeStruct((n, wide), BF16)] * 3
        + [jax.ShapeDtypeStruct((n, KV_LORA), F32), jax.ShapeDtypeStruct((n, SLOT), F32)],
        compiler_params=_cparams(("parallel",)),
        name="mla_proj",
    )(xp, xs, ct, st, w["w_in"], w["qn"], w["kvn"], w["wqa"], w["wqb"], w["wk"], w["wv"], w["vone"])


ATT_TQ = 1024
ATT_TK = 2048
ATT_G = 2


def _attn_kernel(q_ref, k_ref, v_ref, o_ref):
    qi = pl.program_id(1)
    slots = [slice(g * SLOT, (g + 1) * SLOT) for g in range(ATT_G)]
    ratio = ATT_TK // ATT_TQ

    def update(off, tk, state, mask):
        out = []
        for sl, (m, acc) in zip(slots, state):
            s = _dot_nt(q_ref[:, sl], k_ref[pl.ds(off, tk), sl])
            if mask is not None:
                s = jnp.where(mask, s, NEG)
            m_new = jnp.maximum(m, jnp.max(s, axis=-1, keepdims=True))
            p = jnp.exp2(s - m_new)
            pv = _dot(p.astype(BF16), v_ref[pl.ds(off, tk), sl])
            out.append((m_new, acc * jnp.exp2(m - m_new) + pv))
        return tuple(out)

    def body(j, state):
        return update(pl.multiple_of(j * ATT_TK, ATT_TK), ATT_TK, state, None)

    init = tuple((jnp.full((ATT_TQ, 1), NEG, F32), jnp.zeros((ATT_TQ, SLOT), F32)) for _ in range(ATT_G))
    state = lax.fori_loop(0, qi // ratio, body, init)

    def tail(rem):
        tk = (rem + 1) * ATT_TQ
        rc = lax.broadcasted_iota(jnp.int32, (ATT_TQ, tk), 0) // CHUNK
        cc = lax.broadcasted_iota(jnp.int32, (ATT_TQ, tk), 1) // CHUNK - rem * (ATT_TQ // CHUNK)
        return lambda st: update(pl.multiple_of((qi - rem) * ATT_TQ, ATT_TQ), tk, st, cc <= rc)

    state = lax.switch(qi % ratio, [tail(rem) for rem in range(ratio)], state)
    lane = lax.broadcasted_iota(jnp.int32, (ATT_TQ, SLOT), 1)
    outs = []
    for g in range(ATT_G):
        acc = state[g][1]
        l = jnp.sum(jnp.where(lane == V_DIM, acc, 0.0), axis=-1, keepdims=True)
        outs.append((acc / l)[:, :V_DIM])
    o_ref[...] = jnp.concatenate(outs, axis=1).astype(BF16)


def _prompt_attention(q, k, v, n_prompt):
    n = q.shape[0]
    wide = ATT_G * SLOT
    assert ATT_TK % ATT_TQ == 0 and ATT_TQ % CHUNK == 0 and n_prompt % ATT_TQ == 0 and N_HEADS % ATT_G == 0
    assert (ATT_G * V_DIM) % LANES == 0
    return pl.pallas_call(
        _attn_kernel,
        grid=(N_HEADS // ATT_G, n_prompt // ATT_TQ),
        in_specs=[pl.BlockSpec((ATT_TQ, wide), lambda h, i: (i, h)),
                  pl.BlockSpec((n_prompt, wide), lambda h, i: (0, h)),
                  pl.BlockSpec((n_prompt, wide), lambda h, i: (0, h))],
        out_specs=pl.BlockSpec((ATT_TQ, ATT_G * V_DIM), lambda h, i: (i, h)),
        out_shape=jax.ShapeDtypeStruct((n_prompt, N_HEADS * V_DIM), BF16),
        compiler_params=_cparams(("parallel", "arbitrary")),
        name="prompt_attention",
    )(q, k, v)


def _sample_attn_kernel(q_ref, ckvc_ref, kpec_ref, ckvn_ref, kpen_ref, wabs_ref, wuv_ref, o_ref):
    q = q_ref[...]
    seq = q.shape[0]
    qa, qp = [], []
    for h in range(N_HEADS):
        qh = q[:, h * SLOT:(h + 1) * SLOT]
        qa.append(_dot(qh, wabs_ref[h]))
        qp.append(qh[:, QK_NOPE:QK_NOPE + QK_ROPE])
    qa = jnp.concatenate(qa, axis=0).astype(BF16)
    qp = jnp.concatenate(qp, axis=0)
    ckvc = ckvc_ref[0].astype(BF16)
    kpec = kpec_ref[0].astype(BF16)
    ckvn = ckvn_ref[...].astype(BF16)
    kpen = kpen_ref[...][:, QK_NOPE:QK_NOPE + QK_ROPE].astype(BF16)
    s_c = _dot_nt(qa, ckvc) + _dot_nt(qp, kpec)
    s_n = _dot_nt(qa, ckvn) + _dot_nt(qp, kpen)
    m = jnp.maximum(jnp.max(s_c, axis=-1, keepdims=True), jnp.max(s_n, axis=-1, keepdims=True))
    p_c = jnp.exp2(s_c - m)
    p_n = jnp.exp2(s_n - m)
    l = jnp.sum(p_c, axis=-1, keepdims=True) + jnp.sum(p_n, axis=-1, keepdims=True)
    ol = (_dot(p_c.astype(BF16), ckvc) + _dot(p_n.astype(BF16), ckvn)) / l
    olb = ol.astype(BF16)
    outs = [_dot(olb[h * seq:(h + 1) * seq], wuv_ref[h]) for h in range(N_HEADS)]
    o_ref[...] = jnp.concatenate(outs, axis=1).astype(BF16)


def _sample_attention(q, cache_ckv, cache_kpe, ckv_new, kpe_new, wabs, wuv, n_prompt):
    nb, past, _ = cache_ckv.shape
    seq = (q.shape[0] - n_prompt) // nb
    base = n_prompt // seq
    wide = N_HEADS * SLOT
    return pl.pallas_call(
        _sample_attn_kernel,
        grid=(nb,),
        in_specs=[pl.BlockSpec((seq, wide), lambda b: (base + b, 0)),
                  pl.BlockSpec((1, past, KV_LORA), lambda b: (b, 0, 0)),
                  pl.BlockSpec((1, past, QK_ROPE), lambda b: (b, 0, 0)),
                  pl.BlockSpec((seq, KV_LORA), lambda b: (base + b, 0)),
                  pl.BlockSpec((seq, SLOT), lambda b: (base + b, 0)),
                  _full(wabs.shape), _full(wuv.shape)],
        out_specs=pl.BlockSpec((seq, N_HEADS * V_DIM), lambda b: (b, 0)),
        out_shape=jax.ShapeDtypeStruct((nb * seq, N_HEADS * V_DIM), BF16),
        compiler_params=_cparams(("parallel",)),
        name="sample_attention",
    )(q, cache_ckv, cache_kpe, ckv_new, kpe_new, wabs, wuv)


ROUTE_E1, ROUTE_E2, ROUTE_W1, ROUTE_W2 = range(4)


def _lane_col(tile, k):
    lane = lax.broadcasted_iota(jnp.int32, tile.shape, 1)
    return jnp.sum(jnp.where(lane == k, tile, 0.0), axis=-1, keepdims=True)


def _route(lg):
    lane = lax.broadcasted_iota(jnp.int32, lg.shape, 1)
    lanef = lane.astype(F32)
    big = float(ROUTER_LANES)
    is_g = lane < MOE_GROUPS
    gl = jnp.where(is_g, lg, NEG)
    gmax = jnp.max(gl, axis=-1, keepdims=True)
    gsel = jnp.min(jnp.where(gl == gmax, lanef, big), axis=-1, keepdims=True)
    gate_g = 1.0 / jnp.sum(jnp.where(is_g, jnp.exp(gl - gmax), 0.0), axis=-1, keepdims=True)
    lo = EXPERT_LANE0 + MOE_EPG * gsel
    el = jnp.where(jnp.logical_and(lanef >= lo, lanef < lo + MOE_EPG), lg, NEG)
    m1 = jnp.max(el, axis=-1, keepdims=True)
    i1 = jnp.min(jnp.where(el == m1, lanef, big), axis=-1, keepdims=True)
    el2 = jnp.where(lanef == i1, NEG, el)
    m2 = jnp.max(el2, axis=-1, keepdims=True)
    i2 = jnp.min(jnp.where(el2 == m2, lanef, big), axis=-1, keepdims=True)
    r = jnp.exp(m2 - m1)
    w1 = 1.0 / (1.0 + r)
    w2 = r / (1.0 + r)
    cols = {ROUTE_E1: i1 - EXPERT_LANE0, ROUTE_E2: i2 - EXPERT_LANE0, ROUTE_W1: gate_g * w1, ROUTE_W2: gate_g * w2}
    route = jnp.zeros(lg.shape, F32)
    for k, c in cols.items():
        route = jnp.where(lane == k, c, route)
    return route


def _post_mix(t, x, g, b, wr_hi, wr_lo, br, h_ref, hb_ref, route_ref):
    h = _layernorm(ALPHA * x + t, g, b)
    h_ref[...] = h
    hb = h.astype(BF16)
    hb_ref[...] = hb
    h_lo = (h - hb.astype(F32)).astype(BF16)
    lg = _dot(hb, wr_hi) + (_dot(hb, wr_lo) + _dot(h_lo, wr_hi)) + br
    route_ref[...] = _route(lg)


def _attn_out_kernel(op_ref, os_ref, xp_ref, xs_ref, wo_ref, g_ref, b_ref, wrh_ref, wrl_ref, br_ref, h_ref, hb_ref,
                     route_ref, *, prompt_tiles):
    t = _dot(_pick_rows(op_ref, os_ref, prompt_tiles), wo_ref[...])
    _post_mix(t, _pick_rows(xp_ref, xs_ref, prompt_tiles), g_ref[...], b_ref[...], wrh_ref[...], wrl_ref[...],
              br_ref[...], h_ref, hb_ref, route_ref)


def _post_out(n, d):
    specs = [_rows(d), _rows(d), _rows(ROUTER_LANES)]
    shapes = [jax.ShapeDtypeStruct((n, d), F32), jax.ShapeDtypeStruct((n, d), BF16),
              jax.ShapeDtypeStruct((n, ROUTER_LANES), F32)]
    return specs, shapes


def _attn_out(o_prompt, o_sample, xp, xs, wo, lw):
    d = xp.shape[1]
    n = xp.shape[0] + xs.shape[0]
    pt = xp.shape[0] // ROW_TILE
    specs, shapes = _post_out(n, d)
    return pl.pallas_call(
        functools.partial(_attn_out_kernel, prompt_tiles=pt),
        grid=(n // ROW_TILE,),
        in_specs=_split_rows(o_prompt.shape[1], pt) + _split_rows(d, pt)
        + [_full(wo.shape), _full((1, d)), _full((1, d)),
           _full(lw["wr_hi"].shape), _full(lw["wr_lo"].shape), _full((1, ROUTER_LANES))],
        out_specs=specs, out_shape=shapes,
        compiler_params=_cparams(("parallel",)),
        name="attn_out_ln_router",
    )(o_prompt, o_sample, xp, xs, wo, lw["ln1_g"], lw["ln1_b"], lw["wr_hi"], lw["wr_lo"], lw["br"])


MOE_EPS = 8
MOE_TILE_MAX = 896


def _moe_dense_kernel(hb_ref, route_ref, wg_ref, wu_ref, wd_ref, y_ref):
    x = hb_ref[...]
    route = route_ref[...]
    e1 = _lane_col(route, ROUTE_E1)
    e2 = _lane_col(route, ROUTE_E2)
    w1 = _lane_col(route, ROUTE_W1)
    w2 = _lane_col(route, ROUTE_W2)
    y = None
    for k in range(MOE_EPS):
        e = (pl.program_id(1) * MOE_EPS + k).astype(F32)
        g = _dot(x, wg_ref[k])
        u = _dot(x, wu_ref[k])
        gate = jnp.where(e1 == e, w1, 0.0) + jnp.where(e2 == e, w2, 0.0)
        hdn = (g * jax.nn.sigmoid(g)) * u * gate
        yk = _dot(hdn.astype(BF16), wd_ref[k])
        y = yk if y is None else y + yk

    @pl.when(pl.program_id(1) == 0)
    def _():
        y_ref[...] = y

    @pl.when(pl.program_id(1) > 0)
    def _():
        y_ref[...] += y


def _moe_tile(n):
    return max(t for t in range(16, MOE_TILE_MAX + 1, 16) if n % t == 0)


def _moe_dense(hb, route, wg, wu, wd):
    n, d = hb.shape
    ff = wg.shape[-1]
    tile = _moe_tile(n)
    return pl.pallas_call(
        _moe_dense_kernel,
        grid=(n // tile, N_EXPERTS // MOE_EPS),
        in_specs=[pl.BlockSpec((tile, d), lambda i, e: (i, 0)),
                  pl.BlockSpec((tile, ROUTER_LANES), lambda i, e: (i, 0)),
                  pl.BlockSpec((MOE_EPS, d, ff), lambda i, e: (e, 0, 0)),
                  pl.BlockSpec((MOE_EPS, d, ff), lambda i, e: (e, 0, 0)),
                  pl.BlockSpec((MOE_EPS, ff, d), lambda i, e: (e, 0, 0))],
        out_specs=pl.BlockSpec((tile, d), lambda i, e: (i, 0)),
        out_shape=jax.ShapeDtypeStruct((n, d), F32),
        compiler_params=_cparams(("parallel", "arbitrary")),
        name="moe_dense",
    )(hb, route, wg, wu, wd)


def _ln2_ple_kernel(h_ref, y_ref, pp_ref, ps_ref, g_ref, b_ref, wp_ref, wpg_ref, *out_refs, prompt_tiles):
    h2 = _layernorm(ALPHA * h_ref[...] + y_ref[...], g_ref[...], b_ref[...])
    proj = _dot(_pick_rows(pp_ref, ps_ref, prompt_tiles).astype(BF16), wp_ref[...])
    gate = jax.nn.sigmoid(_dot(h2.astype(BF16), wpg_ref[...]))
    x = h2 + proj * gate
    if len(out_refs) == 1:
        out_refs[0][...] = x
    else:
        is_prompt = pl.program_id(0) < prompt_tiles

        @pl.when(is_prompt)
        def _():
            out_refs[0][...] = x

        @pl.when(jnp.logical_not(is_prompt))
        def _():
            out_refs[1][...] = x


def _ln2_ple(h, y, pp, ps, lw, split_out):
    n, d = h.shape
    pt = pp.shape[0] // ROW_TILE
    if split_out:
        out_specs = _split_rows(d, pt)
        out_shape = [jax.ShapeDtypeStruct((pp.shape[0], d), F32), jax.ShapeDtypeStruct((ps.shape[0], d), F32)]
    else:
        out_specs, out_shape = _rows(d), jax.ShapeDtypeStruct((n, d), F32)
    return pl.pallas_call(
        functools.partial(_ln2_ple_kernel, prompt_tiles=pt),
        grid=(n // ROW_TILE,),
        in_specs=[_rows(d), _rows(d)] + _split_rows(pp.shape[1], pt)
        + [_full((1, d)), _full((1, d)), _full(lw["w_proj"].shape), _full(lw["w_pg"].shape)],
        out_specs=out_specs, out_shape=out_shape,
        compiler_params=_cparams(("arbitrary",)),
        name="ln2_ple",
    )(h, y, pp, ps, lw["ln2_g"], lw["ln2_b"], lw["w_proj"], lw["w_pg"])


def _ffn(h, hb, route, pp, ps, lw, split_out=False):
    y = _moe_dense(hb, route, lw["wg"], lw["wu"], lw["wd"])
    return _ln2_ple(h, y, pp, ps, lw, split_out)


def _ssm_in_kernel(x_ref, w_ref, u_ref):
    u_ref[...] = _dot(x_ref[...].astype(BF16), w_ref[...])


def _ssm_in(x, w):
    n, d = x.shape
    return pl.pallas_call(
        _ssm_in_kernel,
        grid=(n // ROW_TILE,),
        in_specs=[_rows(d), _full(w.shape)],
        out_specs=_rows(w.shape[1]),
        out_shape=jax.ShapeDtypeStruct((n, w.shape[1]), F32),
        compiler_params=_cparams(("parallel",)),
        name="ssm_in",
    )(x, w)


SSM_GB = LANES // SSM_GROUP_CH
SSM_PAIR = 2 * LANES


def _ssm_core_kernel(u_ref, bd_ref, bst_ref, cst_ref, apow_ref, h0_ref, y_ref, hp_ref, hs_ref, sre_ref, sim_ref,
                     *, n_chunks, levels):
    t = SSM_T
    rows = u_ref.shape[0] // t
    pad = sre_ref.shape[0] - n_chunks
    npair = SSM_GB // 2
    ut = [u_ref[pl.ds(tl, rows, stride=t), :].astype(BF16) for tl in range(t)]
    u2 = [jnp.concatenate([ut[2 * k], ut[2 * k + 1]], axis=1) for k in range(t // 2)]
    bb = None
    for k in range(t // 2):
        part = _dot(u2[k], jnp.concatenate([bst_ref[0, 2 * k], bst_ref[0, 2 * k + 1]], axis=0))
        bb = part if bb is None else bb + part
    zeros = jnp.zeros((pad, LANES), F32)
    sre_ref[:pad, :] = zeros
    sim_ref[:pad, :] = zeros

    def shifted(d):
        return sre_ref[pl.ds(pad - d, n_chunks), :], sim_ref[pl.ds(pad - d, n_chunks), :]

    h0 = h0_ref[0]
    hprev, h_last, h_samp = [], [], []
    for j in range(npair):
        lo = j * SSM_PAIR
        re = bb[:n_chunks, lo:lo + LANES]
        im = bb[:n_chunks, lo + LANES:lo + SSM_PAIR]
        for lv in range(levels):
            sre_ref[pad:, :] = re
            sim_ref[pad:, :] = im
            pr, pi = shifted(1 << lv)
            ar = apow_ref[0, lv:lv + 1, lo:lo + LANES]
            ai = apow_ref[0, lv:lv + 1, lo + LANES:lo + SSM_PAIR]
            re, im = re + ar * pr - ai * pi, im + ar * pi + ai * pr
        sre_ref[pad:, :] = re
        sim_ref[pad:, :] = im
        pr, pi = shifted(1)
        h0j = h0[:, lo:lo + SSM_PAIR]
        hprev.append(jnp.concatenate([jnp.concatenate([pr, pi], axis=1), h0j], axis=0).astype(BF16))
        h_last.append(jnp.concatenate([re[n_chunks - 1:, :], im[n_chunks - 1:, :]], axis=1))
        ar = apow_ref[0, 0:1, lo:lo + LANES]
        ai = apow_ref[0, 0:1, lo + LANES:lo + SSM_PAIR]
        h0r = h0j[:, :LANES]
        h0i = h0j[:, LANES:]
        h_samp.append(jnp.concatenate([ar * h0r - ai * h0i + bb[n_chunks:, lo:lo + LANES],
                                       ar * h0i + ai * h0r + bb[n_chunks:, lo + LANES:lo + SSM_PAIR]], axis=1))
    hp_ref[0] = jnp.concatenate(h_last, axis=1)
    hs_ref[0] = jnp.concatenate(h_samp, axis=1)
    hprev = jnp.concatenate(hprev, axis=1)
    zero_tap = jnp.zeros((LANES, LANES), BF16)

    def tap(lag):
        return bd_ref[0, lag] if lag >= 0 else zero_tap

    for ko in range(t // 2):
        acc = _dot_nt(hprev, jnp.concatenate([cst_ref[0, 2 * ko], cst_ref[0, 2 * ko + 1]], axis=0))
        for ki in range(ko + 1):
            lag = 2 * (ko - ki)
            w = jnp.concatenate([jnp.concatenate([tap(lag), tap(lag + 1)], axis=1),
                                 jnp.concatenate([tap(lag - 1), tap(lag)], axis=1)], axis=0)
            acc = acc + _dot(u2[ki], w)
        y_ref[pl.ds(2 * ko, rows, stride=t), :] = acc[:, :LANES]
        y_ref[pl.ds(2 * ko + 1, rows, stride=t), :] = acc[:, LANES:]


def _ssm_core(u, tabs, h0, n_chunks):
    n, d = u.shape
    nblk = d // LANES
    ns = n // SSM_T - n_chunks
    levels = max(1, (n_chunks - 1).bit_length())
    pad = max(1 << (levels - 1), 8)
    wide = (SSM_GB // 2) * SSM_PAIR
    kern = functools.partial(_ssm_core_kernel, n_chunks=n_chunks, levels=levels)
    once = dict(pipeline_mode=pl.Buffered(1))

    def blk(*shape):
        return pl.BlockSpec((1,) + shape, lambda i: (i,) + (0,) * len(shape), **once)

    return pl.pallas_call(
        kern,
        grid=(nblk,),
        in_specs=[pl.BlockSpec((n, LANES), lambda i: (0, i), **once), blk(SSM_T, LANES, LANES),
                  blk(SSM_T, LANES, wide), blk(SSM_T, LANES, wide), blk(tabs["apow"].shape[1], wide), blk(ns, wide)],
        out_specs=[pl.BlockSpec((n, LANES), lambda i: (0, i)),
                   pl.BlockSpec((1, 1, wide), lambda i: (i, 0, 0)), pl.BlockSpec((1, ns, wide), lambda i: (i, 0, 0))],
        out_shape=[jax.ShapeDtypeStruct((n, d), F32), jax.ShapeDtypeStruct((nblk, 1, wide), F32),
                   jax.ShapeDtypeStruct((nblk, ns, wide), F32)],
        scratch_shapes=[pltpu.VMEM((pad + n_chunks, LANES), F32), pltpu.VMEM((pad + n_chunks, LANES), F32)],
        compiler_params=_cparams(("parallel",)),
        name="ssm_core",
    )(u, tabs["bd"], tabs["bst"], tabs["cst"], tabs["apow"], h0)


def _gelu_tanh(y):
    c = math.sqrt(2.0 / math.pi)
    return 0.5 * y * (1.0 + jnp.tanh(c * (y + 0.044715 * (y * y * y))))


def _ssm_out_kernel(y_ref, u_ref, x_ref, d_ref, wglu_ref, g_ref, b_ref, wrh_ref, wrl_ref, br_ref,
                    h_ref, hb_ref, route_ref):
    y = y_ref[...] + d_ref[...] * u_ref[...]
    z = _dot(_gelu_tanh(y).astype(BF16), wglu_ref[...])
    dm = z.shape[1] // 2
    t = z[:, :dm] * jax.nn.sigmoid(z[:, dm:])
    _post_mix(t, x_ref[...], g_ref[...], b_ref[...], wrh_ref[...], wrl_ref[...], br_ref[...],
              h_ref, hb_ref, route_ref)


def _ssm_out(y, u, x, d_skip, wglu, lw):
    n, d = x.shape
    specs, shapes = _post_out(n, d)
    return pl.pallas_call(
        _ssm_out_kernel,
        grid=(n // ROW_TILE,),
        in_specs=[_rows(d), _rows(d), _rows(d), _full((1, d)), _full(wglu.shape), _full((1, d)), _full((1, d)),
                  _full(lw["wr_hi"].shape), _full(lw["wr_lo"].shape), _full((1, ROUTER_LANES))],
        out_specs=specs, out_shape=shapes,
        compiler_params=_cparams(("parallel",)),
        name="ssm_out_ln_router",
    )(y, u, x, d_skip, wglu, lw["ln1_g"], lw["ln1_b"], lw["wr_hi"], lw["wr_lo"], lw["br"])


def _rope_tables(pos):
    half = QK_ROPE // 2
    inv = ROPE_BASE ** (-jnp.arange(half, dtype=F32) / half)
    n = pos.shape[0]
    per = LANES // half
    assert n % per == 0
    ang = (pos.astype(F32).reshape(n // per, per, 1) * inv.reshape(1, 1, half)).reshape(n // per, LANES)
    cos, sin = jnp.cos(ang).reshape(n, half), jnp.sin(ang).reshape(n, half)
    ones = jnp.ones((n, QK_NOPE), F32)
    zeros = jnp.zeros((n, SLOT - QK_NOPE - QK_ROPE), F32)
    ct = jnp.concatenate([ones, cos, cos, zeros], axis=1)
    st = jnp.concatenate([jnp.zeros((n, QK_NOPE), F32), -sin, sin, zeros], axis=1)
    return ct, st


def _slot_cols(w, width):
    k = w.shape[0]
    return jnp.pad(w, ((0, 0), (0, 0), (0, SLOT - width))).reshape(k, N_HEADS * SLOT)


def _mla_tables(w_in, q_norm, kv_norm, w_uq, w_uk, w_uv, w_o):
    half = QK_ROPE // 2
    o = Q_LORA + KV_LORA
    d = w_in.shape[0]
    kpe_w = w_in[:, o:]
    kpe_sw = jnp.concatenate([kpe_w[:, half:], kpe_w[:, :half]], axis=1)
    zl = jnp.zeros((d, QK_NOPE), F32)
    zr = jnp.zeros((d, SLOT - QK_NOPE - QK_ROPE), F32)
    w_in_e = jnp.concatenate([w_in[:, :o], zl, kpe_w, zr, zl, kpe_sw, zr], axis=1)
    wq = w_uq.reshape(Q_LORA, N_HEADS, QK_NOPE + QK_ROPE)
    pe = wq[:, :, QK_NOPE:]
    pe_sw = jnp.concatenate([pe[:, :, half:], pe[:, :, :half]], axis=2)
    wqb = jnp.concatenate([jnp.zeros_like(wq[:, :, :QK_NOPE]), pe_sw], axis=2)
    vone = jnp.zeros((N_HEADS, SLOT), F32).at[:, V_DIM].set(1.0).reshape(1, N_HEADS * SLOT)
    wabs = jnp.pad(jnp.transpose(w_uk, (1, 2, 0)), ((0, 0), (0, SLOT - QK_NOPE), (0, 0)))
    wuv = jnp.transpose(w_uv, (1, 0, 2))
    return dict(
        w_in=w_in_e.astype(BF16), qn=q_norm.reshape(1, -1), kvn=kv_norm.reshape(1, -1),
        wqa=_slot_cols(wq, QK_NOPE + QK_ROPE).astype(BF16), wqb=_slot_cols(wqb, QK_NOPE + QK_ROPE).astype(BF16),
        wk=_slot_cols(w_uk, QK_NOPE).astype(BF16), wv=_slot_cols(w_uv, V_DIM).astype(BF16), vone=vone,
        wabs=wabs.astype(BF16), wuv=wuv.astype(BF16), wo=w_o.astype(BF16))


def _cmul(a, b):
    return a[0] * b[0] - a[1] * b[1], a[0] * b[1] + a[1] * b[0]


def _ssm_tables(a_re, a_im, log_dt, b_re, b_im, c_re, c_im, levels):
    t = SSM_T
    g, p = a_re.shape
    c = SSM_GROUP_CH
    hi = lax.Precision.HIGHEST
    dt = jnp.exp(log_dt)[:, None]
    mag = jnp.exp(a_re * dt)
    lam_bar = (mag * jnp.cos(a_im * dt), mag * jnp.sin(a_im * dt))
    den = a_re * a_re + a_im * a_im
    quo = (((lam_bar[0] - 1.0) * a_re + lam_bar[1] * a_im) / den,
           (lam_bar[1] * a_re - (lam_bar[0] - 1.0) * a_im) / den)
    b_bar = _cmul((quo[0][:, :, None], quo[1][:, :, None]), (b_re, b_im))
    pw = [(jnp.ones_like(a_re), jnp.zeros_like(a_re))]
    for _ in range(t):
        pw.append(_cmul(pw[-1], lam_bar))
    pw_r = jnp.stack([x[0] for x in pw])
    pw_i = jnp.stack([x[1] for x in pw])
    cp = _cmul((c_re[None], c_im[None]), (pw_r[:t, :, None, :], pw_i[:t, :, None, :]))
    taps = jnp.einsum("tgok,gki->tgoi", jnp.concatenate([cp[0], -cp[1]], axis=-1),
                      jnp.concatenate([b_bar[0], b_bar[1]], axis=1), precision=hi)
    nblk = g // SSM_GB
    width = SSM_GB * c
    tg = jnp.transpose(taps.reshape(t, nblk, SSM_GB, c, c), (1, 0, 2, 4, 3))
    own = np.zeros((SSM_GB, c, SSM_GB, c), np.float32)
    for g8 in range(SSM_GB):
        own[g8, np.arange(c), g8, np.arange(c)] = 1.0
    own = jnp.asarray(own.reshape(SSM_GB, c, width), BF16)
    bd = jnp.einsum("btgio,gol->btgil", tg.astype(BF16), own, preferred_element_type=BF16)
    bd = bd.reshape(nblk, t, width, width)
    lanes = (SSM_GB // 2) * SSM_PAIR
    place = np.zeros((SSM_GB, 2, p, SSM_GB // 2, 2, 2, p), np.float32)
    for g8 in range(SSM_GB):
        for ri in range(2):
            place[g8, ri, np.arange(p), g8 // 2, ri, g8 % 2, np.arange(p)] = 1.0
    place = jnp.asarray(place.reshape(SSM_GB, 2 * p, lanes), BF16)

    def spread(v):
        v = jnp.transpose(v, (2, 1, 3, 5, 0, 4)).reshape(nblk, t, SSM_GB, c, 2 * p)
        out = jnp.einsum("btgck,gkl->btgcl", v.astype(BF16), place, preferred_element_type=BF16)
        return out.reshape(nblk, t, width, lanes)

    rev = t - 1 - jnp.arange(t)
    bfl = _cmul((pw_r[rev][:, :, :, None], pw_i[rev][:, :, :, None]), (b_bar[0][None], b_bar[1][None]))
    bst = spread(jnp.stack(bfl).reshape(2, t, nblk, SSM_GB, p, c))
    cfl = _cmul((c_re[None], c_im[None]), (pw_r[1:t + 1, :, None, :], pw_i[1:t + 1, :, None, :]))
    cv = jnp.stack([cfl[0], -cfl[1]]).reshape(2, t, nblk, SSM_GB, c, p)
    cst = spread(jnp.swapaxes(cv, 4, 5))
    ap = [pw[t]]
    for _ in range(levels - 1):
        ap.append(_cmul(ap[-1], ap[-1]))

    def pair_lanes(xs):
        v = jnp.stack(xs, axis=1).reshape(g // 2, 2, levels, p)
        return jnp.transpose(v, (0, 2, 1, 3)).reshape(g // 2, levels, 2 * p)

    apow = jnp.concatenate([pair_lanes([x[0] for x in ap]), pair_lanes([x[1] for x in ap])], axis=2)
    rows = -(-levels // 8) * 8
    apow = jnp.pad(apow, ((0, 0), (0, rows - levels), (0, 0)))
    return dict(bd=bd.astype(BF16), bst=bst.astype(BF16), cst=cst.astype(BF16), apow=_block_pairs(apow))


def _block_pairs(v):
    pairs, r, w = v.shape
    per = SSM_GB // 2
    return jnp.transpose(v.reshape(pairs // per, per, r, w), (0, 2, 1, 3)).reshape(pairs // per, r, per * w)


def _unblock_pairs(v):
    nblk, r, w = v.shape
    per = SSM_GB // 2
    return jnp.transpose(v.reshape(nblk, r, per, w // per), (0, 2, 1, 3)).reshape(nblk * per, r, w // per)


def _pair_states(re, im):
    b, g, p = re.shape
    r = jnp.transpose(re.reshape(b, g // 2, 2 * p), (1, 0, 2))
    i = jnp.transpose(im.reshape(b, g // 2, 2 * p), (1, 0, 2))
    return jnp.concatenate([r, i], axis=2)


def _unpair_states(h):
    pairs, b, w = h.shape
    p = w // 4
    r = jnp.transpose(h[:, :, :2 * p], (1, 0, 2)).reshape(b, pairs * 2, p)
    i = jnp.transpose(h[:, :, 2 * p:], (1, 0, 2)).reshape(b, pairs * 2, p)
    return r, i


def _layer_tables(i, ln1_g, ln1_b, ln2_g, ln2_b, w_rg, b_rg, w_re, b_re, w_gate, w_up, w_down, w_proj, w_pg):
    d = w_rg.shape[1]
    wr = jnp.concatenate([w_rg[i], jnp.transpose(w_re[i], (1, 0, 2)).reshape(d, N_EXPERTS)], axis=1)
    wr = jnp.pad(wr, ((0, 0), (0, ROUTER_LANES - wr.shape[1])))
    wr_hi = wr.astype(BF16)
    wr_lo = (wr - wr_hi.astype(F32)).astype(BF16)
    br = jnp.pad(jnp.concatenate([b_rg[i], b_re[i].reshape(-1)]), (0, ROUTER_LANES - MOE_GROUPS - N_EXPERTS))
    ff = w_gate.shape[-1]
    return dict(
        ln1_g=ln1_g[i].reshape(1, d), ln1_b=ln1_b[i].reshape(1, d), ln2_g=ln2_g[i].reshape(1, d),
        ln2_b=ln2_b[i].reshape(1, d), wr_hi=wr_hi, wr_lo=wr_lo, br=br.reshape(1, ROUTER_LANES),
        wg=w_gate[i].reshape(N_EXPERTS, d, ff).astype(BF16), wu=w_up[i].reshape(N_EXPERTS, d, ff).astype(BF16),
        wd=w_down[i].reshape(N_EXPERTS, ff, d).astype(BF16),
        w_proj=w_proj[i].astype(BF16), w_pg=w_pg[i].astype(BF16))


def kernel(x_prompt, x_sample, p_prompt, p_sample, cache_mla_ckv, cache_mla_kpe, state_ssm_re, state_ssm_im, mla_w_in, mla_q_norm, mla_kv_norm, mla_w_uq, mla_w_uk, mla_w_uv, mla_w_o, ssm_w_in, ssm_a_re, ssm_a_im, ssm_log_dt, ssm_b_re, ssm_b_im, ssm_c_re, ssm_c_im, ssm_d, ssm_w_glu, ln1_g, ln1_b, ln2_g, ln2_b, moe_w_rg, moe_b_rg, moe_w_re, moe_b_re, moe_w_gate, moe_w_up, moe_w_down, ple_w_proj, ple_w_gate):
    bp, n_prompt, d = x_prompt.shape
    nb, seq, _ = x_sample.shape
    past = cache_mla_ckv.shape[2]
    assert bp == 1 and seq == SSM_T and n_prompt % ROW_TILE == 0 and (nb * seq) % ROW_TILE == 0
    assert n_prompt % CHUNK == 0 and past % CHUNK == 0 and seq <= CHUNK
    n_samp = nb * seq
    n = n_prompt + n_samp
    xp = x_prompt.reshape(n_prompt, d)
    xs = x_sample.reshape(n_samp, d)
    pp = p_prompt.reshape(DEPTH, n_prompt, -1)
    ps = p_sample.reshape(DEPTH, n_samp, -1)
    layer_args = (ln1_g, ln1_b, ln2_g, ln2_b, moe_w_rg, moe_b_rg, moe_w_re, moe_b_re, moe_w_gate, moe_w_up,
                  moe_w_down, ple_w_proj, ple_w_gate)
    lw = _layer_tables(0, *layer_args)
    mw = _mla_tables(mla_w_in[0], mla_q_norm[0], mla_kv_norm[0], mla_w_uq[0], mla_w_uk[0], mla_w_uv[0], mla_w_o[0])
    pos = jnp.concatenate([jnp.arange(n_prompt, dtype=jnp.int32),
                           past + jnp.tile(jnp.arange(seq, dtype=jnp.int32), nb)])
    ct, st = _rope_tables(pos)
    q, k, v, ckv, kpe_slot = _mla_proj(xp, xs, ct, st, mw)
    o_prompt = _prompt_attention(q, k, v, n_prompt)
    o_sample = _sample_attention(q, cache_mla_ckv[0], cache_mla_kpe[0], ckv, kpe_slot, mw["wabs"], mw["wuv"],
                                 n_prompt)
    h, hb, route = _attn_out(o_prompt, o_sample, xp, xs, mw["wo"], lw)
    x = _ffn(h, hb, route, pp[0], ps[0], lw)
    kpe = kpe_slot[:, QK_NOPE:QK_NOPE + QK_ROPE]

    lw = _layer_tables(1, *layer_args)
    n_chunks = n_prompt // SSM_T
    levels = max(1, (n_chunks - 1).bit_length())
    tabs = _ssm_tables(ssm_a_re[0], ssm_a_im[0], ssm_log_dt[0], ssm_b_re[0], ssm_b_im[0], ssm_c_re[0],
                       ssm_c_im[0], levels)
    u = _ssm_in(x, ssm_w_in[0].astype(BF16))
    h0 = _block_pairs(_pair_states(state_ssm_re[0], state_ssm_im[0]))
    y, hp, hs = _ssm_core(u, tabs, h0, n_chunks)
    h, hb, route = _ssm_out(y, u, x, ssm_d[0].reshape(1, d), ssm_w_glu[0].astype(BF16), lw)
    y_prompt, y_sample = _ffn(h, hb, route, pp[1], ps[1], lw, split_out=True)
    re_p, im_p = _unpair_states(_unblock_pairs(hp))
    re_s, im_s = _unpair_states(_unblock_pairs(hs))

    return (y_prompt.reshape(1, n_prompt, d), y_sample.reshape(nb, seq, d),
            ckv[:n_prompt].reshape(1, 1, n_prompt, KV_LORA), kpe[:n_prompt].reshape(1, 1, n_prompt, QK_ROPE),
            re_p[None], im_p[None],
            ckv[n_prompt:].reshape(1, nb, seq, KV_LORA), kpe[n_prompt:].reshape(1, nb, seq, QK_ROPE),
            re_s[None], im_s[None])
```

```python
import functools
import math

import jax
import jax.numpy as jnp
import numpy as np
from jax import lax
from jax.experimental import pallas as pl
from jax.experimental.pallas import tpu as pltpu

F32 = jnp.float32
BF16 = jnp.bfloat16

N_HEADS = 16
Q_LORA = 256
KV_LORA = 128
QK_NOPE = 64
QK_ROPE = 32
V_DIM = 64
ROPE_BASE = 10000.0
ATTN_SCALE = (QK_NOPE + QK_ROPE) ** -0.5
CHUNK = 64
SSM_GROUP_CH = 16
SSM_STATE = 64
MOE_GROUPS = 4
MOE_EPG = 8
N_EXPERTS = MOE_GROUPS * MOE_EPG
DEPTH = 2
ALPHA = (2.0 * DEPTH) ** 0.25
LN_EPS = 1e-5
RMS_EPS = 1e-6
NEG = -1e30

LANES = 128
SLOT = LANES
ROW_TILE = 256
SSM_T = 16
VMEM_LIMIT = 56 * 1024 * 1024
ROUTER_LANES = LANES
EXPERT_LANE0 = MOE_GROUPS


def _cparams(sem):
    return pltpu.CompilerParams(dimension_semantics=sem, vmem_limit_bytes=VMEM_LIMIT)


def _dot(a, b):
    return jnp.dot(a, b, preferred_element_type=F32)


def _dot_nt(a, b):
    return lax.dot_general(a, b, (((1,), (1,)), ((), ())), preferred_element_type=F32)


def _full(shape):
    nd = len(shape)
    return pl.BlockSpec(shape, lambda *_: (0,) * nd)


def _rows(width, rows=ROW_TILE):
    return pl.BlockSpec((rows, width), lambda i: (i, 0))


def _split_rows(width, prompt_tiles):
    return [pl.BlockSpec((ROW_TILE, width), lambda i: (jnp.minimum(i, prompt_tiles - 1), 0)),
            pl.BlockSpec((ROW_TILE, width), lambda i: (jnp.maximum(i - prompt_tiles, 0), 0))]


def _pick_rows(prompt_ref, sample_ref, prompt_tiles):
    return jnp.where(pl.program_id(0) < prompt_tiles, prompt_ref[...], sample_ref[...])


def _layernorm(t, g, b):
    mu = jnp.mean(t, axis=-1, keepdims=True)
    d = t - mu
    var = jnp.mean(d * d, axis=-1, keepdims=True)
    return d * lax.rsqrt(var + LN_EPS) * g + b


def _rmsnorm(t, g):
    return t * lax.rsqrt(jnp.mean(t * t, axis=-1, keepdims=True) + RMS_EPS) * g


def _mla_proj_kernel(xp_ref, xs_ref, ct_ref, st_ref, w_in_ref, qn_ref, kvn_ref, wqa_ref, wqb_ref, wk_ref, wv_ref,
                     vone_ref, q_ref, k_ref, v_ref, ckv_ref, kpe_ref, *, prompt_tiles):
    x = _pick_rows(xp_ref, xs_ref, prompt_tiles).astype(BF16)
    z = _dot(x, w_in_ref[...])
    cq = _rmsnorm(z[:, :Q_LORA], qn_ref[...])
    ckv = _rmsnorm(z[:, Q_LORA:Q_LORA + KV_LORA], kvn_ref[...])
    ct = ct_ref[...]
    st = st_ref[...]
    o = Q_LORA + KV_LORA
    kpe = z[:, o:o + SLOT] * ct + z[:, o + SLOT:o + 2 * SLOT] * st
    ckv_ref[...] = ckv
    kpe_ref[...] = kpe
    cqb = cq.astype(BF16)
    ckvb = ckv.astype(BF16)
    qa = _dot(cqb, wqa_ref[...])
    qb = _dot(cqb, wqb_ref[...])
    kn = _dot(ckvb, wk_ref[...])
    vv = _dot(ckvb, wv_ref[...]) + vone_ref[...]
    scale = ATTN_SCALE * math.log2(math.e)
    for h in range(N_HEADS):
        sl = slice(h * SLOT, (h + 1) * SLOT)
        q_ref[:, sl] = ((qa[:, sl] * ct + qb[:, sl] * st) * scale).astype(BF16)
        k_ref[:, sl] = (kn[:, sl] + kpe).astype(BF16)
    v_ref[...] = vv.astype(BF16)


def _mla_proj(xp, xs, ct, st, w):
    n = xp.shape[0] + xs.shape[0]
    pt = xp.shape[0] // ROW_TILE
    wide = N_HEADS * SLOT
    return pl.pallas_call(
        functools.partial(_mla_proj_kernel, prompt_tiles=pt),
        grid=(n // ROW_TILE,),
        in_specs=_split_rows(xp.shape[1], pt) + [_rows(SLOT), _rows(SLOT), _full(w["w_in"].shape), _full((1, Q_LORA)),
                  _full((1, KV_LORA)), _full(w["wqa"].shape), _full(w["wqb"].shape), _full(w["wk"].shape),
                  _full(w["wv"].shape), _full((1, wide))],
        out_specs=[_rows(wide), _rows(wide), _rows(wide), _rows(KV_LORA), _rows(SLOT)],
        out_shape=[jax.ShapeDtypeStruct((n, wide), BF16)] * 3
        + [jax.ShapeDtypeStruct((n, KV_LORA), F32), jax.ShapeDtypeStruct((n, SLOT), F32)],
        compiler_params=_cparams(("parallel",)),
        name="mla_proj",
    )(xp, xs, ct, st, w["w_in"], w["qn"], w["kvn"], w["wqa"], w["wqb"], w["wk"], w["wv"], w["vone"])


ATT_TQ = 1024
ATT_TK = 2048
ATT_G = 2


def _attn_kernel(q_ref, k_ref, v_ref, o_ref):
    qi = pl.program_id(1)
    slots = [slice(g * SLOT, (g + 1) * SLOT) for g in range(ATT_G)]
    ratio = ATT_TK // ATT_TQ

    def update(off, tk, state, mask):
        out = []
        for sl, (m, acc) in zip(slots, state):
            s = _dot_nt(q_ref[:, sl], k_ref[pl.ds(off, tk), sl])
            if mask is not None:
                s = jnp.where(mask, s, NEG)
            m_new = jnp.maximum(m, jnp.max(s, axis=-1, keepdims=True))
            p = jnp.exp2(s - m_new)
            pv = _dot(p.astype(BF16), v_ref[pl.ds(off, tk), sl])
            out.append((m_new, acc * jnp.exp2(m - m_new) + pv))
        return tuple(out)

    def body(j, state):
        return update(pl.multiple_of(j * ATT_TK, ATT_TK), ATT_TK, state, None)

    init = tuple((jnp.full((ATT_TQ, 1), NEG, F32), jnp.zeros((ATT_TQ, SLOT), F32)) for _ in range(ATT_G))
    state = lax.fori_loop(0, qi // ratio, body, init)

    def tail(rem):
        tk = (rem + 1) * ATT_TQ
        rc = lax.broadcasted_iota(jnp.int32, (ATT_TQ, tk), 0) // CHUNK
        cc = lax.broadcasted_iota(jnp.int32, (ATT_TQ, tk), 1) // CHUNK - rem * (ATT_TQ // CHUNK)
        return lambda st: update(pl.multiple_of((qi - rem) * ATT_TQ, ATT_TQ), tk, st, cc <= rc)

    state = lax.switch(qi % ratio, [tail(rem) for rem in range(ratio)], state)
    lane = lax.broadcasted_iota(jnp.int32, (ATT_TQ, SLOT), 1)
    outs = []
    for g in range(ATT_G):
        acc = state[g][1]
        l = jnp.sum(jnp.where(lane == V_DIM, acc, 0.0), axis=-1, keepdims=True)
        outs.append((acc / l)[:, :V_DIM])
    o_ref[...] = jnp.concatenate(outs, axis=1).astype(BF16)


def _prompt_attention(q, k, v, n_prompt):
    n = q.shape[0]
    wide = ATT_G * SLOT
    assert ATT_TK % ATT_TQ == 0 and ATT_TQ % CHUNK == 0 and n_prompt % ATT_TQ == 0 and N_HEADS % ATT_G == 0
    assert (ATT_G * V_DIM) % LANES == 0
    return pl.pallas_call(
        _attn_kernel,
        grid=(N_HEADS // ATT_G, n_prompt // ATT_TQ),
        in_specs=[pl.BlockSpec((ATT_TQ, wide), lambda h, i: (i, h)),
                  pl.BlockSpec((n_prompt, wide), lambda h, i: (0, h)),
                  pl.BlockSpec((n_prompt, wide), lambda h, i: (0, h))],
        out_specs=pl.BlockSpec((ATT_TQ, ATT_G * V_DIM), lambda h, i: (i, h)),
        out_shape=jax.ShapeDtypeStruct((n_prompt, N_HEADS * V_DIM), BF16),
        compiler_params=_cparams(("parallel", "arbitrary")),
        name="prompt_attention",
    )(q, k, v)


def _sample_attn_kernel(q_ref, ckvc_ref, kpec_ref, ckvn_ref, kpen_ref, wabs_ref, wuv_ref, o_ref):
    q = q_ref[...]
    seq = q.shape[0]
    qa, qp = [], []
    for h in range(N_HEADS):
        qh = q[:, h * SLOT:(h + 1) * SLOT]
        qa.append(_dot(qh, wabs_ref[h]))
        qp.append(qh[:, QK_NOPE:QK_NOPE + QK_ROPE])
    qa = jnp.concatenate(qa, axis=0).astype(BF16)
    qp = jnp.concatenate(qp, axis=0)
    ckvc = ckvc_ref[0].astype(BF16)
    kpec = kpec_ref[0].astype(BF16)
    ckvn = ckvn_ref[...].astype(BF16)
    kpen = kpen_ref[...][:, QK_NOPE:QK_NOPE + QK_ROPE].astype(BF16)
    s_c = _dot_nt(qa, ckvc) + _dot_nt(qp, kpec)
    s_n = _dot_nt(qa, ckvn) + _dot_nt(qp, kpen)
    m = jnp.maximum(jnp.max(s_c, axis=-1, keepdims=True), jnp.max(s_n, axis=-1, keepdims=True))
    p_c = jnp.exp2(s_c - m)
    p_n = jnp.exp2(s_n - m)
    l = jnp.sum(p_c, axis=-1, keepdims=True) + jnp.sum(p_n, axis=-1, keepdims=True)
    ol = (_dot(p_c.astype(BF16), ckvc) + _dot(p_n.astype(BF16), ckvn)) / l
    olb = ol.astype(BF16)
    outs = [_dot(olb[h * seq:(h + 1) * seq], wuv_ref[h]) for h in range(N_HEADS)]
    o_ref[...] = jnp.concatenate(outs, axis=1).astype(BF16)


def _sample_attention(q, cache_ckv, cache_kpe, ckv_new, kpe_new, wabs, wuv, n_prompt):
    nb, past, _ = cache_ckv.shape
    seq = (q.shape[0] - n_prompt) // nb
    base = n_prompt // seq
    wide = N_HEADS * SLOT
    return pl.pallas_call(
        _sample_attn_kernel,
        grid=(nb,),
        in_specs=[pl.BlockSpec((seq, wide), lambda b: (base + b, 0)),
                  pl.BlockSpec((1, past, KV_LORA), lambda b: (b, 0, 0)),
                  pl.BlockSpec((1, past, QK_ROPE), lambda b: (b, 0, 0)),
                  pl.BlockSpec((seq, KV_LORA), lambda b: (base + b, 0)),
                  pl.BlockSpec((seq, SLOT), lambda b: (base + b, 0)),
                  _full(wabs.shape), _full(wuv.shape)],
        out_specs=pl.BlockSpec((seq, N_HEADS * V_DIM), lambda b: (b, 0)),
        out_shape=jax.ShapeDtypeStruct((nb * seq, N_HEADS * V_DIM), BF16),
        compiler_params=_cparams(("parallel",)),
        name="sample_attention",
    )(q, cache_ckv, cache_kpe, ckv_new, kpe_new, wabs, wuv)


ROUTE_E1, ROUTE_E2, ROUTE_W1, ROUTE_W2 = range(4)


def _lane_col(tile, k):
    lane = lax.broadcasted_iota(jnp.int32, tile.shape, 1)
    return jnp.sum(jnp.where(lane == k, tile, 0.0), axis=-1, keepdims=True)


def _route(lg):
    lane = lax.broadcasted_iota(jnp.int32, lg.shape, 1)
    lanef = lane.astype(F32)
    big = float(ROUTER_LANES)
    is_g = lane < MOE_GROUPS
    gl = jnp.where(is_g, lg, NEG)
    gmax = jnp.max(gl, axis=-1, keepdims=True)
    gsel = jnp.min(jnp.where(gl == gmax, lanef, big), axis=-1, keepdims=True)
    gate_g = 1.0 / jnp.sum(jnp.where(is_g, jnp.exp(gl - gmax), 0.0), axis=-1, keepdims=True)
    lo = EXPERT_LANE0 + MOE_EPG * gsel
    el = jnp.where(jnp.logical_and(lanef >= lo, lanef < lo + MOE_EPG), lg, NEG)
    m1 = jnp.max(el, axis=-1, keepdims=True)
    i1 = jnp.min(jnp.where(el == m1, lanef, big), axis=-1, keepdims=True)
    el2 = jnp.where(lanef == i1, NEG, el)
    m2 = jnp.max(el2, axis=-1, keepdims=True)
    i2 = jnp.min(jnp.where(el2 == m2, lanef, big), axis=-1, keepdims=True)
    r = jnp.exp(m2 - m1)
    w1 = 1.0 / (1.0 + r)
    w2 = r / (1.0 + r)
    cols = {ROUTE_E1: i1 - EXPERT_LANE0, ROUTE_E2: i2 - EXPERT_LANE0, ROUTE_W1: gate_g * w1, ROUTE_W2: gate_g * w2}
    route = jnp.zeros(lg.shape, F32)
    for k, c in cols.items():
        route = jnp.where(lane == k, c, route)
    return route


def _post_mix(t, x, g, b, wr_hi, wr_lo, br, h_ref, hb_ref, route_ref):
    h = _layernorm(ALPHA * x + t, g, b)
    h_ref[...] = h
    hb = h.astype(BF16)
    hb_ref[...] = hb
    h_lo = (h - hb.astype(F32)).astype(BF16)
    lg = _dot(hb, wr_hi) + (_dot(hb, wr_lo) + _dot(h_lo, wr_hi)) + br
    route_ref[...] = _route(lg)


def _attn_out_kernel(op_ref, os_ref, xp_ref, xs_ref, wo_ref, g_ref, b_ref, wrh_ref, wrl_ref, br_ref, h_ref, hb_ref,
                     route_ref, *, prompt_tiles):
    t = _dot(_pick_rows(op_ref, os_ref, prompt_tiles), wo_ref[...])
    _post_mix(t, _pick_rows(xp_ref, xs_ref, prompt_tiles), g_ref[...], b_ref[...], wrh_ref[...], wrl_ref[...],
              br_ref[...], h_ref, hb_ref, route_ref)


def _post_out(n, d):
    specs = [_rows(d), _rows(d), _rows(ROUTER_LANES)]
    shapes = [jax.ShapeDtypeStruct((n, d), F32), jax.ShapeDtypeStruct((n, d), BF16),
              jax.ShapeDtypeStruct((n, ROUTER_LANES), F32)]
    return specs, shapes


def _attn_out(o_prompt, o_sample, xp, xs, wo, lw):
    d = xp.shape[1]
    n = xp.shape[0] + xs.shape[0]
    pt = xp.shape[0] // ROW_TILE
    specs, shapes = _post_out(n, d)
    return pl.pallas_call(
        functools.partial(_attn_out_kernel, prompt_tiles=pt),
        grid=(n // ROW_TILE,),
        in_specs=_split_rows(o_prompt.shape[1], pt) + _split_rows(d, pt)
        + [_full(wo.shape), _full((1, d)), _full((1, d)),
           _full(lw["wr_hi"].shape), _full(lw["wr_lo"].shape), _full((1, ROUTER_LANES))],
        out_specs=specs, out_shape=shapes,
        compiler_params=_cparams(("parallel",)),
        name="attn_out_ln_router",
    )(o_prompt, o_sample, xp, xs, wo, lw["ln1_g"], lw["ln1_b"], lw["wr_hi"], lw["wr_lo"], lw["br"])


MOE_EPS = 8
MOE_TILE_MAX = 896


def _moe_dense_kernel(hb_ref, route_ref, wg_ref, wu_ref, wd_ref, y_ref):
    x = hb_ref[...]
    route = route_ref[...]
    e1 = _lane_col(route, ROUTE_E1)
    e2 = _lane_col(route, ROUTE_E2)
    w1 = _lane_col(route, ROUTE_W1)
    w2 = _lane_col(route, ROUTE_W2)
    y = None
    for k in range(MOE_EPS):
        e = (pl.program_id(1) * MOE_EPS + k).astype(F32)
        g = _dot(x, wg_ref[k])
        u = _dot(x, wu_ref[k])
        gate = jnp.where(e1 == e, w1, 0.0) + jnp.where(e2 == e, w2, 0.0)
        hdn = (g * jax.nn.sigmoid(g)) * u * gate
        yk = _dot(hdn.astype(BF16), wd_ref[k])
        y = yk if y is None else y + yk

    @pl.when(pl.program_id(1) == 0)
    def _():
        y_ref[...] = y

    @pl.when(pl.program_id(1) > 0)
    def _():
        y_ref[...] += y


def _moe_tile(n):
    return max(t for t in range(16, MOE_TILE_MAX + 1, 16) if n % t == 0)


def _moe_dense(hb, route, wg, wu, wd):
    n, d = hb.shape
    ff = wg.shape[-1]
    tile = _moe_tile(n)
    return pl.pallas_call(
        _moe_dense_kernel,
        grid=(n // tile, N_EXPERTS // MOE_EPS),
        in_specs=[pl.BlockSpec((tile, d), lambda i, e: (i, 0)),
                  pl.BlockSpec((tile, ROUTER_LANES), lambda i, e: (i, 0)),
                  pl.BlockSpec((MOE_EPS, d, ff), lambda i, e: (e, 0, 0)),
                  pl.BlockSpec((MOE_EPS, d, ff), lambda i, e: (e, 0, 0)),
                  pl.BlockSpec((MOE_EPS, ff, d), lambda i, e: (e, 0, 0))],
        out_specs=pl.BlockSpec((tile, d), lambda i, e: (i, 0)),
        out_shape=jax.ShapeDtypeStruct((n, d), F32),
        compiler_params=_cparams(("parallel", "arbitrary")),
        name="moe_dense",
    )(hb, route, wg, wu, wd)


def _ln2_ple_kernel(h_ref, y_ref, pp_ref, ps_ref, g_ref, b_ref, wp_ref, wpg_ref, *out_refs, prompt_tiles):
    h2 = _layernorm(ALPHA * h_ref[...] + y_ref[...], g_ref[...], b_ref[...])
    proj = _dot(_pick_rows(pp_ref, ps_ref, prompt_tiles).astype(BF16), wp_ref[...])
    gate = jax.nn.sigmoid(_dot(h2.astype(BF16), wpg_ref[...]))
    x = h2 + proj * gate
    if len(out_refs) == 1:
        out_refs[0][...] = x
    else:
        is_prompt = pl.program_id(0) < prompt_tiles

        @pl.when(is_prompt)
        def _():
            out_refs[0][...] = x

        @pl.when(jnp.logical_not(is_prompt))
        def _():
            out_refs[1][...] = x


def _ln2_ple(h, y, pp, ps, lw, split_out):
    n, d = h.shape
    pt = pp.shape[0] // ROW_TILE
    if split_out:
        out_specs = _split_rows(d, pt)
        out_shape = [jax.ShapeDtypeStruct((pp.shape[0], d), F32), jax.ShapeDtypeStruct((ps.shape[0], d), F32)]
    else:
        out_specs, out_shape = _rows(d), jax.ShapeDtypeStruct((n, d), F32)
    return pl.pallas_call(
        functools.partial(_ln2_ple_kernel, prompt_tiles=pt),
        grid=(n // ROW_TILE,),
        in_specs=[_rows(d), _rows(d)] + _split_rows(pp.shape[1], pt)
        + [_full((1, d)), _full((1, d)), _full(lw["w_proj"].shape), _full(lw["w_pg"].shape)],
        out_specs=out_specs, out_shape=out_shape,
        compiler_params=_cparams(("arbitrary",)),
        name="ln2_ple",
    )(h, y, pp, ps, lw["ln2_g"], lw["ln2_b"], lw["w_proj"], lw["w_pg"])


def _ffn(h, hb, route, pp, ps, lw, split_out=False):
    y = _moe_dense(hb, route, lw["wg"], lw["wu"], lw["wd"])
    return _ln2_ple(h, y, pp, ps, lw, split_out)


def _ssm_in_kernel(x_ref, w_ref, u_ref):
    u_ref[...] = _dot(x_ref[...].astype(BF16), w_ref[...])


def _ssm_in(x, w):
    n, d = x.shape
    return pl.pallas_call(
        _ssm_in_kernel,
        grid=(n // ROW_TILE,),
        in_specs=[_rows(d), _full(w.shape)],
        out_specs=_rows(w.shape[1]),
        out_shape=jax.ShapeDtypeStruct((n, w.shape[1]), F32),
        compiler_params=_cparams(("parallel",)),
        name="ssm_in",
    )(x, w)


SSM_GB = LANES // SSM_GROUP_CH
SSM_PAIR = 2 * LANES


def _ssm_core_kernel(u_ref, bd_ref, bst_ref, cst_ref, apow_ref, h0_ref, y_ref, hp_ref, hs_ref, sre_ref, sim_ref,
                     *, n_chunks, levels):
    t = SSM_T
    rows = u_ref.shape[0] // t
    pad = sre_ref.shape[0] - n_chunks
    npair = SSM_GB // 2
    ut = [u_ref[pl.ds(tl, rows, stride=t), :].astype(BF16) for tl in range(t)]
    u2 = [jnp.concatenate([ut[2 * k], ut[2 * k + 1]], axis=1) for k in range(t // 2)]
    bb = None
    for k in range(t // 2):
        part = _dot(u2[k], jnp.concatenate([bst_ref[0, 2 * k], bst_ref[0, 2 * k + 1]], axis=0))
        bb = part if bb is None else bb + part
    zeros = jnp.zeros((pad, LANES), F32)
    sre_ref[:pad, :] = zeros
    sim_ref[:pad, :] = zeros

    def shifted(d):
        return sre_ref[pl.ds(pad - d, n_chunks), :], sim_ref[pl.ds(pad - d, n_chunks), :]

    h0 = h0_ref[0]
    hprev, h_last, h_samp = [], [], []
    for j in range(npair):
        lo = j * SSM_PAIR
        re = bb[:n_chunks, lo:lo + LANES]
        im = bb[:n_chunks, lo + LANES:lo + SSM_PAIR]
        for lv in range(levels):
            sre_ref[pad:, :] = re
            sim_ref[pad:, :] = im
            pr, pi = shifted(1 << lv)
            ar = apow_ref[0, lv:lv + 1, lo:lo + LANES]
            ai = apow_ref[0, lv:lv + 1, lo + LANES:lo + SSM_PAIR]
            re, im = re + ar * pr - ai * pi, im + ar * pi + ai * pr
        sre_ref[pad:, :] = re
        sim_ref[pad:, :] = im
        pr, pi = shifted(1)
        h0j = h0[:, lo:lo + SSM_PAIR]
        hprev.append(jnp.concatenate([jnp.concatenate([pr, pi], axis=1), h0j], axis=0).astype(BF16))
        h_last.append(jnp.concatenate([re[n_chunks - 1:, :], im[n_chunks - 1:, :]], axis=1))
        ar = apow_ref[0, 0:1, lo:lo + LANES]
        ai = apow_ref[0, 0:1, lo + LANES:lo + SSM_PAIR]
        h0r = h0j[:, :LANES]
        h0i = h0j[:, LANES:]
        h_samp.append(jnp.concatenate([ar * h0r - ai * h0i + bb[n_chunks:, lo:lo + LANES],
                                       ar * h0i + ai * h0r + bb[n_chunks:, lo + LANES:lo + SSM_PAIR]], axis=1))
    hp_ref[0] = jnp.concatenate(h_last, axis=1)
    hs_ref[0] = jnp.concatenate(h_samp, axis=1)
    hprev = jnp.concatenate(hprev, axis=1)
    zero_tap = jnp.zeros((LANES, LANES), BF16)

    def tap(lag):
        return bd_ref[0, lag] if lag >= 0 else zero_tap

    for ko in range(t // 2):
        acc = _dot_nt(hprev, jnp.concatenate([cst_ref[0, 2 * ko], cst_ref[0, 2 * ko + 1]], axis=0))
        for ki in range(ko + 1):
            lag = 2 * (ko - ki)
            w = jnp.concatenate([jnp.concatenate([tap(lag), tap(lag + 1)], axis=1),
                                 jnp.concatenate([tap(lag - 1), tap(lag)], axis=1)], axis=0)
            acc = acc + _dot(u2[ki], w)
        y_ref[pl.ds(2 * ko, rows, stride=t), :] = acc[:, :LANES]
        y_ref[pl.ds(2 * ko + 1, rows, stride=t), :] = acc[:, LANES:]


def _ssm_core(u, tabs, h0, n_chunks):
    n, d = u.shape
    nblk = d // LANES
    ns = n // SSM_T - n_chunks
    levels = max(1, (n_chunks - 1).bit_length())
    pad = max(1 << (levels - 1), 8)
    wide = (SSM_GB // 2) * SSM_PAIR
    kern = functools.partial(_ssm_core_kernel, n_chunks=n_chunks, levels=levels)
    once = dict(pipeline_mode=pl.Buffered(1))

    def blk(*shape):
        return pl.BlockSpec((1,) + shape, lambda i: (i,) + (0,) * len(shape), **once)

    return pl.pallas_call(
        kern,
        grid=(nblk,),
        in_specs=[pl.BlockSpec((n, LANES), lambda i: (0, i), **once), blk(SSM_T, LANES, LANES),
                  blk(SSM_T, LANES, wide), blk(SSM_T, LANES, wide), blk(tabs["apow"].shape[1], wide), blk(ns, wide)],
        out_specs=[pl.BlockSpec((n, LANES), lambda i: (0, i)),
                   pl.BlockSpec((1, 1, wide), lambda i: (i, 0, 0)), pl.BlockSpec((1, ns, wide), lambda i: (i, 0, 0))],
        out_shape=[jax.ShapeDtypeStruct((n, d), F32), jax.ShapeDtypeStruct((nblk, 1, wide), F32),
                   jax.ShapeDtypeStruct((nblk, ns, wide), F32)],
        scratch_shapes=[pltpu.VMEM((pad + n_chunks, LANES), F32), pltpu.VMEM((pad + n_chunks, LANES), F32)],
        compiler_params=_cparams(("parallel",)),
        name="ssm_core",
    )(u, tabs["bd"], tabs["bst"], tabs["cst"], tabs["apow"], h0)


def _gelu_tanh(y):
    c = math.sqrt(2.0 / math.pi)
    return 0.5 * y * (1.0 + jnp.tanh(c * (y + 0.044715 * (y * y * y))))


def _ssm_out_kernel(y_ref, u_ref, x_ref, d_ref, wglu_ref, g_ref, b_ref, wrh_ref, wrl_ref, br_ref,
                    h_ref, hb_ref, route_ref):
    y = y_ref[...] + d_ref[...] * u_ref[...]
    z = _dot(_gelu_tanh(y).astype(BF16), wglu_ref[...])
    dm = z.shape[1] // 2
    t = z[:, :dm] * jax.nn.sigmoid(z[:, dm:])
    _post_mix(t, x_ref[...], g_ref[...], b_ref[...], wrh_ref[...], wrl_ref[...], br_ref[...],
              h_ref, hb_ref, route_ref)


def _ssm_out(y, u, x, d_skip, wglu, lw):
    n, d = x.shape
    specs, shapes = _post_out(n, d)
    return pl.pallas_call(
        _ssm_out_kernel,
        grid=(n // ROW_TILE,),
        in_specs=[_rows(d), _rows(d), _rows(d), _full((1, d)), _full(wglu.shape), _full((1, d)), _full((1, d)),
                  _full(lw["wr_hi"].shape), _full(lw["wr_lo"].shape), _full((1, ROUTER_LANES))],
        out_specs=specs, out_shape=shapes,
        compiler_params=_cparams(("parallel",)),
        name="ssm_out_ln_router",
    )(y, u, x, d_skip, wglu, lw["ln1_g"], lw["ln1_b"], lw["wr_hi"], lw["wr_lo"], lw["br"])


def _rope_tables(pos):
    half = QK_ROPE // 2
    inv = ROPE_BASE ** (-jnp.arange(half, dtype=F32) / half)
    n = pos.shape[0]
    per = LANES // half
    assert n % per == 0
    ang = (pos.astype(F32).reshape(n // per, per, 1) * inv.reshape(1, 1, half)).reshape(n // per, LANES)
    cos, sin = jnp.cos(ang).reshape(n, half), jnp.sin(ang).reshape(n, half)
    ones = jnp.ones((n, QK_NOPE), F32)
    zeros = jnp.zeros((n, SLOT - QK_NOPE - QK_ROPE), F32)
    ct = jnp.concatenate([ones, cos, cos, zeros], axis=1)
    st = jnp.concatenate([jnp.zeros((n, QK_NOPE), F32), -sin, sin, zeros], axis=1)
    return ct, st


def _slot_cols(w, width):
    k = w.shape[0]
    return jnp.pad(w, ((0, 0), (0, 0), (0, SLOT - width))).reshape(k, N_HEADS * SLOT)


def _mla_tables(w_in, q_norm, kv_norm, w_uq, w_uk, w_uv, w_o):
    half = QK_ROPE // 2
    o = Q_LORA + KV_LORA
    d = w_in.shape[0]
    kpe_w = w_in[:, o:]
    kpe_sw = jnp.concatenate([kpe_w[:, half:], kpe_w[:, :half]], axis=1)
    zl = jnp.zeros((d, QK_NOPE), F32)
    zr = jnp.zeros((d, SLOT - QK_NOPE - QK_ROPE), F32)
    w_in_e = jnp.concatenate([w_in[:, :o], zl, kpe_w, zr, zl, kpe_sw, zr], axis=1)
    wq = w_uq.reshape(Q_LORA, N_HEADS, QK_NOPE + QK_ROPE)
    pe = wq[:, :, QK_NOPE:]
    pe_sw = jnp.concatenate([pe[:, :, half:], pe[:, :, :half]], axis=2)
    wqb = jnp.concatenate([jnp.zeros_like(wq[:, :, :QK_NOPE]), pe_sw], axis=2)
    vone = jnp.zeros((N_HEADS, SLOT), F32).at[:, V_DIM].set(1.0).reshape(1, N_HEADS * SLOT)
    wabs = jnp.pad(jnp.transpose(w_uk, (1, 2, 0)), ((0, 0), (0, SLOT - QK_NOPE), (0, 0)))
    wuv = jnp.transpose(w_uv, (1, 0, 2))
    return dict(
        w_in=w_in_e.astype(BF16), qn=q_norm.reshape(1, -1), kvn=kv_norm.reshape(1, -1),
        wqa=_slot_cols(wq, QK_NOPE + QK_ROPE).astype(BF16), wqb=_slot_cols(wqb, QK_NOPE + QK_ROPE).astype(BF16),
        wk=_slot_cols(w_uk, QK_NOPE).astype(BF16), wv=_slot_cols(w_uv, V_DIM).astype(BF16), vone=vone,
        wabs=wabs.astype(BF16), wuv=wuv.astype(BF16), wo=w_o.astype(BF16))


def _cmul(a, b):
    return a[0] * b[0] - a[1] * b[1], a[0] * b[1] + a[1] * b[0]


def _ssm_tables(a_re, a_im, log_dt, b_re, b_im, c_re, c_im, levels):
    t = SSM_T
    g, p = a_re.shape
    c = SSM_GROUP_CH
    hi = lax.Precision.HIGHEST
    dt = jnp.exp(log_dt)[:, None]
    mag = jnp.exp(a_re * dt)
    lam_bar = (mag * jnp.cos(a_im * dt), mag * jnp.sin(a_im * dt))
    den = a_re * a_re + a_im * a_im
    quo = (((lam_bar[0] - 1.0) * a_re + lam_bar[1] * a_im) / den,
           (lam_bar[1] * a_re - (lam_bar[0] - 1.0) * a_im) / den)
    b_bar = _cmul((quo[0][:, :, None], quo[1][:, :, None]), (b_re, b_im))
    pw = [(jnp.ones_like(a_re), jnp.zeros_like(a_re))]
    for _ in range(t):
        pw.append(_cmul(pw[-1], lam_bar))
    pw_r = jnp.stack([x[0] for x in pw])
    pw_i = jnp.stack([x[1] for x in pw])
    cp = _cmul((c_re[:, None], c_im[:, None]),
               (jnp.transpose(pw_r[:t], (1, 0, 2))[:, :, None, :], jnp.transpose(pw_i[:t], (1, 0, 2))[:, :, None, :]))
    taps = jnp.einsum("gmk,gki->gmi", jnp.concatenate([cp[0], -cp[1]], axis=-1).reshape(g, t * c, 2 * p),
                      jnp.concatenate([b_bar[0], b_bar[1]], axis=1), precision=hi)
    nblk = g // SSM_GB
    width = SSM_GB * c
    tg = jnp.transpose(taps.reshape(nblk, SSM_GB, t, c, c), (0, 2, 1, 4, 3))
    own = np.zeros((SSM_GB, c, SSM_GB, c), np.float32)
    for g8 in range(SSM_GB):
        own[g8, np.arange(c), g8, np.arange(c)] = 1.0
    own = jnp.asarray(own.reshape(SSM_GB, c, width), BF16)
    bd = jnp.einsum("btgio,gol->btgil", tg.astype(BF16), own, preferred_element_type=BF16)
    bd = bd.reshape(nblk, t, width, width)
    lanes = (SSM_GB // 2) * SSM_PAIR
    place = np.zeros((SSM_GB, 2, p, SSM_GB // 2, 2, 2, p), np.float32)
    for g8 in range(SSM_GB):
        for ri in range(2):
            place[g8, ri, np.arange(p), g8 // 2, ri, g8 % 2, np.arange(p)] = 1.0
    place = jnp.asarray(place.reshape(SSM_GB, 2 * p, lanes), BF16)

    def spread(v):
        v = jnp.transpose(v, (2, 1, 3, 5, 0, 4)).reshape(nblk, t, SSM_GB, c, 2 * p)
        out = jnp.einsum("btgck,gkl->btgcl", v.astype(BF16), place, preferred_element_type=BF16)
        return out.reshape(nblk, t, width, lanes)

    rev = t - 1 - jnp.arange(t)
    bfl = _cmul((pw_r[rev][:, :, :, None], pw_i[rev][:, :, :, None]), (b_bar[0][None], b_bar[1][None]))
    bst = spread(jnp.stack(bfl).reshape(2, t, nblk, SSM_GB, p, c))
    cfl = _cmul((c_re[None], c_im[None]), (pw_r[1:t + 1, :, None, :], pw_i[1:t + 1, :, None, :]))
    cv = jnp.stack([cfl[0], -cfl[1]]).reshape(2, t, nblk, SSM_GB, c, p)
    cst = spread(jnp.swapaxes(cv, 4, 5))
    ap = [pw[t]]
    for _ in range(levels - 1):
        ap.append(_cmul(ap[-1], ap[-1]))

    def pair_lanes(xs):
        v = jnp.stack(xs, axis=1).reshape(g // 2, 2, levels, p)
        return jnp.transpose(v, (0, 2, 1, 3)).reshape(g // 2, levels, 2 * p)

    apow = jnp.concatenate([pair_lanes([x[0] for x in ap]), pair_lanes([x[1] for x in ap])], axis=2)
    rows = -(-levels // 8) * 8
    apow = jnp.pad(apow, ((0, 0), (0, rows - levels), (0, 0)))
    return dict(bd=bd.astype(BF16), bst=bst.astype(BF16), cst=cst.astype(BF16), apow=_block_pairs(apow))


def _block_pairs(v):
    pairs, r, w = v.shape
    per = SSM_GB // 2
    return jnp.transpose(v.reshape(pairs // per, per, r, w), (0, 2, 1, 3)).reshape(pairs // per, r, per * w)


def _unblock_pairs(v):
    nblk, r, w = v.shape
    per = SSM_GB // 2
    return jnp.transpose(v.reshape(nblk, r, per, w // per), (0, 2, 1, 3)).reshape(nblk * per, r, w // per)


def _pair_states(re, im):
    b, g, p = re.shape
    r = jnp.transpose(re.reshape(b, g // 2, 2 * p), (1, 0, 2))
    i = jnp.transpose(im.reshape(b, g // 2, 2 * p), (1, 0, 2))
    return jnp.concatenate([r, i], axis=2)


def _unpair_states(h):
    pairs, b, w = h.shape
    p = w // 4
    r = jnp.transpose(h[:, :, :2 * p], (1, 0, 2)).reshape(b, pairs * 2, p)
    i = jnp.transpose(h[:, :, 2 * p:], (1, 0, 2)).reshape(b, pairs * 2, p)
    return r, i


def _layer_tables(i, ln1_g, ln1_b, ln2_g, ln2_b, w_rg, b_rg, w_re, b_re, w_gate, w_up, w_down, w_proj, w_pg):
    d = w_rg.shape[1]
    wr = jnp.concatenate([w_rg[i], jnp.transpose(w_re[i], (1, 0, 2)).reshape(d, N_EXPERTS)], axis=1)
    wr = jnp.pad(wr, ((0, 0), (0, ROUTER_LANES - wr.shape[1])))
    wr_hi = wr.astype(BF16)
    wr_lo = (wr - wr_hi.astype(F32)).astype(BF16)
    br = jnp.pad(jnp.concatenate([b_rg[i], b_re[i].reshape(-1)]), (0, ROUTER_LANES - MOE_GROUPS - N_EXPERTS))
    ff = w_gate.shape[-1]
    return dict(
        ln1_g=ln1_g[i].reshape(1, d), ln1_b=ln1_b[i].reshape(1, d), ln2_g=ln2_g[i].reshape(1, d),
        ln2_b=ln2_b[i].reshape(1, d), wr_hi=wr_hi, wr_lo=wr_lo, br=br.reshape(1, ROUTER_LANES),
        wg=w_gate[i].reshape(N_EXPERTS, d, ff).astype(BF16), wu=w_up[i].reshape(N_EXPERTS, d, ff).astype(BF16),
        wd=w_down[i].reshape(N_EXPERTS, ff, d).astype(BF16),
        w_proj=w_proj[i].astype(BF16), w_pg=w_pg[i].astype(BF16))


def kernel(x_prompt, x_sample, p_prompt, p_sample, cache_mla_ckv, cache_mla_kpe, state_ssm_re, state_ssm_im, mla_w_in, mla_q_norm, mla_kv_norm, mla_w_uq, mla_w_uk, mla_w_uv, mla_w_o, ssm_w_in, ssm_a_re, ssm_a_im, ssm_log_dt, ssm_b_re, ssm_b_im, ssm_c_re, ssm_c_im, ssm_d, ssm_w_glu, ln1_g, ln1_b, ln2_g, ln2_b, moe_w_rg, moe_b_rg, moe_w_re, moe_b_re, moe_w_gate, moe_w_up, moe_w_down, ple_w_proj, ple_w_gate):
    bp, n_prompt, d = x_prompt.shape
    nb, seq, _ = x_sample.shape
    past = cache_mla_ckv.shape[2]
    assert bp == 1 and seq == SSM_T and n_prompt % ROW_TILE == 0 and (nb * seq) % ROW_TILE == 0
    assert n_prompt % CHUNK == 0 and past % CHUNK == 0 and seq <= CHUNK
    n_samp = nb * seq
    n = n_prompt + n_samp
    xp = x_prompt.reshape(n_prompt, d)
    xs = x_sample.reshape(n_samp, d)
    pp = p_prompt.reshape(DEPTH, n_prompt, -1)
    ps = p_sample.reshape(DEPTH, n_samp, -1)
    layer_args = (ln1_g, ln1_b, ln2_g, ln2_b, moe_w_rg, moe_b_rg, moe_w_re, moe_b_re, moe_w_gate, moe_w_up,
                  moe_w_down, ple_w_proj, ple_w_gate)
    lw = _layer_tables(0, *layer_args)
    mw = _mla_tables(mla_w_in[0], mla_q_norm[0], mla_kv_norm[0], mla_w_uq[0], mla_w_uk[0], mla_w_uv[0], mla_w_o[0])
    pos = jnp.concatenate([jnp.arange(n_prompt, dtype=jnp.int32),
                           past + jnp.tile(jnp.arange(seq, dtype=jnp.int32), nb)])
    ct, st = _rope_tables(pos)
    q, k, v, ckv, kpe_slot = _mla_proj(xp, xs, ct, st, mw)
    o_prompt = _prompt_attention(q, k, v, n_prompt)
    o_sample = _sample_attention(q, cache_mla_ckv[0], cache_mla_kpe[0], ckv, kpe_slot, mw["wabs"], mw["wuv"],
                                 n_prompt)
    h, hb, route = _attn_out(o_prompt, o_sample, xp, xs, mw["wo"], lw)
    x = _ffn(h, hb, route, pp[0], ps[0], lw)
    kpe = kpe_slot[:, QK_NOPE:QK_NOPE + QK_ROPE]

    lw = _layer_tables(1, *layer_args)
    n_chunks = n_prompt // SSM_T
    levels = max(1, (n_chunks - 1).bit_length())
    tabs = _ssm_tables(ssm_a_re[0], ssm_a_im[0], ssm_log_dt[0], ssm_b_re[0], ssm_b_im[0], ssm_c_re[0],
                       ssm_c_im[0], levels)
    u = _ssm_in(x, ssm_w_in[0].astype(BF16))
    h0 = _block_pairs(_pair_states(state_ssm_re[0], state_ssm_im[0]))
    y, hp, hs = _ssm_core(u, tabs, h0, n_chunks)
    h, hb, route = _ssm_out(y, u, x, ssm_d[0].reshape(1, d), ssm_w_glu[0].astype(BF16), lw)
    y_prompt, y_sample = _ffn(h, hb, route, pp[1], ps[1], lw, split_out=True)
    re_p, im_p = _unpair_states(_unblock_pairs(hp))
    re_s, im_s = _unpair_states(_unblock_pairs(hs))

    return (y_prompt.reshape(1, n_prompt, d), y_sample.reshape(nb, seq, d),
            ckv[:n_prompt].reshape(1, 1, n_prompt, KV_LORA), kpe[:n_prompt].reshape(1, 1, n_prompt, QK_ROPE),
            re_p[None], im_p[None],
            ckv[n_prompt:].reshape(1, nb, seq, KV_LORA), kpe[n_prompt:].reshape(1, nb, seq, QK_ROPE),
            re_s[None], im_s[None])
```

```python
import functools
import math

import jax
import jax.numpy as jnp
import numpy as np
from jax import lax
from jax.experimental import pallas as pl
from jax.experimental.pallas import tpu as pltpu

F32 = jnp.float32
BF16 = jnp.bfloat16

N_HEADS = 16
Q_LORA = 256
KV_LORA = 128
QK_NOPE = 64
QK_ROPE = 32
V_DIM = 64
ROPE_BASE = 10000.0
ATTN_SCALE = (QK_NOPE + QK_ROPE) ** -0.5
CHUNK = 64
SSM_GROUP_CH = 16
SSM_STATE = 64
MOE_GROUPS = 4
MOE_EPG = 8
N_EXPERTS = MOE_GROUPS * MOE_EPG
DEPTH = 2
ALPHA = (2.0 * DEPTH) ** 0.25
LN_EPS = 1e-5
RMS_EPS = 1e-6
NEG = -1e30

LANES = 128
SLOT = LANES
ROW_TILE = 256
SSM_T = 16
VMEM_LIMIT = 56 * 1024 * 1024
ROUTER_LANES = LANES
EXPERT_LANE0 = MOE_GROUPS


def _cparams(sem):
    return pltpu.CompilerParams(dimension_semantics=sem, vmem_limit_bytes=VMEM_LIMIT)


def _dot(a, b):
    return jnp.dot(a, b, preferred_element_type=F32)


def _dot_nt(a, b):
    return lax.dot_general(a, b, (((1,), (1,)), ((), ())), preferred_element_type=F32)


def _full(shape):
    nd = len(shape)
    return pl.BlockSpec(shape, lambda *_: (0,) * nd)


def _rows(width, rows=ROW_TILE):
    return pl.BlockSpec((rows, width), lambda i: (i, 0))


def _split_rows(width, prompt_tiles):
    return [pl.BlockSpec((ROW_TILE, width), lambda i: (jnp.minimum(i, prompt_tiles - 1), 0)),
            pl.BlockSpec((ROW_TILE, width), lambda i: (jnp.maximum(i - prompt_tiles, 0), 0))]


def _pick_rows(prompt_ref, sample_ref, prompt_tiles):
    return jnp.where(pl.program_id(0) < prompt_tiles, prompt_ref[...], sample_ref[...])


def _layernorm(t, g, b):
    mu = jnp.mean(t, axis=-1, keepdims=True)
    d = t - mu
    var = jnp.mean(d * d, axis=-1, keepdims=True)
    return d * lax.rsqrt(var + LN_EPS) * g + b


def _rmsnorm(t, g):
    return t * lax.rsqrt(jnp.mean(t * t, axis=-1, keepdims=True) + RMS_EPS) * g


def _mla_proj_kernel(xp_ref, xs_ref, ct_ref, st_ref, w_in_ref, qn_ref, kvn_ref, wqa_ref, wqb_ref, wk_ref, wv_ref,
                     vone_ref, q_ref, k_ref, v_ref, ckv_ref, kpe_ref, *, prompt_tiles):
    x = _pick_rows(xp_ref, xs_ref, prompt_tiles).astype(BF16)
    z = _dot(x, w_in_ref[...])
    cq = _rmsnorm(z[:, :Q_LORA], qn_ref[...])
    ckv = _rmsnorm(z[:, Q_LORA:Q_LORA + KV_LORA], kvn_ref[...])
    ct = ct_ref[...]
    st = st_ref[...]
    o = Q_LORA + KV_LORA
    kpe = z[:, o:o + SLOT] * ct + z[:, o + SLOT:o + 2 * SLOT] * st
    ckv_ref[...] = ckv
    kpe_ref[...] = kpe
    cqb = cq.astype(BF16)
    ckvb = ckv.astype(BF16)
    qa = _dot(cqb, wqa_ref[...])
    qb = _dot(cqb, wqb_ref[...])
    kn = _dot(ckvb, wk_ref[...])
    vv = _dot(ckvb, wv_ref[...]) + vone_ref[...]
    scale = ATTN_SCALE * math.log2(math.e)
    for h in range(N_HEADS):
        sl = slice(h * SLOT, (h + 1) * SLOT)
        q_ref[:, sl] = ((qa[:, sl] * ct + qb[:, sl] * st) * scale).astype(BF16)
        k_ref[:, sl] = (kn[:, sl] + kpe).astype(BF16)
    v_ref[...] = vv.astype(BF16)


def _mla_proj(xp, xs, ct, st, w):
    n = xp.shape[0] + xs.shape[0]
    pt = xp.shape[0] // ROW_TILE
    wide = N_HEADS * SLOT
    return pl.pallas_call(
        functools.partial(_mla_proj_kernel, prompt_tiles=pt),
        grid=(n // ROW_TILE,),
        in_specs=_split_rows(xp.shape[1], pt) + [_rows(SLOT), _rows(SLOT), _full(w["w_in"].shape), _full((1, Q_LORA)),
                  _full((1, KV_LORA)), _full(w["wqa"].shape), _full(w["wqb"].shape), _full(w["wk"].shape),
                  _full(w["wv"].shape), _full((1, wide))],
        out_specs=[_rows(wide), _rows(wide), _rows(wide), _rows(KV_LORA), _rows(SLOT)],
        out_shape=[jax.ShapeDtypeStruct((n, wide), BF16)] * 3
        + [jax.ShapeDtypeStruct((n, KV_LORA), F32), jax.ShapeDtypeStruct((n, SLOT), F32)],
        compiler_params=_cparams(("parallel",)),
        name="mla_proj",
    )(xp, xs, ct, st, w["w_in"], w["qn"], w["kvn"], w["wqa"], w["wqb"], w["wk"], w["wv"], w["vone"])


ATT_TQ = 1024
ATT_TK = 2048
ATT_G = 2


def _attn_kernel(q_ref, k_ref, v_ref, o_ref):
    qi = pl.program_id(1)
    slots = [slice(g * SLOT, (g + 1) * SLOT) for g in range(ATT_G)]
    ratio = ATT_TK // ATT_TQ

    def update(off, tk, state, mask):
        out = []
        for sl, (m, acc) in zip(slots, state):
            s = _dot_nt(q_ref[:, sl], k_ref[pl.ds(off, tk), sl])
            if mask is not None:
                s = jnp.where(mask, s, NEG)
            m_new = jnp.maximum(m, jnp.max(s, axis=-1, keepdims=True))
            p = jnp.exp2(s - m_new)
            pv = _dot(p.astype(BF16), v_ref[pl.ds(off, tk), sl])
            out.append((m_new, acc * jnp.exp2(m - m_new) + pv))
        return tuple(out)

    def body(j, state):
        return update(pl.multiple_of(j * ATT_TK, ATT_TK), ATT_TK, state, None)

    init = tuple((jnp.full((ATT_TQ, 1), NEG, F32), jnp.zeros((ATT_TQ, SLOT), F32)) for _ in range(ATT_G))
    state = lax.fori_loop(0, qi // ratio, body, init)

    def tail(rem):
        tk = (rem + 1) * ATT_TQ
        rc = lax.broadcasted_iota(jnp.int32, (ATT_TQ, tk), 0) // CHUNK
        cc = lax.broadcasted_iota(jnp.int32, (ATT_TQ, tk), 1) // CHUNK - rem * (ATT_TQ // CHUNK)
        return lambda st: update(pl.multiple_of((qi - rem) * ATT_TQ, ATT_TQ), tk, st, cc <= rc)

    state = lax.switch(qi % ratio, [tail(rem) for rem in range(ratio)], state)
    lane = lax.broadcasted_iota(jnp.int32, (ATT_TQ, SLOT), 1)
    outs = []
    for g in range(ATT_G):
        acc = state[g][1]
        l = jnp.sum(jnp.where(lane == V_DIM, acc, 0.0), axis=-1, keepdims=True)
        outs.append((acc / l)[:, :V_DIM])
    o_ref[...] = jnp.concatenate(outs, axis=1).astype(BF16)


def _prompt_attention(q, k, v, n_prompt):
    n = q.shape[0]
    wide = ATT_G * SLOT
    assert ATT_TK % ATT_TQ == 0 and ATT_TQ % CHUNK == 0 and n_prompt % ATT_TQ == 0 and N_HEADS % ATT_G == 0
    assert (ATT_G * V_DIM) % LANES == 0
    return pl.pallas_call(
        _attn_kernel,
        grid=(N_HEADS // ATT_G, n_prompt // ATT_TQ),
        in_specs=[pl.BlockSpec((ATT_TQ, wide), lambda h, i: (i, h)),
                  pl.BlockSpec((n_prompt, wide), lambda h, i: (0, h)),
                  pl.BlockSpec((n_prompt, wide), lambda h, i: (0, h))],
        out_specs=pl.BlockSpec((ATT_TQ, ATT_G * V_DIM), lambda h, i: (i, h)),
        out_shape=jax.ShapeDtypeStruct((n_prompt, N_HEADS * V_DIM), BF16),
        compiler_params=_cparams(("parallel", "arbitrary")),
        name="prompt_attention",
    )(q, k, v)


def _sample_attn_kernel(q_ref, ckvc_ref, kpec_ref, ckvn_ref, kpen_ref, wabs_ref, wuv_ref, o_ref):
    q = q_ref[...]
    seq = q.shape[0]
    qa, qp = [], []
    for h in range(N_HEADS):
        qh = q[:, h * SLOT:(h + 1) * SLOT]
        qa.append(_dot(qh, wabs_ref[h]))
        qp.append(qh[:, QK_NOPE:QK_NOPE + QK_ROPE])
    qa = jnp.concatenate(qa, axis=0).astype(BF16)
    qp = jnp.concatenate(qp, axis=0)
    ckvc = ckvc_ref[0].astype(BF16)
    kpec = kpec_ref[0].astype(BF16)
    ckvn = ckvn_ref[...].astype(BF16)
    kpen = kpen_ref[...][:, QK_NOPE:QK_NOPE + QK_ROPE].astype(BF16)
    s_c = _dot_nt(qa, ckvc) + _dot_nt(qp, kpec)
    s_n = _dot_nt(qa, ckvn) + _dot_nt(qp, kpen)
    m = jnp.maximum(jnp.max(s_c, axis=-1, keepdims=True), jnp.max(s_n, axis=-1, keepdims=True))
    p_c = jnp.exp2(s_c - m)
    p_n = jnp.exp2(s_n - m)
    l = jnp.sum(p_c, axis=-1, keepdims=True) + jnp.sum(p_n, axis=-1, keepdims=True)
    ol = (_dot(p_c.astype(BF16), ckvc) + _dot(p_n.astype(BF16), ckvn)) / l
    olb = ol.astype(BF16)
    outs = [_dot(olb[h * seq:(h + 1) * seq], wuv_ref[h]) for h in range(N_HEADS)]
    o_ref[...] = jnp.concatenate(outs, axis=1).astype(BF16)


def _sample_attention(q, cache_ckv, cache_kpe, ckv_new, kpe_new, wabs, wuv, n_prompt):
    nb, past, _ = cache_ckv.shape
    seq = (q.shape[0] - n_prompt) // nb
    base = n_prompt // seq
    wide = N_HEADS * SLOT
    return pl.pallas_call(
        _sample_attn_kernel,
        grid=(nb,),
        in_specs=[pl.BlockSpec((seq, wide), lambda b: (base + b, 0)),
                  pl.BlockSpec((1, past, KV_LORA), lambda b: (b, 0, 0)),
                  pl.BlockSpec((1, past, QK_ROPE), lambda b: (b, 0, 0)),
                  pl.BlockSpec((seq, KV_LORA), lambda b: (base + b, 0)),
                  pl.BlockSpec((seq, SLOT), lambda b: (base + b, 0)),
                  _full(wabs.shape), _full(wuv.shape)],
        out_specs=pl.BlockSpec((seq, N_HEADS * V_DIM), lambda b: (b, 0)),
        out_shape=jax.ShapeDtypeStruct((nb * seq, N_HEADS * V_DIM), BF16),
        compiler_params=_cparams(("parallel",)),
        name="sample_attention",
    )(q, cache_ckv, cache_kpe, ckv_new, kpe_new, wabs, wuv)


ROUTE_E1, ROUTE_E2, ROUTE_W1, ROUTE_W2 = range(4)


def _lane_col(tile, k):
    lane = lax.broadcasted_iota(jnp.int32, tile.shape, 1)
    return jnp.sum(jnp.where(lane == k, tile, 0.0), axis=-1, keepdims=True)


def _route(lg):
    lane = lax.broadcasted_iota(jnp.int32, lg.shape, 1)
    lanef = lane.astype(F32)
    big = float(ROUTER_LANES)
    is_g = lane < MOE_GROUPS
    gl = jnp.where(is_g, lg, NEG)
    gmax = jnp.max(gl, axis=-1, keepdims=True)
    gsel = jnp.min(jnp.where(gl == gmax, lanef, big), axis=-1, keepdims=True)
    gate_g = 1.0 / jnp.sum(jnp.where(is_g, jnp.exp(gl - gmax), 0.0), axis=-1, keepdims=True)
    lo = EXPERT_LANE0 + MOE_EPG * gsel
    el = jnp.where(jnp.logical_and(lanef >= lo, lanef < lo + MOE_EPG), lg, NEG)
    m1 = jnp.max(el, axis=-1, keepdims=True)
    i1 = jnp.min(jnp.where(el == m1, lanef, big), axis=-1, keepdims=True)
    el2 = jnp.where(lanef == i1, NEG, el)
    m2 = jnp.max(el2, axis=-1, keepdims=True)
    i2 = jnp.min(jnp.where(el2 == m2, lanef, big), axis=-1, keepdims=True)
    r = jnp.exp(m2 - m1)
    w1 = 1.0 / (1.0 + r)
    w2 = r / (1.0 + r)
    cols = {ROUTE_E1: i1 - EXPERT_LANE0, ROUTE_E2: i2 - EXPERT_LANE0, ROUTE_W1: gate_g * w1, ROUTE_W2: gate_g * w2}
    route = jnp.zeros(lg.shape, F32)
    for k, c in cols.items():
        route = jnp.where(lane == k, c, route)
    return route


def _post_mix(t, x, g, b, wr_hi, wr_lo, br, h_ref, hb_ref, route_ref):
    h = _layernorm(ALPHA * x + t, g, b)
    h_ref[...] = h
    hb = h.astype(BF16)
    hb_ref[...] = hb
    h_lo = (h - hb.astype(F32)).astype(BF16)
    lg = _dot(hb, wr_hi) + (_dot(hb, wr_lo) + _dot(h_lo, wr_hi)) + br
    route_ref[...] = _route(lg)


def _attn_out_kernel(op_ref, os_ref, xp_ref, xs_ref, wo_ref, g_ref, b_ref, wrh_ref, wrl_ref, br_ref, h_ref, hb_ref,
                     route_ref, *, prompt_tiles):
    t = _dot(_pick_rows(op_ref, os_ref, prompt_tiles), wo_ref[...])
    _post_mix(t, _pick_rows(xp_ref, xs_ref, prompt_tiles), g_ref[...], b_ref[...], wrh_ref[...], wrl_ref[...],
              br_ref[...], h_ref, hb_ref, route_ref)


def _post_out(n, d):
    specs = [_rows(d), _rows(d), _rows(ROUTER_LANES)]
    shapes = [jax.ShapeDtypeStruct((n, d), F32), jax.ShapeDtypeStruct((n, d), BF16),
              jax.ShapeDtypeStruct((n, ROUTER_LANES), F32)]
    return specs, shapes


def _attn_out(o_prompt, o_sample, xp, xs, wo, lw):
    d = xp.shape[1]
    n = xp.shape[0] + xs.shape[0]
    pt = xp.shape[0] // ROW_TILE
    specs, shapes = _post_out(n, d)
    return pl.pallas_call(
        functools.partial(_attn_out_kernel, prompt_tiles=pt),
        grid=(n // ROW_TILE,),
        in_specs=_split_rows(o_prompt.shape[1], pt) + _split_rows(d, pt)
        + [_full(wo.shape), _full((1, d)), _full((1, d)),
           _full(lw["wr_hi"].shape), _full(lw["wr_lo"].shape), _full((1, ROUTER_LANES))],
        out_specs=specs, out_shape=shapes,
        compiler_params=_cparams(("parallel",)),
        name="attn_out_ln_router",
    )(o_prompt, o_sample, xp, xs, wo, lw["ln1_g"], lw["ln1_b"], lw["wr_hi"], lw["wr_lo"], lw["br"])


MOE_EPS = 8
MOE_TILE_MAX = 896


def _moe_dense_kernel(hb_ref, route_ref, wg_ref, wu_ref, wd_ref, y_ref):
    x = hb_ref[...]
    route = route_ref[...]
    e1 = _lane_col(route, ROUTE_E1)
    e2 = _lane_col(route, ROUTE_E2)
    w1 = _lane_col(route, ROUTE_W1)
    w2 = _lane_col(route, ROUTE_W2)
    y = None
    for k in range(MOE_EPS):
        e = (pl.program_id(1) * MOE_EPS + k).astype(F32)
        g = _dot(x, wg_ref[k])
        u = _dot(x, wu_ref[k])
        gate = jnp.where(e1 == e, w1, 0.0) + jnp.where(e2 == e, w2, 0.0)
        hdn = (g * jax.nn.sigmoid(g)) * u * gate
        yk = _dot(hdn.astype(BF16), wd_ref[k])
        y = yk if y is None else y + yk

    @pl.when(pl.program_id(1) == 0)
    def _():
        y_ref[...] = y

    @pl.when(pl.program_id(1) > 0)
    def _():
        y_ref[...] += y


def _moe_tile(n):
    return max(t for t in range(16, MOE_TILE_MAX + 1, 16) if n % t == 0)


def _moe_dense(hb, route, wg, wu, wd):
    n, d = hb.shape
    ff = wg.shape[-1]
    tile = _moe_tile(n)
    return pl.pallas_call(
        _moe_dense_kernel,
        grid=(n // tile, N_EXPERTS // MOE_EPS),
        in_specs=[pl.BlockSpec((tile, d), lambda i, e: (i, 0)),
                  pl.BlockSpec((tile, ROUTER_LANES), lambda i, e: (i, 0)),
                  pl.BlockSpec((MOE_EPS, d, ff), lambda i, e: (e, 0, 0)),
                  pl.BlockSpec((MOE_EPS, d, ff), lambda i, e: (e, 0, 0)),
                  pl.BlockSpec((MOE_EPS, ff, d), lambda i, e: (e, 0, 0))],
        out_specs=pl.BlockSpec((tile, d), lambda i, e: (i, 0)),
        out_shape=jax.ShapeDtypeStruct((n, d), F32),
        compiler_params=_cparams(("parallel", "arbitrary")),
        name="moe_dense",
    )(hb, route, wg, wu, wd)


def _ln2_ple_kernel(h_ref, y_ref, pp_ref, ps_ref, g_ref, b_ref, wp_ref, wpg_ref, *rest, prompt_tiles, fuse_next):
    wn_ref, out_refs = (rest[0], rest[1:]) if fuse_next else (None, rest)
    h2 = _layernorm(ALPHA * h_ref[...] + y_ref[...], g_ref[...], b_ref[...])
    proj = _dot(_pick_rows(pp_ref, ps_ref, prompt_tiles).astype(BF16), wp_ref[...])
    gate = jax.nn.sigmoid(_dot(h2.astype(BF16), wpg_ref[...]))
    x = h2 + proj * gate
    if fuse_next:
        out_refs[0][...] = x
        out_refs[1][...] = _dot(x.astype(BF16), wn_ref[...])
    elif len(out_refs) == 1:
        out_refs[0][...] = x
    else:
        is_prompt = pl.program_id(0) < prompt_tiles

        @pl.when(is_prompt)
        def _():
            out_refs[0][...] = x

        @pl.when(jnp.logical_not(is_prompt))
        def _():
            out_refs[1][...] = x


def _ln2_ple(h, y, pp, ps, lw, split_out, w_next):
    n, d = h.shape
    pt = pp.shape[0] // ROW_TILE
    extra = []
    if w_next is not None:
        assert not split_out
        extra = [w_next]
        out_specs = [_rows(d), _rows(w_next.shape[1])]
        out_shape = [jax.ShapeDtypeStruct((n, d), F32), jax.ShapeDtypeStruct((n, w_next.shape[1]), F32)]
    elif split_out:
        out_specs = _split_rows(d, pt)
        out_shape = [jax.ShapeDtypeStruct((pp.shape[0], d), F32), jax.ShapeDtypeStruct((ps.shape[0], d), F32)]
    else:
        out_specs, out_shape = _rows(d), jax.ShapeDtypeStruct((n, d), F32)
    return pl.pallas_call(
        functools.partial(_ln2_ple_kernel, prompt_tiles=pt, fuse_next=w_next is not None),
        grid=(n // ROW_TILE,),
        in_specs=[_rows(d), _rows(d)] + _split_rows(pp.shape[1], pt)
        + [_full((1, d)), _full((1, d)), _full(lw["w_proj"].shape), _full(lw["w_pg"].shape)]
        + [_full(w.shape) for w in extra],
        out_specs=out_specs, out_shape=out_shape,
        compiler_params=_cparams(("arbitrary",)),
        name="ln2_ple",
    )(h, y, pp, ps, lw["ln2_g"], lw["ln2_b"], lw["w_proj"], lw["w_pg"], *extra)


def _ffn(h, hb, route, pp, ps, lw, split_out=False, w_next=None):
    y = _moe_dense(hb, route, lw["wg"], lw["wu"], lw["wd"])
    return _ln2_ple(h, y, pp, ps, lw, split_out, w_next)


SSM_GB = LANES // SSM_GROUP_CH
SSM_PAIR = 2 * LANES


def _ssm_core_kernel(u_ref, bd_ref, bst_ref, cst_ref, apow_ref, h0_ref, y_ref, hp_ref, hs_ref, sre_ref, sim_ref,
                     *, n_chunks, levels):
    t = SSM_T
    rows = u_ref.shape[0] // t
    pad = sre_ref.shape[0] - n_chunks
    npair = SSM_GB // 2
    ut = [u_ref[pl.ds(tl, rows, stride=t), :].astype(BF16) for tl in range(t)]
    u2 = [jnp.concatenate([ut[2 * k], ut[2 * k + 1]], axis=1) for k in range(t // 2)]
    bb = None
    for k in range(t // 2):
        part = _dot(u2[k], jnp.concatenate([bst_ref[0, 2 * k], bst_ref[0, 2 * k + 1]], axis=0))
        bb = part if bb is None else bb + part
    zeros = jnp.zeros((pad, LANES), F32)
    sre_ref[:pad, :] = zeros
    sim_ref[:pad, :] = zeros

    def shifted(d):
        return sre_ref[pl.ds(pad - d, n_chunks), :], sim_ref[pl.ds(pad - d, n_chunks), :]

    h0 = h0_ref[0]
    hprev, h_last, h_samp = [], [], []
    for j in range(npair):
        lo = j * SSM_PAIR
        re = bb[:n_chunks, lo:lo + LANES]
        im = bb[:n_chunks, lo + LANES:lo + SSM_PAIR]
        for lv in range(levels):
            sre_ref[pad:, :] = re
            sim_ref[pad:, :] = im
            pr, pi = shifted(1 << lv)
            ar = apow_ref[0, lv:lv + 1, lo:lo + LANES]
            ai = apow_ref[0, lv:lv + 1, lo + LANES:lo + SSM_PAIR]
            re, im = re + ar * pr - ai * pi, im + ar * pi + ai * pr
        sre_ref[pad:, :] = re
        sim_ref[pad:, :] = im
        pr, pi = shifted(1)
        h0j = h0[:, lo:lo + SSM_PAIR]
        hprev.append(jnp.concatenate([jnp.concatenate([pr, pi], axis=1), h0j], axis=0).astype(BF16))
        h_last.append(jnp.concatenate([re[n_chunks - 1:, :], im[n_chunks - 1:, :]], axis=1))
        ar = apow_ref[0, 0:1, lo:lo + LANES]
        ai = apow_ref[0, 0:1, lo + LANES:lo + SSM_PAIR]
        h0r = h0j[:, :LANES]
        h0i = h0j[:, LANES:]
        h_samp.append(jnp.concatenate([ar * h0r - ai * h0i + bb[n_chunks:, lo:lo + LANES],
                                       ar * h0i + ai * h0r + bb[n_chunks:, lo + LANES:lo + SSM_PAIR]], axis=1))
    hp_ref[0] = jnp.concatenate(h_last, axis=1)
    hs_ref[0] = jnp.concatenate(h_samp, axis=1)
    hprev = jnp.concatenate(hprev, axis=1)
    zero_tap = jnp.zeros((LANES, LANES), BF16)

    def tap(lag):
        return bd_ref[0, lag] if lag >= 0 else zero_tap

    for ko in range(t // 2):
        acc = _dot_nt(hprev, jnp.concatenate([cst_ref[0, 2 * ko], cst_ref[0, 2 * ko + 1]], axis=0))
        for ki in range(ko + 1):
            lag = 2 * (ko - ki)
            w = jnp.concatenate([jnp.concatenate([tap(lag), tap(lag + 1)], axis=1),
                                 jnp.concatenate([tap(lag - 1), tap(lag)], axis=1)], axis=0)
            acc = acc + _dot(u2[ki], w)
        y_ref[pl.ds(2 * ko, rows, stride=t), :] = acc[:, :LANES]
        y_ref[pl.ds(2 * ko + 1, rows, stride=t), :] = acc[:, LANES:]


def _ssm_core(u, tabs, h0, n_chunks):
    n, d = u.shape
    nblk = d // LANES
    ns = n // SSM_T - n_chunks
    levels = max(1, (n_chunks - 1).bit_length())
    pad = max(1 << (levels - 1), 8)
    wide = (SSM_GB // 2) * SSM_PAIR
    kern = functools.partial(_ssm_core_kernel, n_chunks=n_chunks, levels=levels)
    once = dict(pipeline_mode=pl.Buffered(1))

    def blk(*shape):
        return pl.BlockSpec((1,) + shape, lambda i: (i,) + (0,) * len(shape), **once)

    return pl.pallas_call(
        kern,
        grid=(nblk,),
        in_specs=[pl.BlockSpec((n, LANES), lambda i: (0, i), **once), blk(SSM_T, LANES, LANES),
                  blk(SSM_T, LANES, wide), blk(SSM_T, LANES, wide), blk(tabs["apow"].shape[1], wide), blk(ns, wide)],
        out_specs=[pl.BlockSpec((n, LANES), lambda i: (0, i)),
                   pl.BlockSpec((1, 1, wide), lambda i: (i, 0, 0)), pl.BlockSpec((1, ns, wide), lambda i: (i, 0, 0))],
        out_shape=[jax.ShapeDtypeStruct((n, d), F32), jax.ShapeDtypeStruct((nblk, 1, wide), F32),
                   jax.ShapeDtypeStruct((nblk, ns, wide), F32)],
        scratch_shapes=[pltpu.VMEM((pad + n_chunks, LANES), F32), pltpu.VMEM((pad + n_chunks, LANES), F32)],
        compiler_params=_cparams(("parallel",)),
        name="ssm_core",
    )(u, tabs["bd"], tabs["bst"], tabs["cst"], tabs["apow"], h0)


def _gelu_tanh(y):
    c = math.sqrt(2.0 / math.pi)
    return 0.5 * y * (1.0 + jnp.tanh(c * (y + 0.044715 * (y * y * y))))


def _ssm_out_kernel(y_ref, u_ref, x_ref, d_ref, wglu_ref, g_ref, b_ref, wrh_ref, wrl_ref, br_ref,
                    h_ref, hb_ref, route_ref):
    y = y_ref[...] + d_ref[...] * u_ref[...]
    z = _dot(_gelu_tanh(y).astype(BF16), wglu_ref[...])
    dm = z.shape[1] // 2
    t = z[:, :dm] * jax.nn.sigmoid(z[:, dm:])
    _post_mix(t, x_ref[...], g_ref[...], b_ref[...], wrh_ref[...], wrl_ref[...], br_ref[...],
              h_ref, hb_ref, route_ref)


def _ssm_out(y, u, x, d_skip, wglu, lw):
    n, d = x.shape
    specs, shapes = _post_out(n, d)
    return pl.pallas_call(
        _ssm_out_kernel,
        grid=(n // ROW_TILE,),
        in_specs=[_rows(d), _rows(d), _rows(d), _full((1, d)), _full(wglu.shape), _full((1, d)), _full((1, d)),
                  _full(lw["wr_hi"].shape), _full(lw["wr_lo"].shape), _full((1, ROUTER_LANES))],
        out_specs=specs, out_shape=shapes,
        compiler_params=_cparams(("parallel",)),
        name="ssm_out_ln_router",
    )(y, u, x, d_skip, wglu, lw["ln1_g"], lw["ln1_b"], lw["wr_hi"], lw["wr_lo"], lw["br"])


def _rope_tables(pos):
    half = QK_ROPE // 2
    inv = ROPE_BASE ** (-jnp.arange(half, dtype=F32) / half)
    n = pos.shape[0]
    per = LANES // half
    assert n % per == 0
    ang = (pos.astype(F32).reshape(n // per, per, 1) * inv.reshape(1, 1, half)).reshape(n // per, LANES)
    cos, sin = jnp.cos(ang).reshape(n, half), jnp.sin(ang).reshape(n, half)
    ones = jnp.ones((n, QK_NOPE), F32)
    zeros = jnp.zeros((n, SLOT - QK_NOPE - QK_ROPE), F32)
    ct = jnp.concatenate([ones, cos, cos, zeros], axis=1)
    st = jnp.concatenate([jnp.zeros((n, QK_NOPE), F32), -sin, sin, zeros], axis=1)
    return ct, st


def _slot_cols(w, width):
    k = w.shape[0]
    return jnp.pad(w, ((0, 0), (0, 0), (0, SLOT - width))).reshape(k, N_HEADS * SLOT)


def _mla_tables(w_in, q_norm, kv_norm, w_uq, w_uk, w_uv, w_o):
    half = QK_ROPE // 2
    o = Q_LORA + KV_LORA
    d = w_in.shape[0]
    kpe_w = w_in[:, o:]
    kpe_sw = jnp.concatenate([kpe_w[:, half:], kpe_w[:, :half]], axis=1)
    zl = jnp.zeros((d, QK_NOPE), F32)
    zr = jnp.zeros((d, SLOT - QK_NOPE - QK_ROPE), F32)
    w_in_e = jnp.concatenate([w_in[:, :o], zl, kpe_w, zr, zl, kpe_sw, zr], axis=1)
    wq = w_uq.reshape(Q_LORA, N_HEADS, QK_NOPE + QK_ROPE)
    pe = wq[:, :, QK_NOPE:]
    pe_sw = jnp.concatenate([pe[:, :, half:], pe[:, :, :half]], axis=2)
    wqb = jnp.concatenate([jnp.zeros_like(wq[:, :, :QK_NOPE]), pe_sw], axis=2)
    vone = jnp.zeros((N_HEADS, SLOT), F32).at[:, V_DIM].set(1.0).reshape(1, N_HEADS * SLOT)
    wabs = jnp.pad(jnp.transpose(w_uk, (1, 2, 0)), ((0, 0), (0, SLOT - QK_NOPE), (0, 0)))
    wuv = jnp.transpose(w_uv, (1, 0, 2))
    return dict(
        w_in=w_in_e.astype(BF16), qn=q_norm.reshape(1, -1), kvn=kv_norm.reshape(1, -1),
        wqa=_slot_cols(wq, QK_NOPE + QK_ROPE).astype(BF16), wqb=_slot_cols(wqb, QK_NOPE + QK_ROPE).astype(BF16),
        wk=_slot_cols(w_uk, QK_NOPE).astype(BF16), wv=_slot_cols(w_uv, V_DIM).astype(BF16), vone=vone,
        wabs=wabs.astype(BF16), wuv=wuv.astype(BF16), wo=w_o.astype(BF16))


def _cmul(a, b):
    return a[0] * b[0] - a[1] * b[1], a[0] * b[1] + a[1] * b[0]


def _ssm_tables(a_re, a_im, log_dt, b_re, b_im, c_re, c_im, levels):
    t = SSM_T
    g, p = a_re.shape
    c = SSM_GROUP_CH
    hi = lax.Precision.HIGHEST
    dt = jnp.exp(log_dt)[:, None]
    mag = jnp.exp(a_re * dt)
    lam_bar = (mag * jnp.cos(a_im * dt), mag * jnp.sin(a_im * dt))
    den = a_re * a_re + a_im * a_im
    quo = (((lam_bar[0] - 1.0) * a_re + lam_bar[1] * a_im) / den,
           (lam_bar[1] * a_re - (lam_bar[0] - 1.0) * a_im) / den)
    b_bar = _cmul((quo[0][:, :, None], quo[1][:, :, None]), (b_re, b_im))
    pw = [(jnp.ones_like(a_re), jnp.zeros_like(a_re))]
    for _ in range(t):
        pw.append(_cmul(pw[-1], lam_bar))
    pw_r = jnp.stack([x[0] for x in pw])
    pw_i = jnp.stack([x[1] for x in pw])
    cp = _cmul((c_re[:, None], c_im[:, None]),
               (jnp.transpose(pw_r[:t], (1, 0, 2))[:, :, None, :], jnp.transpose(pw_i[:t], (1, 0, 2))[:, :, None, :]))
    taps = jnp.einsum("gmk,gki->gmi", jnp.concatenate([cp[0], -cp[1]], axis=-1).reshape(g, t * c, 2 * p),
                      jnp.concatenate([b_bar[0], b_bar[1]], axis=1), precision=hi)
    nblk = g // SSM_GB
    width = SSM_GB * c
    tg = jnp.transpose(taps.reshape(nblk, SSM_GB, t, c, c), (0, 2, 1, 4, 3))
    own = np.zeros((SSM_GB, c, SSM_GB, c), np.float32)
    for g8 in range(SSM_GB):
        own[g8, np.arange(c), g8, np.arange(c)] = 1.0
    own = jnp.asarray(own.reshape(SSM_GB, c, width), BF16)
    bd = jnp.einsum("btgio,gol->btgil", tg.astype(BF16), own, preferred_element_type=BF16)
    bd = bd.reshape(nblk, t, width, width)
    lanes = (SSM_GB // 2) * SSM_PAIR
    place = np.zeros((SSM_GB, 2, p, SSM_GB // 2, 2, 2, p), np.float32)
    for g8 in range(SSM_GB):
        for ri in range(2):
            place[g8, ri, np.arange(p), g8 // 2, ri, g8 % 2, np.arange(p)] = 1.0
    place = jnp.asarray(place.reshape(SSM_GB, 2 * p, lanes), BF16)

    def spread(v):
        v = jnp.transpose(v, (2, 1, 3, 5, 0, 4)).reshape(nblk, t, SSM_GB, c, 2 * p)
        out = jnp.einsum("btgck,gkl->btgcl", v.astype(BF16), place, preferred_element_type=BF16)
        return out.reshape(nblk, t, width, lanes)

    rev = t - 1 - jnp.arange(t)
    bfl = _cmul((pw_r[rev][:, :, :, None], pw_i[rev][:, :, :, None]), (b_bar[0][None], b_bar[1][None]))
    bst = spread(jnp.stack(bfl).reshape(2, t, nblk, SSM_GB, p, c))
    cfl = _cmul((c_re[None], c_im[None]), (pw_r[1:t + 1, :, None, :], pw_i[1:t + 1, :, None, :]))
    cv = jnp.stack([cfl[0], -cfl[1]]).reshape(2, t, nblk, SSM_GB, c, p)
    cst = spread(jnp.swapaxes(cv, 4, 5))
    ap = [pw[t]]
    for _ in range(levels - 1):
        ap.append(_cmul(ap[-1], ap[-1]))

    def pair_lanes(xs):
        v = jnp.stack(xs, axis=1).reshape(g // 2, 2, levels, p)
        return jnp.transpose(v, (0, 2, 1, 3)).reshape(g // 2, levels, 2 * p)

    apow = jnp.concatenate([pair_lanes([x[0] for x in ap]), pair_lanes([x[1] for x in ap])], axis=2)
    rows = -(-levels // 8) * 8
    apow = jnp.pad(apow, ((0, 0), (0, rows - levels), (0, 0)))
    return dict(bd=bd.astype(BF16), bst=bst.astype(BF16), cst=cst.astype(BF16), apow=_block_pairs(apow))


def _block_pairs(v):
    pairs, r, w = v.shape
    per = SSM_GB // 2
    return jnp.transpose(v.reshape(pairs // per, per, r, w), (0, 2, 1, 3)).reshape(pairs // per, r, per * w)


def _unblock_pairs(v):
    nblk, r, w = v.shape
    per = SSM_GB // 2
    return jnp.transpose(v.reshape(nblk, r, per, w // per), (0, 2, 1, 3)).reshape(nblk * per, r, w // per)


def _pair_states(re, im):
    b, g, p = re.shape
    r = jnp.transpose(re.reshape(b, g // 2, 2 * p), (1, 0, 2))
    i = jnp.transpose(im.reshape(b, g // 2, 2 * p), (1, 0, 2))
    return jnp.concatenate([r, i], axis=2)


def _unpair_states(h):
    pairs, b, w = h.shape
    p = w // 4
    r = jnp.transpose(h[:, :, :2 * p], (1, 0, 2)).reshape(b, pairs * 2, p)
    i = jnp.transpose(h[:, :, 2 * p:], (1, 0, 2)).reshape(b, pairs * 2, p)
    return r, i


def _layer_tables(i, ln1_g, ln1_b, ln2_g, ln2_b, w_rg, b_rg, w_re, b_re, w_gate, w_up, w_down, w_proj, w_pg):
    d = w_rg.shape[1]
    wr = jnp.concatenate([w_rg[i], jnp.transpose(w_re[i], (1, 0, 2)).reshape(d, N_EXPERTS)], axis=1)
    wr = jnp.pad(wr, ((0, 0), (0, ROUTER_LANES - wr.shape[1])))
    wr_hi = wr.astype(BF16)
    wr_lo = (wr - wr_hi.astype(F32)).astype(BF16)
    br = jnp.pad(jnp.concatenate([b_rg[i], b_re[i].reshape(-1)]), (0, ROUTER_LANES - MOE_GROUPS - N_EXPERTS))
    ff = w_gate.shape[-1]
    return dict(
        ln1_g=ln1_g[i].reshape(1, d), ln1_b=ln1_b[i].reshape(1, d), ln2_g=ln2_g[i].reshape(1, d),
        ln2_b=ln2_b[i].reshape(1, d), wr_hi=wr_hi, wr_lo=wr_lo, br=br.reshape(1, ROUTER_LANES),
        wg=w_gate[i].reshape(N_EXPERTS, d, ff).astype(BF16), wu=w_up[i].reshape(N_EXPERTS, d, ff).astype(BF16),
        wd=w_down[i].reshape(N_EXPERTS, ff, d).astype(BF16),
        w_proj=w_proj[i].astype(BF16), w_pg=w_pg[i].astype(BF16))


def kernel(x_prompt, x_sample, p_prompt, p_sample, cache_mla_ckv, cache_mla_kpe, state_ssm_re, state_ssm_im, mla_w_in, mla_q_norm, mla_kv_norm, mla_w_uq, mla_w_uk, mla_w_uv, mla_w_o, ssm_w_in, ssm_a_re, ssm_a_im, ssm_log_dt, ssm_b_re, ssm_b_im, ssm_c_re, ssm_c_im, ssm_d, ssm_w_glu, ln1_g, ln1_b, ln2_g, ln2_b, moe_w_rg, moe_b_rg, moe_w_re, moe_b_re, moe_w_gate, moe_w_up, moe_w_down, ple_w_proj, ple_w_gate):
    bp, n_prompt, d = x_prompt.shape
    nb, seq, _ = x_sample.shape
    past = cache_mla_ckv.shape[2]
    assert bp == 1 and seq == SSM_T and n_prompt % ROW_TILE == 0 and (nb * seq) % ROW_TILE == 0
    assert n_prompt % CHUNK == 0 and past % CHUNK == 0 and seq <= CHUNK
    n_samp = nb * seq
    n = n_prompt + n_samp
    xp = x_prompt.reshape(n_prompt, d)
    xs = x_sample.reshape(n_samp, d)
    pp = p_prompt.reshape(DEPTH, n_prompt, -1)
    ps = p_sample.reshape(DEPTH, n_samp, -1)
    layer_args = (ln1_g, ln1_b, ln2_g, ln2_b, moe_w_rg, moe_b_rg, moe_w_re, moe_b_re, moe_w_gate, moe_w_up,
                  moe_w_down, ple_w_proj, ple_w_gate)
    lw = _layer_tables(0, *layer_args)
    mw = _mla_tables(mla_w_in[0], mla_q_norm[0], mla_kv_norm[0], mla_w_uq[0], mla_w_uk[0], mla_w_uv[0], mla_w_o[0])
    pos = jnp.concatenate([jnp.arange(n_prompt, dtype=jnp.int32),
                           past + jnp.tile(jnp.arange(seq, dtype=jnp.int32), nb)])
    ct, st = _rope_tables(pos)
    q, k, v, ckv, kpe_slot = _mla_proj(xp, xs, ct, st, mw)
    o_prompt = _prompt_attention(q, k, v, n_prompt)
    o_sample = _sample_attention(q, cache_mla_ckv[0], cache_mla_kpe[0], ckv, kpe_slot, mw["wabs"], mw["wuv"],
                                 n_prompt)
    h, hb, route = _attn_out(o_prompt, o_sample, xp, xs, mw["wo"], lw)
    x, u = _ffn(h, hb, route, pp[0], ps[0], lw, w_next=ssm_w_in[0].astype(BF16))
    kpe = kpe_slot[:, QK_NOPE:QK_NOPE + QK_ROPE]

    lw = _layer_tables(1, *layer_args)
    n_chunks = n_prompt // SSM_T
    levels = max(1, (n_chunks - 1).bit_length())
    tabs = _ssm_tables(ssm_a_re[0], ssm_a_im[0], ssm_log_dt[0], ssm_b_re[0], ssm_b_im[0], ssm_c_re[0],
                       ssm_c_im[0], levels)
    h0 = _block_pairs(_pair_states(state_ssm_re[0], state_ssm_im[0]))
    y, hp, hs = _ssm_core(u, tabs, h0, n_chunks)
    h, hb, route = _ssm_out(y, u, x, ssm_d[0].reshape(1, d), ssm_w_glu[0].astype(BF16), lw)
    y_prompt, y_sample = _ffn(h, hb, route, pp[1], ps[1], lw, split_out=True)
    re_p, im_p = _unpair_states(_unblock_pairs(hp))
    re_s, im_s = _unpair_states(_unblock_pairs(hs))

    return (y_prompt.reshape(1, n_prompt, d), y_sample.reshape(nb, seq, d),
            ckv[:n_prompt].reshape(1, 1, n_prompt, KV_LORA), kpe[:n_prompt].reshape(1, 1, n_prompt, QK_ROPE),
            re_p[None], im_p[None],
            ckv[n_prompt:].reshape(1, nb, seq, KV_LORA), kpe[n_prompt:].reshape(1, nb, seq, QK_ROPE),
            re_s[None], im_s[None])
```
